```python
import jax, jax.numpy as jnp
from jax import lax
import numpy as np

D_MODEL = 1024
BATCH = 16
SEQ = 2048
DEPTH = 1
DEC_BATCH = 16
DEC_SEQ = 32
PAST_LEN = 1024

CHUNK = 64
N_META = 16
Q_BLOCK = 128
EPS = 1e-6
GLA_HEADS = 4
GLA_DK = 64
GLA_DV = 128
GLA_RANK = 16
GLA_TAU = 16.0
GLA_QK = GLA_HEADS * GLA_DK
GLA_V = GLA_HEADS * GLA_DV
FOX_HEADS = 8
FOX_DH = 64
FOX_W = FOX_HEADS * FOX_DH
FORGET_BIAS_INIT = 4.0
D_FF = 4 * D_MODEL
SPLIT_SIZES = (GLA_QK, GLA_QK, GLA_V, GLA_V, GLA_RANK, FOX_W, FOX_W, FOX_W, FOX_HEADS, D_MODEL, D_MODEL)
D_IN = GLA_QK + GLA_QK + GLA_V + GLA_V + GLA_RANK + FOX_W + FOX_W + FOX_W + FOX_HEADS + D_MODEL + D_MODEL
NEG = -1e30

kernel_name = 'hybrid_gla_fox_stream_step'


def _rmsnorm(x, g):
    xf = x.astype(jnp.float32)
    y = xf * lax.rsqrt(jnp.mean(xf * xf, axis=-1, keepdims=True) + EPS)
    return (y * g.astype(jnp.float32)).astype(x.dtype)


def _in_proj(hn, w_in):
    z = jnp.einsum('btd,de->bte', hn, w_in)
    cuts = [int(c) for c in np.cumsum(SPLIT_SIZES)[:-1]]
    return jnp.split(z, cuts, axis=-1)


def _gla_heads(gq, gk, gv, glr, w_gate, b_gate):
    B, T = gq.shape[0], gq.shape[1]
    f32 = jnp.float32
    q = gq.reshape(B, T, GLA_HEADS, GLA_DK).astype(f32) * (GLA_DK ** -0.5)
    k = gk.reshape(B, T, GLA_HEADS, GLA_DK).astype(f32)
    v = gv.reshape(B, T, GLA_HEADS, GLA_DV).astype(f32)
    z = jnp.einsum('btr,rk->btk', glr, w_gate) + b_gate
    log_a = (jax.nn.log_sigmoid(z.astype(f32)) / GLA_TAU).reshape(B, T, GLA_HEADS, GLA_DK)
    return q, k, v, log_a


def _gla_chunks(q, k, v, log_a, s0):
    g = jnp.cumsum(log_a, axis=2)
    g_tot = g[:, :, -1]
    k_dec = k * jnp.exp(g_tot[:, :, None] - g)
    u = jnp.einsum('bnchk,bnchv->bnhkv', k_dec, v)

    def step(s, inp):
        a, uc = inp
        s = jnp.exp(a)[..., None] * s + uc
        return s, s

    s_last, s_all = lax.scan(step, s0, (jnp.moveaxis(g_tot, 1, 0), jnp.moveaxis(u, 1, 0)))
    s_all = jnp.moveaxis(s_all, 0, 1)
    o = jnp.einsum('bnchk,bnhkv->bnchv', q, s_all)
    return o, s_last


def _gla_out(o, gr, g_norm, w_branch, dtype):
    B, T = o.shape[0], o.shape[1]
    on = _rmsnorm(o, g_norm)
    gate = jax.nn.silu(gr.astype(jnp.float32)).reshape(B, T, GLA_HEADS, GLA_DV)
    y = (on * gate).reshape(B, T, GLA_V).astype(dtype)
    return jnp.einsum('btv,vd->btd', y, w_branch)


def _fox_heads(fq, fk, fv, ff, b_f):
    B, T = fq.shape[0], fq.shape[1]
    sh = lambda a: a.reshape(B, T, FOX_HEADS, FOX_DH).transpose(0, 2, 1, 3)
    logf = jax.nn.log_sigmoid(ff.astype(jnp.float32) + b_f.astype(jnp.float32)).transpose(0, 2, 1)
    return sh(fq), sh(fk), sh(fv), logf


def _fox_attend(q, k, v, cq, ck, qpos, kpos):
    f32 = jnp.float32
    s = jnp.einsum('bhqd,bhkd->bhqk', q.astype(f32), k.astype(f32)) * (FOX_DH ** -0.5)
    s = s + cq[..., :, None] - ck[..., None, :]
    s = jnp.where(kpos[None, :] <= qpos[:, None], s, NEG)
    p = jax.nn.softmax(s, axis=-1)
    return jnp.einsum('bhqk,bhkd->bhqd', p, v.astype(f32))


def _fox_out(o, w_branch, dtype):
    B, T = o.shape[0], o.shape[2]
    y = o.transpose(0, 2, 1, 3).reshape(B, T, FOX_W).astype(dtype)
    return jnp.einsum('btf,fd->btd', y, w_branch)


def _merge_ffn(h, ya, yb, ga, gb, w_out, norm_ffn, w_up, w_down):
    m = jax.nn.sigmoid(ga) * ya + jax.nn.sigmoid(gb) * yb
    h = h + jnp.einsum('btd,de->bte', m, w_out)
    u = jnp.einsum('btd,df->btf', _rmsnorm(h, norm_ffn), w_up)
    return h + jnp.einsum('btf,fd->btd', jnp.square(jax.nn.relu(u)), w_down)


def _layer_prompt(h, norm_mix, w_in, w_gla_gate, b_gla_gate, g_gla_norm, b_fox_forget,
                  w_branch_gla, w_branch_fox, w_out, norm_ffn, w_up, w_down):
    B, L = h.shape[0], h.shape[1]
    n_real = L - N_META
    nc = n_real // CHUNK
    nb = n_real // Q_BLOCK
    hn = _rmsnorm(h, norm_mix)
    gq, gk, gv, gr, glr, fq, fk, fv, ff, ga, gb = _in_proj(hn, w_in)
    q, k, v, log_a = _gla_heads(gq, gk, gv, glr, w_gla_gate, b_gla_gate)
    lead = lambda a: a[:, None, :N_META]
    body = lambda a: a[:, N_META:].reshape((B, nc, CHUNK) + a.shape[2:])
    s0 = jnp.zeros((B, GLA_HEADS, GLA_DK, GLA_DV), jnp.float32)
    o_meta, s_meta = _gla_chunks(lead(q), lead(k), lead(v), lead(log_a), s0)
    o_body, s_last = _gla_chunks(body(q), body(k), body(v), body(log_a), s_meta)
    o = jnp.concatenate([o_meta.reshape(B, N_META, GLA_HEADS, GLA_DV),
                         o_body.reshape(B, n_real, GLA_HEADS, GLA_DV)], axis=1)
    ya = _gla_out(o, gr, g_gla_norm, w_branch_gla, h.dtype)
    fq_h, fk_h, fv_h, logf = _fox_heads(fq, fk, fv, ff, b_fox_forget)
    c = jnp.cumsum(logf, axis=-1)
    pos = jnp.arange(L)
    out_meta = _fox_attend(fq_h[:, :, :N_META], fk_h[:, :, :N_META], fv_h[:, :, :N_META],
                           c[..., :N_META], c[..., :N_META], pos[:N_META], pos[:N_META])
    qb = jnp.moveaxis(fq_h[:, :, N_META:].reshape(B, FOX_HEADS, nb, Q_BLOCK, FOX_DH), 2, 0)
    cb = jnp.moveaxis(c[..., N_META:].reshape(B, FOX_HEADS, nb, Q_BLOCK), 2, 0)
    pb = pos[N_META:].reshape(nb, Q_BLOCK)
    out_body = lax.map(lambda a: _fox_attend(a[0], fk_h, fv_h, a[1], c, a[2], pos), (qb, cb, pb))
    out_body = jnp.moveaxis(out_body, 0, 2).reshape(B, FOX_HEADS, n_real, FOX_DH)
    yb = _fox_out(jnp.concatenate([out_meta, out_body], axis=2), w_branch_fox, h.dtype)
    h = _merge_ffn(h, ya, yb, ga, gb, w_out, norm_ffn, w_up, w_down)
    return h, (fk_h, fv_h, logf, s_last)


def _layer_sample(h, cache_k, cache_v, cache_logf, state, norm_mix, w_in, w_gla_gate, b_gla_gate,
                  g_gla_norm, b_fox_forget, w_branch_gla, w_branch_fox, w_out, norm_ffn, w_up, w_down):
    B, T = h.shape[0], h.shape[1]
    past = cache_k.shape[2]
    hn = _rmsnorm(h, norm_mix)
    gq, gk, gv, gr, glr, fq, fk, fv, ff, ga, gb = _in_proj(hn, w_in)
    q, k, v, log_a = _gla_heads(gq, gk, gv, glr, w_gla_gate, b_gla_gate)
    o, s_new = _gla_chunks(q[:, None], k[:, None], v[:, None], log_a[:, None], state.astype(jnp.float32))
    ya = _gla_out(o.reshape(B, T, GLA_HEADS, GLA_DV), gr, g_gla_norm, w_branch_gla, h.dtype)
    fq_h, fk_h, fv_h, logf = _fox_heads(fq, fk, fv, ff, b_fox_forget)
    k_all = jnp.concatenate([cache_k.astype(fk_h.dtype), fk_h], axis=2)
    v_all = jnp.concatenate([cache_v.astype(fv_h.dtype), fv_h], axis=2)
    c_all = jnp.cumsum(jnp.concatenate([cache_logf.astype(jnp.float32), logf], axis=-1), axis=-1)
    kpos = jnp.arange(past + T)
    fo = _fox_attend(fq_h, k_all, v_all, c_all[..., past:], c_all, kpos[past:], kpos)
    yb = _fox_out(fo, w_branch_fox, h.dtype)
    h = _merge_ffn(h, ya, yb, ga, gb, w_out, norm_ffn, w_up, w_down)
    return h, (fk_h, fv_h, logf, s_new)


def setup_inputs(seed: int = 0) -> dict:
    key = jax.random.key(seed)
    ks = jax.random.split(key, 24)
    f32 = jnp.float32
    nrm = lambda k, shape, scale: jax.random.normal(k, shape, f32) * scale
    return {
        'x_prompt': nrm(ks[0], (BATCH, SEQ, D_MODEL), 1.0),
        'x_sample': nrm(ks[1], (DEC_BATCH, DEC_SEQ, D_MODEL), 1.0),
        'cache_fox_k': nrm(ks[2], (DEPTH, DEC_BATCH, FOX_HEADS, PAST_LEN, FOX_DH), 1.0),
        'cache_fox_v': nrm(ks[3], (DEPTH, DEC_BATCH, FOX_HEADS, PAST_LEN, FOX_DH), 1.0),
        'cache_fox_logf': jax.nn.log_sigmoid(FORGET_BIAS_INIT + nrm(ks[4], (DEPTH, DEC_BATCH, FOX_HEADS, PAST_LEN), 1.0)),
        'state_gla': nrm(ks[5], (DEPTH, DEC_BATCH, GLA_HEADS, GLA_DK, GLA_DV), 1.0),
        'meta_tokens': nrm(ks[6], (N_META, D_MODEL), 1.0),
        'norm_mix': 1.0 + nrm(ks[7], (DEPTH, D_MODEL), 0.05),
        'w_in': nrm(ks[8], (DEPTH, D_MODEL, D_IN), D_MODEL ** -0.5),
        'w_gla_gate': nrm(ks[9], (DEPTH, GLA_RANK, GLA_QK), GLA_RANK ** -0.5),
        'b_gla_gate': nrm(ks[10], (DEPTH, GLA_QK), 0.01),
        'g_gla_norm': 1.0 + nrm(ks[11], (DEPTH, GLA_DV), 0.05),
        'b_fox_forget': FORGET_BIAS_INIT + nrm(ks[12], (DEPTH, FOX_HEADS), 0.1),
        'w_branch_gla': nrm(ks[13], (DEPTH, GLA_V, D_MODEL), GLA_V ** -0.5),
        'w_branch_fox': nrm(ks[14], (DEPTH, FOX_W, D_MODEL), FOX_W ** -0.5),
        'w_out': nrm(ks[15], (DEPTH, D_MODEL, D_MODEL), D_MODEL ** -0.5),
        'norm_ffn': 1.0 + nrm(ks[16], (DEPTH, D_MODEL), 0.05),
        'w_up': nrm(ks[17], (DEPTH, D_MODEL, D_FF), D_MODEL ** -0.5),
        'w_down': nrm(ks[18], (DEPTH, D_FF, D_MODEL), D_FF ** -0.5),
        'norm_final': 1.0 + nrm(ks[19], (D_MODEL,), 0.05),
    }


def reference(x_prompt, x_sample, cache_fox_k, cache_fox_v, cache_fox_logf, state_gla,
              meta_tokens, norm_mix, w_in, w_gla_gate, b_gla_gate, g_gla_norm, b_fox_forget,
              w_branch_gla, w_branch_fox, w_out, norm_ffn, w_up, w_down, norm_final):
    B = x_prompt.shape[0]
    meta = jnp.broadcast_to(meta_tokens.astype(x_prompt.dtype)[None], (B, N_META, D_MODEL))
    h_p = jnp.concatenate([meta, x_prompt], axis=1)
    h_s = x_sample
    st_p, st_s = [], []
    for l in range(DEPTH):
        w = (norm_mix[l], w_in[l], w_gla_gate[l], b_gla_gate[l], g_gla_norm[l], b_fox_forget[l],
             w_branch_gla[l], w_branch_fox[l], w_out[l], norm_ffn[l], w_up[l], w_down[l])
        h_p, sp = _layer_prompt(h_p, *w)
        h_s, ss = _layer_sample(h_s, cache_fox_k[l], cache_fox_v[l], cache_fox_logf[l], state_gla[l], *w)
        st_p.append(sp)
        st_s.append(ss)
    y_prompt = _rmsnorm(h_p[:, N_META:], norm_final)
    y_sample = _rmsnorm(h_s, norm_final)
    new_fox_k_prompt = jnp.stack([s[0] for s in st_p])
    new_fox_v_prompt = jnp.stack([s[1] for s in st_p])
    new_fox_logf_prompt = jnp.stack([s[2] for s in st_p])
    new_gla_state_prompt = jnp.stack([s[3] for s in st_p])
    new_fox_k_sample = jnp.stack([s[0] for s in st_s])
    new_fox_v_sample = jnp.stack([s[1] for s in st_s])
    new_fox_logf_sample = jnp.stack([s[2] for s in st_s])
    new_gla_state_sample = jnp.stack([s[3] for s in st_s])
    return (y_prompt, y_sample, new_fox_k_prompt, new_fox_v_prompt, new_fox_logf_prompt, new_gla_state_prompt,
            new_fox_k_sample, new_fox_v_sample, new_fox_logf_sample, new_gla_state_sample)
```

```python
import functools

import jax
import jax.numpy as jnp
from jax import lax
from jax.experimental import pallas as pl
from jax.experimental.pallas import tpu as pltpu

F32 = jnp.float32
BF16 = jnp.bfloat16

D_MODEL = 1024
N_META = 16
EPS = 1e-6
GLA_HEADS = 4
GLA_DK = 64
GLA_DV = 128
GLA_RANK = 16
GLA_TAU = 16.0
GLA_QK = GLA_HEADS * GLA_DK
GLA_V = GLA_HEADS * GLA_DV
FOX_HEADS = 8
FOX_DH = 64
FOX_W = FOX_HEADS * FOX_DH
D_FF = 4 * D_MODEL
NEG = -1e30
CHUNK = 64

LANES = 128
VMEM_LIMIT = 56 * 1024 * 1024

_C_GQ, _C_GV, _C_GR, _C_FQ, _C_FK, _C_FV, _C_GA, _C_GB, _C_END = (
    0, 256, 768, 1280, 1792, 2304, 2816, 3840, 4864)
_R_K, _R_LR, _R_FF, _R_END = 0, 256, 272, 288


def _dot(a, b):
    return jnp.dot(a, b, preferred_element_type=F32)


def _dot_nt(a, b):
    return lax.dot_general(a, b, (((1,), (1,)), ((), ())), preferred_element_type=F32)


def _split3(x):
    hi = x.astype(BF16)
    r = x - hi.astype(F32)
    mid = r.astype(BF16)
    lo = (r - mid.astype(F32)).astype(BF16)
    return hi, mid, lo


def _dot3(x, m):
    hi, mid, lo = _split3(x)
    return _dot(hi, m) + _dot(mid, m) + _dot(lo, m)


def _log_sigmoid(x):
    return jnp.minimum(x, 0.0) - jnp.log1p(jnp.exp(-jnp.abs(x)))


def _const_spec(shape):
    nd = len(shape)
    return pl.BlockSpec(shape, lambda *_: (0,) * nd, pipeline_mode=pl.Buffered(1))


def _inproj_kernel(x_ref, gn_ref, wnn_ref, wt_ref, wg_ref, bg_ref, bf_ref,
                   qvr_ref, kT_ref, laT_ref, logf_ref, fq_ref, fkb_ref, fvb_ref,
                   fk32_ref, fv32_ref, sg_ref, *, nb, tmb, tw):
    x = x_ref[...]
    ms = jnp.mean(x * x, axis=-1, keepdims=True)
    hn = ((x * lax.rsqrt(ms + EPS)) * gn_ref[...]).astype(BF16)

    qvr_ref[...] = _dot(hn, wnn_ref[:, _C_GQ:_C_FQ]).astype(BF16)

    zf = _dot(hn, wnn_ref[:, _C_FQ:_C_GA])
    fq_ref[...] = zf[:, 0:FOX_W].astype(BF16)
    fkb_ref[...] = zf[:, FOX_W:2 * FOX_W].astype(BF16)
    fvb_ref[...] = zf[:, 2 * FOX_W:3 * FOX_W].astype(BF16)
    for bb in range(nb):
        r0 = bb * tmb
        for h in range(FOX_HEADS):
            c0 = FOX_W + h * FOX_DH
            fk32_ref[bb, h] = zf[r0:r0 + tmb, c0:c0 + FOX_DH]
            fv32_ref[bb, h] = zf[r0:r0 + tmb, c0 + FOX_W:c0 + FOX_W + FOX_DH]

    sg_ref[...] = jax.nn.sigmoid(_dot(hn, wnn_ref[:, _C_GA:_C_END])).astype(BF16)

    zT = _dot_nt(wt_ref[...], hn)
    kT = zT[_R_K:_R_LR].astype(BF16)
    glrT = zT[_R_LR:_R_FF].astype(BF16)
    laT = _log_sigmoid(_dot(wg_ref[...], glrT) + bg_ref[...]) * (1.0 / GLA_TAU)
    logf = _log_sigmoid(zT[_R_FF:_R_FF + FOX_HEADS] + bf_ref[...])
    for bb in range(nb):
        r0 = bb * tmb
        logf_ref[bb] = logf[:, r0:r0 + tmb]
        for j in range(tmb // tw):
            kT_ref[bb, j] = kT[:, r0 + j * tw:r0 + (j + 1) * tw]
            laT_ref[bb, j] = laT[:, r0 + j * tw:r0 + (j + 1) * tw]


def _inproj(x2d, nbatch, n, nb, tmb, tw, w):
    tm = nb * tmb
    ngrp, nt = nbatch // nb, n // tmb
    t_total = nbatch * n
    tok = lambda width: pl.BlockSpec((tm, width), lambda g, i: (g * nt + i, 0))
    out_shape = (
        jax.ShapeDtypeStruct((t_total, _C_FQ), BF16),
        jax.ShapeDtypeStruct((nbatch, n // tw, GLA_QK, tw), BF16),
        jax.ShapeDtypeStruct((nbatch, n // tw, GLA_QK, tw), F32),
        jax.ShapeDtypeStruct((nbatch, FOX_HEADS, n), F32),
        jax.ShapeDtypeStruct((t_total, FOX_W), BF16),
        jax.ShapeDtypeStruct((t_total, FOX_W), BF16),
        jax.ShapeDtypeStruct((t_total, FOX_W), BF16),
        jax.ShapeDtypeStruct((nbatch, FOX_HEADS, n, FOX_DH), F32),
        jax.ShapeDtypeStruct((nbatch, FOX_HEADS, n, FOX_DH), F32),
        jax.ShapeDtypeStruct((t_total, 2 * D_MODEL), BF16),
    )
    tblk = pl.BlockSpec((nb, tmb // tw, GLA_QK, tw), lambda g, i: (g, i, 0, 0))
    hblk = pl.BlockSpec((nb, FOX_HEADS, tmb, FOX_DH), lambda g, i: (g, 0, i, 0))
    out_specs = (
        tok(_C_FQ), tblk, tblk,
        pl.BlockSpec((nb, FOX_HEADS, tmb), lambda g, i: (g, 0, i)),
        tok(FOX_W), tok(FOX_W), tok(FOX_W), hblk, hblk, tok(2 * D_MODEL),
    )
    in_specs = [
        tok(D_MODEL),
        _const_spec((1, D_MODEL)),
        _const_spec((D_MODEL, _C_END)),
        _const_spec((_R_END, D_MODEL)),
        _const_spec((GLA_QK, GLA_RANK)),
        _const_spec((GLA_QK, 1)),
        _const_spec((FOX_HEADS, 1)),
    ]
    return pl.pallas_call(
        functools.partial(_inproj_kernel, nb=nb, tmb=tmb, tw=tw),
        grid=(ngrp, nt), in_specs=in_specs, out_specs=out_specs, out_shape=out_shape,
        compiler_params=pltpu.CompilerParams(
            dimension_semantics=("arbitrary", "arbitrary"), vmem_limit_bytes=VMEM_LIMIT),
        name="inproj",
    )(x2d, w["gn_mix"], w["w_nn"], w["w_t"], w["w_gate_t"], w["b_gate_col"], w["b_f_col"])


def _cumsum_lanes(x, carry):
    n = x.shape[1]
    r = lax.broadcasted_iota(jnp.int32, (LANES, LANES), 0)
    c = lax.broadcasted_iota(jnp.int32, (LANES, LANES), 1)
    upper = jnp.where(r <= c, 1.0, 0.0).astype(BF16)
    outs = []
    for j in range(n // LANES):
        cj = _dot3(x[:, j * LANES:(j + 1) * LANES], upper) + carry
        outs.append(cj)
        carry = cj[:, LANES - 1:LANES]
    return jnp.concatenate(outs, axis=1), carry


def _decay_kernel(lp_ref, ln_ref, cp_ref, cn_ref, *, p_valid):
    lp = lp_ref[...]
    lane = lax.broadcasted_iota(jnp.int32, lp.shape, 1)
    lp = jnp.where(lane < p_valid, lp, 0.0)
    zero = jnp.zeros((lp.shape[0], 1), F32)
    cp, tot = _cumsum_lanes(lp, zero)
    cp_ref[...] = cp
    cn, _ = _cumsum_lanes(ln_ref[...], tot)
    cn_ref[...] = cn


def _decay(logf_p, logf_n, p_valid):
    r, pp = logf_p.shape
    npad = logf_n.shape[1]
    return pl.pallas_call(
        functools.partial(_decay_kernel, p_valid=p_valid),
        out_shape=(jax.ShapeDtypeStruct((r, pp), F32), jax.ShapeDtypeStruct((r, npad), F32)),
        compiler_params=pltpu.CompilerParams(vmem_limit_bytes=VMEM_LIMIT),
        name="decay",
    )(logf_p, logf_n)


def _gla_kernel(qvr_ref, kT_ref, laT_ref, s0_ref, gn_ref, y_ref, sout_ref, s_ref,
                *, n, tw, chunk, need_y):
    nblk = n // tw
    ncb = tw // chunk
    s_ref[...] = s0_ref[0]

    tr = lax.broadcasted_iota(jnp.int32, (tw, tw), 0)
    tc = lax.broadcasted_iota(jnp.int32, (tw, tw), 1)
    same = (tr // chunk) == (tc // chunk)
    m_cum = jnp.where(same & (tr <= tc), 1.0, 0.0).astype(BF16)
    m_tot = jnp.where(same, 1.0, 0.0).astype(BF16)
    sr = lax.broadcasted_iota(jnp.int32, (tw, LANES), 0)
    m_sel = [jnp.where((sr // chunk) == ci, 1.0, 0.0).astype(BF16) for ci in range(ncb)]
    br = lax.broadcasted_iota(jnp.int32, (GLA_QK, GLA_V), 0)
    bc = lax.broadcasted_iota(jnp.int32, (GLA_QK, GLA_V), 1)
    head_mask = (br // GLA_DK) == (bc // GLA_DV)
    gn = gn_ref[...]

    def block(jb, _):
        laT = laT_ref[0, jb]
        hi, mid, lo = _split3(laT)
        d3 = lambda m: _dot(hi, m) + _dot(mid, m) + _dot(lo, m)
        g_cum = d3(m_cum)
        g_tot = d3(m_tot)
        kdec = (kT_ref[0, jb].astype(F32) * jnp.exp(g_tot - g_cum)).astype(BF16)
        for ci in range(ncb):
            row0 = pl.multiple_of(jb * tw, tw) + ci * chunk
            rows = pl.ds(row0, chunk)
            a = jnp.exp(d3(m_sel[ci]))
            a_bd = jnp.concatenate([a] * (GLA_V // LANES), axis=1)
            v = qvr_ref[rows, _C_GV:_C_GR]
            u = _dot(kdec[:, ci * chunk:(ci + 1) * chunk], v)
            s_new = s_ref[...] * a_bd + jnp.where(head_mask, u, 0.0)
            s_ref[...] = s_new
            if need_y:
                q = qvr_ref[rows, _C_GQ:_C_GV]
                o = _dot(q, s_new.astype(BF16))
                gate = jax.nn.silu(qvr_ref[rows, _C_GR:_C_FQ].astype(F32))
                parts = []
                for h in range(GLA_HEADS):
                    oh = o[:, h * GLA_DV:(h + 1) * GLA_DV]
                    msq = jnp.mean(oh * oh, axis=-1, keepdims=True)
                    parts.append(oh * lax.rsqrt(msq + EPS))
                on = jnp.concatenate(parts, axis=1) * gn
                y_ref[rows, :] = (on * gate).astype(BF16)
        return 0

    lax.fori_loop(0, nblk, block, 0)
    sout_ref[0] = s_ref[...]


def _gla(qvr, kT, laT, s0_bd, gn_tiled, nbatch, n, tw, chunk, need_y):
    s0_map = (lambda b: (0, 0, 0)) if s0_bd.shape[0] == 1 else (lambda b: (b, 0, 0))
    kern = functools.partial(_gla_kernel, n=n, tw=tw, chunk=chunk, need_y=need_y)
    if not need_y:
        def kern(qvr_ref, kT_ref, laT_ref, s0_ref, gn_ref, sout_ref, s_ref):
            _gla_kernel(qvr_ref, kT_ref, laT_ref, s0_ref, gn_ref, None, sout_ref, s_ref,
                        n=n, tw=tw, chunk=chunk, need_y=False)
    out_shape = [jax.ShapeDtypeStruct((nbatch, GLA_QK, GLA_V), F32)]
    out_specs = [pl.BlockSpec((1, GLA_QK, GLA_V), lambda b: (b, 0, 0))]
    if need_y:
        out_shape = [jax.ShapeDtypeStruct((nbatch * n, GLA_V), BF16)] + out_shape
        out_specs = [pl.BlockSpec((n, GLA_V), lambda b: (b, 0))] + out_specs
    res = pl.pallas_call(
        kern, grid=(nbatch,),
        in_specs=[
            pl.BlockSpec((n, _C_FQ), lambda b: (b, 0)),
            pl.BlockSpec((1, n // tw, GLA_QK, tw), lambda b: (b, 0, 0, 0)),
            pl.BlockSpec((1, n // tw, GLA_QK, tw), lambda b: (b, 0, 0, 0)),
            pl.BlockSpec((1, GLA_QK, GLA_V), s0_map),
            _const_spec((1, GLA_V)),
        ],
        out_specs=out_specs, out_shape=out_shape,
        scratch_shapes=[pltpu.VMEM((GLA_QK, GLA_V), F32)],
        compiler_params=pltpu.CompilerParams(
            dimension_semantics=("arbitrary",), vmem_limit_bytes=VMEM_LIMIT),
        name="gla",
    )(qvr, kT, laT, s0_bd, gn_tiled)
    return res if need_y else (None, res[0])


def _fox_kernel(q_ref, kn_ref, vn_ref, kp_ref, vp_ref, ccol_ref, crow_ref, cprow_ref, o_ref,
                *, tq, tkp, p_pad, p_valid):
    qi = pl.program_id(2)
    outs = []
    for hh in range(2):
        lo, hi = hh * FOX_DH, (hh + 1) * FOX_DH
        q = q_ref[:, lo:hi]
        cq = ccol_ref[0, 0, :, hh:hh + 1]

        def update(carry, s, v):
            m, l, acc = carry
            m_new = jnp.maximum(m, jnp.max(s, axis=-1, keepdims=True))
            alpha = jnp.exp(m - m_new)
            p = jnp.exp(s - m_new)
            l = alpha * l + jnp.sum(p, axis=-1, keepdims=True)
            acc = alpha * acc + _dot(p.astype(BF16), v)
            return m_new, l, acc

        carry = (jnp.full((tq, 1), NEG, F32), jnp.zeros((tq, 1), F32), jnp.zeros((tq, FOX_DH), F32))

        for jp in range(p_pad // tkp):
            k = kp_ref[0, hh, jp * tkp:(jp + 1) * tkp, :].astype(BF16)
            v = vp_ref[0, hh, jp * tkp:(jp + 1) * tkp, :].astype(BF16)
            s = _dot_nt(q, k) + cq - cprow_ref[0, 0, hh:hh + 1, jp * tkp:(jp + 1) * tkp]
            if (jp + 1) * tkp > p_valid:
                col = lax.broadcasted_iota(jnp.int32, s.shape, 1) + jp * tkp
                s = jnp.where(col < p_valid, s, NEG)
            carry = update(carry, s, v)

        def full_block(j, carry):
            rows = pl.ds(pl.multiple_of(j * tq, tq), tq)
            s = _dot_nt(q, kn_ref[rows, lo:hi]) + cq - crow_ref[0, 0, j, hh:hh + 1, :]
            return update(carry, s, vn_ref[rows, lo:hi])

        carry = lax.fori_loop(0, qi, full_block, carry)

        rows = pl.ds(pl.multiple_of(qi * tq, tq), tq)
        s = _dot_nt(q, kn_ref[rows, lo:hi]) + cq - crow_ref[0, 0, qi, hh:hh + 1, :]
        r = lax.broadcasted_iota(jnp.int32, s.shape, 0)
        c = lax.broadcasted_iota(jnp.int32, s.shape, 1)
        s = jnp.where(c <= r, s, NEG)
        m, l, acc = update(carry, s, vn_ref[rows, lo:hi])
        outs.append(acc / l)
    o_ref[...] = jnp.concatenate(outs, axis=1).astype(BF16)


def _fox(q, kn, vn, kp, vp, c_col, c_row, cp_row, nbatch, n, tq, tkp, p_valid):
    nq = n // tq
    p_pad = kp.shape[2]
    pmap4 = (lambda b, hp, qi: (0, hp, 0, 0)) if kp.shape[0] == 1 else (lambda b, hp, qi: (b, hp, 0, 0))
    cmap4 = (lambda b, hp, qi: (0, hp, 0, 0)) if cp_row.shape[0] == 1 else (lambda b, hp, qi: (b, hp, 0, 0))
    return pl.pallas_call(
        functools.partial(_fox_kernel, tq=tq, tkp=tkp, p_pad=p_pad, p_valid=p_valid),
        grid=(nbatch, FOX_HEADS // 2, nq),
        in_specs=[
            pl.BlockSpec((tq, LANES), lambda b, hp, qi: (b * nq + qi, hp)),
            pl.BlockSpec((n, LANES), lambda b, hp, qi: (b, hp)),
            pl.BlockSpec((n, LANES), lambda b, hp, qi: (b, hp)),
            pl.BlockSpec((1, 2, p_pad, FOX_DH), pmap4),
            pl.BlockSpec((1, 2, p_pad, FOX_DH), pmap4),
            pl.BlockSpec((1, 1, tq, 2), lambda b, hp, qi: (b, hp, qi, 0)),
            pl.BlockSpec((1, 1, nq, 2, tq), lambda b, hp, qi: (b, hp, 0, 0, 0)),
            pl.BlockSpec((1, 1, 2, p_pad), cmap4),
        ],
        out_specs=pl.BlockSpec((tq, LANES), lambda b, hp, qi: (b * nq + qi, hp)),
        out_shape=jax.ShapeDtypeStruct((nbatch * n, FOX_W), BF16),
        compiler_params=pltpu.CompilerParams(
            dimension_semantics=("arbitrary", "arbitrary", "arbitrary"), vmem_limit_bytes=VMEM_LIMIT),
        name="fox",
    )(q, kn, vn, kp, vp, c_col, c_row, cp_row)


def _merge_ffn_kernel(x_ref, ygla_ref, ofox_ref, sg_ref, wbg_ref, wbf_ref, wout_ref, gffn_ref,
                      wup_ref, wdown_ref, gfin_ref, y_ref, *, ff_chunk):
    ya = _dot(ygla_ref[...], wbg_ref[...])
    yb = _dot(ofox_ref[...], wbf_ref[...])
    m = sg_ref[:, 0:D_MODEL].astype(F32) * ya + sg_ref[:, D_MODEL:2 * D_MODEL].astype(F32) * yb
    h = x_ref[...] + _dot(m.astype(BF16), wout_ref[...])
    ms = jnp.mean(h * h, axis=-1, keepdims=True)
    hn = ((h * lax.rsqrt(ms + EPS)) * gffn_ref[...]).astype(BF16)
    for c in range(D_FF // ff_chunk):
        u = _dot(hn, wup_ref[:, c * ff_chunk:(c + 1) * ff_chunk])
        r = jnp.square(jnp.maximum(u, 0.0)).astype(BF16)
        h = h + _dot(r, wdown_ref[c * ff_chunk:(c + 1) * ff_chunk, :])
    ms = jnp.mean(h * h, axis=-1, keepdims=True)
    y_ref[...] = (h * lax.rsqrt(ms + EPS)) * gfin_ref[...]


def _merge_ffn(x2d, ygla, ofox, sg, w, tm):
    t_total = x2d.shape[0]
    tok = lambda width: pl.BlockSpec((tm, width), lambda i: (i, 0))
    return pl.pallas_call(
        functools.partial(_merge_ffn_kernel, ff_chunk=1024),
        grid=(t_total // tm,),
        in_specs=[
            tok(D_MODEL), tok(GLA_V), tok(FOX_W), tok(2 * D_MODEL),
            _const_spec((GLA_V, D_MODEL)), _const_spec((FOX_W, D_MODEL)),
            _const_spec((D_MODEL, D_MODEL)), _const_spec((1, D_MODEL)),
            _const_spec((D_MODEL, D_FF)), _const_spec((D_FF, D_MODEL)), _const_spec((1, D_MODEL)),
        ],
        out_specs=tok(D_MODEL),
        out_shape=jax.ShapeDtypeStruct((t_total, D_MODEL), F32),
        compiler_params=pltpu.CompilerParams(
            dimension_semantics=("arbitrary",), vmem_limit_bytes=VMEM_LIMIT),
        name="merge_ffn",
    )(x2d, ygla, ofox, sg, w["w_bg"], w["w_bf"], w["w_out"], w["gn_ffn"], w["w_up"], w["w_down"],
      w["gn_final"])


def _prep_weights(norm_mix, w_in, w_gla_gate, b_gla_gate, g_gla_norm, b_fox_forget, w_branch_gla,
                  w_branch_fox, w_out, norm_ffn, w_up, w_down, norm_final):
    sizes = (GLA_QK, GLA_QK, GLA_V, GLA_V, GLA_RANK, FOX_W, FOX_W, FOX_W, FOX_HEADS, D_MODEL, D_MODEL)
    offs = [0]
    for s in sizes:
        offs.append(offs[-1] + s)
    col = lambda i: w_in[:, offs[i]:offs[i + 1]]
    gq, gk, gv, gr, glr, fq, fk, fv, ff, ga, gb = (col(i) for i in range(11))
    w_nn = jnp.concatenate([gq * (GLA_DK ** -0.5), gv, gr, fq * (FOX_DH ** -0.5), fk, fv, ga, gb],
                           axis=1).astype(BF16)
    pad = jnp.zeros((D_MODEL, _R_END - _R_FF - FOX_HEADS), F32)
    w_t = jnp.concatenate([gk, glr, ff, pad], axis=1).T.astype(BF16)
    return dict(
        gn_mix=norm_mix.reshape(1, D_MODEL), w_nn=w_nn, w_t=w_t,
        w_gate_t=w_gla_gate.T.astype(BF16), b_gate_col=b_gla_gate.reshape(GLA_QK, 1),
        b_f_col=b_fox_forget.reshape(FOX_HEADS, 1),
        gn_gla=jnp.tile(g_gla_norm.reshape(1, GLA_DV), (1, GLA_HEADS)),
        w_bg=w_branch_gla.astype(BF16), w_bf=w_branch_fox.astype(BF16), w_out=w_out.astype(BF16),
        gn_ffn=norm_ffn.reshape(1, D_MODEL), w_up=w_up.astype(BF16), w_down=w_down.astype(BF16),
        gn_final=norm_final.reshape(1, D_MODEL),
    )


def _state_to_bd(s):
    nbt = s.shape[0]
    eye = jnp.eye(GLA_HEADS, dtype=s.dtype)
    return jnp.einsum('bhkv,hg->bhkgv', s, eye).reshape(nbt, GLA_QK, GLA_V)


def _bd_to_state(s_bd):
    nbt = s_bd.shape[0]
    s5 = s_bd.reshape(nbt, GLA_HEADS, GLA_DK, GLA_HEADS, GLA_DV)
    return jnp.stack([s5[:, h, :, h, :] for h in range(GLA_HEADS)], axis=1)


def _pad_lanes(x):
    n = x.shape[-1]
    npad = -(-n // LANES) * LANES
    return x if npad == n else jnp.pad(x, [(0, 0)] * (x.ndim - 1) + [(0, npad - n)])


def _c_layouts(c, nbatch, n, tq):
    c4 = c.reshape(nbatch, FOX_HEADS // 2, 2, n)
    c_row = jnp.swapaxes(c4.reshape(nbatch, FOX_HEADS // 2, 2, n // tq, tq), 2, 3)
    return c_row, jnp.swapaxes(c4, 2, 3)


def kernel(x_prompt, x_sample, cache_fox_k, cache_fox_v, cache_fox_logf, state_gla, meta_tokens,
           norm_mix, w_in, w_gla_gate, b_gla_gate, g_gla_norm, b_fox_forget, w_branch_gla,
           w_branch_fox, w_out, norm_ffn, w_up, w_down, norm_final):
    bsz, seq, _ = x_prompt.shape
    dbsz, dseq, _ = x_sample.shape
    past = cache_fox_k.shape[3]
    w = _prep_weights(norm_mix[0], w_in[0], w_gla_gate[0], b_gla_gate[0], g_gla_norm[0],
                      b_fox_forget[0], w_branch_gla[0], w_branch_fox[0], w_out[0], norm_ffn[0],
                      w_up[0], w_down[0], norm_final)
    xp = x_prompt.reshape(bsz * seq, D_MODEL)
    xs = x_sample.reshape(dbsz * dseq, D_MODEL)

    (m_qvr, m_kT, m_laT, m_logf, _, _, _, m_k32, m_v32, _) = _inproj(
        meta_tokens.astype(F32), 1, N_META, 1, N_META, N_META, w)
    zero_state = jnp.zeros((1, GLA_QK, GLA_V), F32)
    _, s_meta_bd = _gla(m_qvr, m_kT, m_laT, zero_state, w["gn_gla"], 1, N_META, N_META, N_META, False)

    tm = 512
    (p_qvr, p_kT, p_laT, p_logf, p_fq, p_fkb, p_fvb, p_k32, p_v32, p_sg) = _inproj(
        xp, bsz, seq, 1, tm, LANES, w)
    y_gla, s_p_bd = _gla(p_qvr, p_kT, p_laT, s_meta_bd, w["gn_gla"], bsz, seq, LANES, CHUNK, True)
    lp = _pad_lanes(jnp.broadcast_to(m_logf, (bsz, FOX_HEADS, N_META)).reshape(bsz * FOX_HEADS, N_META))
    cp, cn = _decay(lp, p_logf.reshape(bsz * FOX_HEADS, seq), N_META)
    tq = 256
    c_row, c_col = _c_layouts(cn, bsz, seq, tq)
    cp_row = cp[:FOX_HEADS].reshape(1, FOX_HEADS // 2, 2, LANES)
    kp = jnp.pad(m_k32, ((0, 0), (0, 0), (0, LANES - N_META), (0, 0)))
    vp = jnp.pad(m_v32, ((0, 0), (0, 0), (0, LANES - N_META), (0, 0)))
    o_fox = _fox(p_fq, p_fkb, p_fvb, kp, vp, c_col, c_row, cp_row, bsz, seq, tq, LANES, N_META)
    y_prompt = _merge_ffn(xp, y_gla, o_fox, p_sg, w, tm).reshape(bsz, seq, D_MODEL)

    (s_qvr, s_kT, s_laT, s_logf, s_fq, s_fkb, s_fvb, s_k32, s_v32, s_sg) = _inproj(
        xs, dbsz, dseq, dbsz, dseq, dseq, w)
    ys_gla, s_s_bd = _gla(s_qvr, s_kT, s_laT, _state_to_bd(state_gla[0].astype(F32)), w["gn_gla"],
                          dbsz, dseq, dseq, dseq, True)
    cps, cns = _decay(cache_fox_logf[0].astype(F32).reshape(dbsz * FOX_HEADS, past),
                      _pad_lanes(s_logf.reshape(dbsz * FOX_HEADS, dseq)), past)
    cs_row, cs_col = _c_layouts(cns[:, :dseq], dbsz, dseq, dseq)
    cps_row = cps.reshape(dbsz, FOX_HEADS // 2, 2, past)
    os_fox = _fox(s_fq, s_fkb, s_fvb, cache_fox_k[0], cache_fox_v[0], cs_col, cs_row, cps_row,
                  dbsz, dseq, dseq, 256, past)
    y_sample = _merge_ffn(xs, ys_gla, os_fox, s_sg, w, dbsz * dseq).reshape(dbsz, dseq, D_MODEL)

    bc = lambda a: jnp.broadcast_to(a, (bsz,) + a.shape[1:])
    new_fox_k_prompt = jnp.concatenate([bc(m_k32), p_k32], axis=2)[None]
    new_fox_v_prompt = jnp.concatenate([bc(m_v32), p_v32], axis=2)[None]
    new_fox_logf_prompt = jnp.concatenate([bc(m_logf), p_logf], axis=2)[None]
    new_gla_state_prompt = _bd_to_state(s_p_bd)[None]
    return (y_prompt, y_sample, new_fox_k_prompt, new_fox_v_prompt, new_fox_logf_prompt,
            new_gla_state_prompt, s_k32[None], s_v32[None], s_logf[None], _bd_to_state(s_s_bd)[None])
```

```python
import functools

import jax
import jax.numpy as jnp
from jax import lax
from jax.experimental import pallas as pl
from jax.experimental.pallas import tpu as pltpu

F32 = jnp.float32
BF16 = jnp.bfloat16

D_MODEL = 1024
N_META = 16
EPS = 1e-6
GLA_HEADS = 4
GLA_DK = 64
GLA_DV = 128
GLA_RANK = 16
GLA_TAU = 16.0
GLA_QK = GLA_HEADS * GLA_DK
GLA_V = GLA_HEADS * GLA_DV
FOX_HEADS = 8
FOX_DH = 64
FOX_W = FOX_HEADS * FOX_DH
D_FF = 4 * D_MODEL
NEG = -1e30
CHUNK = 64
LOG2E = 1.4426950408889634

LANES = 128
VMEM_LIMIT = 56 * 1024 * 1024

_C_GQ, _C_GV, _C_GR, _C_FQ, _C_FK, _C_FV, _C_GA, _C_GB, _C_END = (
    0, 256, 768, 1280, 1792, 2304, 2816, 3840, 4864)
_R_K, _R_LR, _R_FF, _R_END = 0, 256, 272, 288


def _dot(a, b):
    return jnp.dot(a, b, preferred_element_type=F32)


def _dot_nt(a, b):
    return lax.dot_general(a, b, (((1,), (1,)), ((), ())), preferred_element_type=F32)


def _split3(x):
    hi = x.astype(BF16)
    r = x - hi.astype(F32)
    mid = r.astype(BF16)
    lo = (r - mid.astype(F32)).astype(BF16)
    return hi, mid, lo


def _dot3(x, m):
    hi, mid, lo = _split3(x)
    return _dot(hi, m) + _dot(mid, m) + _dot(lo, m)


def _log_sigmoid(x):
    return jnp.minimum(x, 0.0) - jnp.log1p(jnp.exp(-jnp.abs(x)))


def _const_spec(shape):
    nd = len(shape)
    return pl.BlockSpec(shape, lambda *_: (0,) * nd, pipeline_mode=pl.Buffered(1))


def _inproj_kernel(x_ref, gn_ref, wnn_ref, wt_ref, wg_ref, bg_ref, bf_ref,
                   qvr_ref, kT_ref, laT_ref, logf_ref, fq_ref, fkb_ref, fvb_ref,
                   fk32_ref, fv32_ref, sg_ref, *, nb, tmb, tw):
    x = x_ref[...]
    ms = jnp.mean(x * x, axis=-1, keepdims=True)
    hn = ((x * lax.rsqrt(ms + EPS)) * gn_ref[...]).astype(BF16)

    qvr_ref[...] = _dot(hn, wnn_ref[:, _C_GQ:_C_FQ]).astype(BF16)

    zf = _dot(hn, wnn_ref[:, _C_FQ:_C_GA])
    fq_ref[...] = zf[:, 0:FOX_W].astype(BF16)
    fkb_ref[...] = zf[:, FOX_W:2 * FOX_W].astype(BF16)
    fvb_ref[...] = zf[:, 2 * FOX_W:3 * FOX_W].astype(BF16)
    for bb in range(nb):
        r0 = bb * tmb
        for h in range(FOX_HEADS):
            c0 = FOX_W + h * FOX_DH
            fk32_ref[bb, h] = zf[r0:r0 + tmb, c0:c0 + FOX_DH]
            fv32_ref[bb, h] = zf[r0:r0 + tmb, c0 + FOX_W:c0 + FOX_W + FOX_DH]

    sg_ref[...] = jax.nn.sigmoid(_dot(hn, wnn_ref[:, _C_GA:_C_END])).astype(BF16)

    zT = _dot_nt(wt_ref[...], hn)
    kT = zT[_R_K:_R_LR].astype(BF16)
    glrT = zT[_R_LR:_R_FF].astype(BF16)
    laT = _log_sigmoid(_dot(wg_ref[...], glrT) + bg_ref[...]) * (1.0 / GLA_TAU)
    logf = _log_sigmoid(zT[_R_FF:_R_FF + FOX_HEADS] + bf_ref[...])
    for bb in range(nb):
        r0 = bb * tmb
        logf_ref[bb] = logf[:, r0:r0 + tmb]
        for j in range(tmb // tw):
            kT_ref[bb, j] = kT[:, r0 + j * tw:r0 + (j + 1) * tw]
            laT_ref[bb, j] = laT[:, r0 + j * tw:r0 + (j + 1) * tw]


def _inproj(x2d, nbatch, n, nb, tmb, tw, w):
    tm = nb * tmb
    ngrp, nt = nbatch // nb, n // tmb
    t_total = nbatch * n
    tok = lambda width: pl.BlockSpec((tm, width), lambda g, i: (g * nt + i, 0))
    out_shape = (
        jax.ShapeDtypeStruct((t_total, _C_FQ), BF16),
        jax.ShapeDtypeStruct((nbatch, n // tw, GLA_QK, tw), BF16),
        jax.ShapeDtypeStruct((nbatch, n // tw, GLA_QK, tw), F32),
        jax.ShapeDtypeStruct((nbatch, FOX_HEADS, n), F32),
        jax.ShapeDtypeStruct((t_total, FOX_W), BF16),
        jax.ShapeDtypeStruct((t_total, FOX_W), BF16),
        jax.ShapeDtypeStruct((t_total, FOX_W), BF16),
        jax.ShapeDtypeStruct((nbatch, FOX_HEADS, n, FOX_DH), F32),
        jax.ShapeDtypeStruct((nbatch, FOX_HEADS, n, FOX_DH), F32),
        jax.ShapeDtypeStruct((t_total, 2 * D_MODEL), BF16),
    )
    tblk = pl.BlockSpec((nb, tmb // tw, GLA_QK, tw), lambda g, i: (g, i, 0, 0))
    hblk = pl.BlockSpec((nb, FOX_HEADS, tmb, FOX_DH), lambda g, i: (g, 0, i, 0))
    out_specs = (
        tok(_C_FQ), tblk, tblk,
        pl.BlockSpec((nb, FOX_HEADS, tmb), lambda g, i: (g, 0, i)),
        tok(FOX_W), tok(FOX_W), tok(FOX_W), hblk, hblk, tok(2 * D_MODEL),
    )
    in_specs = [
        tok(D_MODEL),
        _const_spec((1, D_MODEL)),
        _const_spec((D_MODEL, _C_END)),
        _const_spec((_R_END, D_MODEL)),
        _const_spec((GLA_QK, GLA_RANK)),
        _const_spec((GLA_QK, 1)),
        _const_spec((FOX_HEADS, 1)),
    ]
    return pl.pallas_call(
        functools.partial(_inproj_kernel, nb=nb, tmb=tmb, tw=tw),
        grid=(ngrp, nt), in_specs=in_specs, out_specs=out_specs, out_shape=out_shape,
        compiler_params=pltpu.CompilerParams(
            dimension_semantics=("arbitrary", "arbitrary"), vmem_limit_bytes=VMEM_LIMIT),
        name="inproj",
    )(x2d, w["gn_mix"], w["w_nn"], w["w_t"], w["w_gate_t"], w["b_gate_col"], w["b_f_col"])


def _cumsum_lanes(x, carry):
    n = x.shape[1]
    r = lax.broadcasted_iota(jnp.int32, (LANES, LANES), 0)
    c = lax.broadcasted_iota(jnp.int32, (LANES, LANES), 1)
    upper = jnp.where(r <= c, 1.0, 0.0).astype(BF16)
    outs = []
    for j in range(n // LANES):
        cj = _dot3(x[:, j * LANES:(j + 1) * LANES], upper) + carry
        outs.append(cj)
        carry = cj[:, LANES - 1:LANES]
    return jnp.concatenate(outs, axis=1), carry


def _decay_kernel(lp_ref, ln_ref, cp_ref, cn_ref, *, p_valid):
    lp = lp_ref[...]
    lane = lax.broadcasted_iota(jnp.int32, lp.shape, 1)
    lp = jnp.where(lane < p_valid, lp, 0.0)
    zero = jnp.zeros((lp.shape[0], 1), F32)
    cp, tot = _cumsum_lanes(lp, zero)
    cp_ref[...] = cp * LOG2E
    cn, _ = _cumsum_lanes(ln_ref[...], tot)
    cn_ref[...] = cn * LOG2E


def _decay(logf_p, logf_n, p_valid):
    r, pp = logf_p.shape
    npad = logf_n.shape[1]
    return pl.pallas_call(
        functools.partial(_decay_kernel, p_valid=p_valid),
        out_shape=(jax.ShapeDtypeStruct((r, pp), F32), jax.ShapeDtypeStruct((r, npad), F32)),
        compiler_params=pltpu.CompilerParams(vmem_limit_bytes=VMEM_LIMIT),
        name="decay",
    )(logf_p, logf_n)


def _gla_kernel(qvr_ref, kT_ref, laT_ref, s0_ref, gn_ref, y_ref, sout_ref, s_ref,
                *, n, tw, chunk, need_y):
    nblk = n // tw
    ncb = tw // chunk
    s_ref[...] = s0_ref[0]

    tr = lax.broadcasted_iota(jnp.int32, (tw, tw), 0)
    tc = lax.broadcasted_iota(jnp.int32, (tw, tw), 1)
    same = (tr // chunk) == (tc // chunk)
    m_cum = jnp.where(same & (tr <= tc), 1.0, 0.0).astype(BF16)
    m_tot = jnp.where(same, 1.0, 0.0).astype(BF16)
    sr = lax.broadcasted_iota(jnp.int32, (tw, LANES), 0)
    m_sel = [jnp.where((sr // chunk) == ci, 1.0, 0.0).astype(BF16) for ci in range(ncb)]
    br = lax.broadcasted_iota(jnp.int32, (GLA_QK, GLA_V), 0)
    bc = lax.broadcasted_iota(jnp.int32, (GLA_QK, GLA_V), 1)
    head_mask = (br // GLA_DK) == (bc // GLA_DV)
    gn = gn_ref[...]

    def block(jb, _):
        laT = laT_ref[0, jb]
        hi, mid, lo = _split3(laT)
        d3 = lambda m: _dot(hi, m) + _dot(mid, m) + _dot(lo, m)
        g_cum = d3(m_cum)
        g_tot = d3(m_tot)
        kdec = (kT_ref[0, jb].astype(F32) * jnp.exp(g_tot - g_cum)).astype(BF16)
        for ci in range(ncb):
            row0 = pl.multiple_of(jb * tw, tw) + ci * chunk
            rows = pl.ds(row0, chunk)
            a = jnp.exp(d3(m_sel[ci]))
            a_bd = jnp.concatenate([a] * (GLA_V // LANES), axis=1)
            v = qvr_ref[rows, _C_GV:_C_GR]
            u = _dot(kdec[:, ci * chunk:(ci + 1) * chunk], v)
            s_new = s_ref[...] * a_bd + jnp.where(head_mask, u, 0.0)
            s_ref[...] = s_new
            if need_y:
                q = qvr_ref[rows, _C_GQ:_C_GV]
                o = _dot(q, s_new.astype(BF16))
                gate = jax.nn.silu(qvr_ref[rows, _C_GR:_C_FQ].astype(F32))
                parts = []
                for h in range(GLA_HEADS):
                    oh = o[:, h * GLA_DV:(h + 1) * GLA_DV]
                    msq = jnp.mean(oh * oh, axis=-1, keepdims=True)
                    parts.append(oh * lax.rsqrt(msq + EPS))
                on = jnp.concatenate(parts, axis=1) * gn
                y_ref[rows, :] = (on * gate).astype(BF16)
        return 0

    lax.fori_loop(0, nblk, block, 0)
    sout_ref[0] = s_ref[...]


def _gla(qvr, kT, laT, s0_bd, gn_tiled, nbatch, n, tw, chunk, need_y):
    s0_map = (lambda b: (0, 0, 0)) if s0_bd.shape[0] == 1 else (lambda b: (b, 0, 0))
    kern = functools.partial(_gla_kernel, n=n, tw=tw, chunk=chunk, need_y=need_y)
    if not need_y:
        def kern(qvr_ref, kT_ref, laT_ref, s0_ref, gn_ref, sout_ref, s_ref):
            _gla_kernel(qvr_ref, kT_ref, laT_ref, s0_ref, gn_ref, None, sout_ref, s_ref,
                        n=n, tw=tw, chunk=chunk, need_y=False)
    out_shape = [jax.ShapeDtypeStruct((nbatch, GLA_QK, GLA_V), F32)]
    out_specs = [pl.BlockSpec((1, GLA_QK, GLA_V), lambda b: (b, 0, 0))]
    if need_y:
        out_shape = [jax.ShapeDtypeStruct((nbatch * n, GLA_V), BF16)] + out_shape
        out_specs = [pl.BlockSpec((n, GLA_V), lambda b: (b, 0))] + out_specs
    res = pl.pallas_call(
        kern, grid=(nbatch,),
        in_specs=[
            pl.BlockSpec((n, _C_FQ), lambda b: (b, 0)),
            pl.BlockSpec((1, n // tw, GLA_QK, tw), lambda b: (b, 0, 0, 0)),
            pl.BlockSpec((1, n // tw, GLA_QK, tw), lambda b: (b, 0, 0, 0)),
            pl.BlockSpec((1, GLA_QK, GLA_V), s0_map),
            _const_spec((1, GLA_V)),
        ],
        out_specs=out_specs, out_shape=out_shape,
        scratch_shapes=[pltpu.VMEM((GLA_QK, GLA_V), F32)],
        compiler_params=pltpu.CompilerParams(
            dimension_semantics=("arbitrary",), vmem_limit_bytes=VMEM_LIMIT),
        name="gla",
    )(qvr, kT, laT, s0_bd, gn_tiled)
    return res if need_y else (None, res[0])


def _fold_lanes(x, op):
    w = x.shape[1]
    if w % LANES == 0:
        f = x[:, 0:LANES]
        for g in range(1, w // LANES):
            f = op(f, x[:, g * LANES:(g + 1) * LANES])
        return f
    assert op is jnp.add
    lane = lax.broadcasted_iota(jnp.int32, (x.shape[0], LANES), 1)
    return jnp.where(lane == 0, jnp.sum(x, axis=-1, keepdims=True), 0.0)


def _tile_lanes(x, w):
    return x[:, 0:w] if w < LANES else jnp.concatenate([x] * (w // LANES), axis=1)


def _fox_kernel(q_ref, kn_ref, vn_ref, kp_ref, vp_ref, ccol_ref, crow_ref, cprow_ref, o_ref,
                s_scr, p_scr, vt_scr, sh_scr, l_scr, *, n, tq, tkp, p_pad, p_valid, pv_t):
    nq = n // tq
    rb = min(tq, 64)
    nbuf = s_scr.shape[0] // 2

    class RowReduce:
        def __init__(self, op, red):
            self.op, self.red, self.lane, self.col = op, red, None, None

        def add(self, x):
            if x.shape[1] % LANES == 0:
                f = x[:, 0:LANES]
                for g in range(1, x.shape[1] // LANES):
                    f = self.op(f, x[:, g * LANES:(g + 1) * LANES])
                self.lane = f if self.lane is None else self.op(self.lane, f)
            else:
                r = self.red(x, axis=-1, keepdims=True)
                self.col = r if self.col is None else self.op(self.col, r)

        def result(self):
            out = None if self.lane is None else self.red(self.lane, axis=-1, keepdims=True)
            if self.col is not None:
                out = self.col if out is None else self.op(out, self.col)
            return out

    if pv_t:
        vt_scr[...] = vn_ref[...].astype(F32).T.astype(BF16)
    vp = [vp_ref[0, hh].astype(BF16) for hh in range(2)]
    for qi in range(nq):
        rows = slice(qi * tq, (qi + 1) * tq)
        w_new = (qi + 1) * tq
        chunks = [(c0, tkp) for c0 in range(0, p_pad, tkp)] + [(p_pad + j * tq, tq) for j in range(qi + 1)]
        sbs = [(qi % nbuf) * 2 + hh for hh in range(2)]

        for hh in range(2):
            lo, hi = hh * FOX_DH, (hh + 1) * FOX_DH
            sb = sbs[hh]
            q = q_ref[rows, lo:hi]
            cq = ccol_ref[0, 0, rows, hh:hh + 1]
            mx = RowReduce(jnp.maximum, jnp.max)
            for jp in range(p_pad // tkp):
                cols = slice(jp * tkp, (jp + 1) * tkp)
                k = kp_ref[0, hh, cols, :].astype(BF16)
                s = _dot_nt(q, k) - cprow_ref[0, 0, hh:hh + 1, cols]
                if (jp + 1) * tkp > p_valid:
                    col = lax.broadcasted_iota(jnp.int32, s.shape, 1) + jp * tkp
                    s = jnp.where(col < p_valid, s, NEG)
                s_scr[sb, :, cols] = s
                mx.add(s)
            for j in range(qi + 1):
                krows = slice(j * tq, (j + 1) * tq)
                s = _dot_nt(q, kn_ref[krows, lo:hi]) - crow_ref[0, 0, j, hh:hh + 1, :]
                if j == qi:
                    r = lax.broadcasted_iota(jnp.int32, s.shape, 0)
                    c = lax.broadcasted_iota(jnp.int32, s.shape, 1)
                    s = jnp.where(c <= r, s, NEG)
                s_scr[sb, :, p_pad + j * tq:p_pad + (j + 1) * tq] = s
                mx.add(s)
            m = mx.result() + cq
            sh_scr[hh] = jnp.broadcast_to(m - cq, (tq, LANES))

        def row_group(g, carry):
            rr = pl.ds(pl.multiple_of(g * rb, rb), rb)
            for hh in range(2):
                sh = sh_scr[hh, rr, :]
                lsum = None
                for c0, cw in chunks:
                    p = jnp.exp2(s_scr[sbs[hh], rr, c0:c0 + cw] - _tile_lanes(sh, cw))
                    p_scr[sbs[hh], rr, c0:c0 + cw] = p.astype(BF16)
                    f = _fold_lanes(p, jnp.add)
                    lsum = f if lsum is None else lsum + f
                l_scr[hh, rr, :] = lsum
            return carry

        lax.fori_loop(0, tq // rb, row_group, 0, unroll=True)

        accs = []
        for hh in range(2):
            lo, hi = hh * FOX_DH, (hh + 1) * FOX_DH
            p_pre, p_new = p_scr[sbs[hh], :, 0:p_pad], p_scr[sbs[hh], :, p_pad:p_pad + w_new]
            if pv_t:
                accs.append(_dot_nt(vp[hh], p_pre) + _dot_nt(vt_scr[lo:hi, 0:w_new], p_new))
            else:
                accs.append(_dot(p_pre, vp[hh]) + _dot(p_new, vn_ref[0:w_new, lo:hi]))
        acc = jnp.concatenate(accs, axis=0).T if pv_t else jnp.concatenate(accs, axis=1)
        lane = lax.broadcasted_iota(jnp.int32, acc.shape, 1)
        l0, l1 = (jnp.sum(l_scr[hh], axis=-1, keepdims=True) for hh in range(2))
        o_ref[rows, :] = (acc * jnp.where(lane < FOX_DH, 1.0 / l0, 1.0 / l1)).astype(BF16)


def _fox(q, kn, vn, kp, vp, c_col, c_row, cp_row, nbatch, n, tq, tkp, p_valid, pv_t):
    nq = n // tq
    p_pad = kp.shape[2]
    pmap4 = (lambda b, hp: (0, hp, 0, 0)) if kp.shape[0] == 1 else (lambda b, hp: (b, hp, 0, 0))
    cmap4 = (lambda b, hp: (0, hp, 0, 0)) if cp_row.shape[0] == 1 else (lambda b, hp: (b, hp, 0, 0))
    tok = pl.BlockSpec((n, LANES), lambda b, hp: (b, hp))
    nbuf = min(nq, 2)
    return pl.pallas_call(
        functools.partial(_fox_kernel, n=n, tq=tq, tkp=tkp, p_pad=p_pad, p_valid=p_valid, pv_t=pv_t),
        grid=(nbatch, FOX_HEADS // 2),
        in_specs=[
            tok, tok, tok,
            pl.BlockSpec((1, 2, p_pad, FOX_DH), pmap4),
            pl.BlockSpec((1, 2) + vp.shape[2:], pmap4),
            pl.BlockSpec((1, 1, n, 2), lambda b, hp: (b, hp, 0, 0)),
            pl.BlockSpec((1, 1, nq, 2, tq), lambda b, hp: (b, hp, 0, 0, 0)),
            pl.BlockSpec((1, 1, 2, p_pad), cmap4),
        ],
        out_specs=tok,
        out_shape=jax.ShapeDtypeStruct((nbatch * n, FOX_W), BF16),
        scratch_shapes=[pltpu.VMEM((2 * nbuf, tq, p_pad + n), F32),
                        pltpu.VMEM((2 * nbuf, tq, p_pad + n), BF16),
                        pltpu.VMEM((LANES, n), BF16),
                        pltpu.VMEM((2, tq, LANES), F32), pltpu.VMEM((2, tq, LANES), F32)],
        compiler_params=pltpu.CompilerParams(
            dimension_semantics=("arbitrary", "arbitrary"), vmem_limit_bytes=VMEM_LIMIT),
        name="fox",
    )(q, kn, vn, kp, vp, c_col, c_row, cp_row)


def _merge_ffn_kernel(x_ref, ygla_ref, ofox_ref, sg_ref, wbg_ref, wbf_ref, wout_ref, gffn_ref,
                      wup_ref, wdown_ref, gfin_ref, y_ref, *, ff_chunk):
    ya = _dot(ygla_ref[...], wbg_ref[...])
    yb = _dot(ofox_ref[...], wbf_ref[...])
    m = sg_ref[:, 0:D_MODEL].astype(F32) * ya + sg_ref[:, D_MODEL:2 * D_MODEL].astype(F32) * yb
    h = x_ref[...] + _dot(m.astype(BF16), wout_ref[...])
    ms = jnp.mean(h * h, axis=-1, keepdims=True)
    hn = ((h * lax.rsqrt(ms + EPS)) * gffn_ref[...]).astype(BF16)
    for c in range(D_FF // ff_chunk):
        u = _dot(hn, wup_ref[:, c * ff_chunk:(c + 1) * ff_chunk])
        r = jnp.square(jnp.maximum(u, 0.0)).astype(BF16)
        h = h + _dot(r, wdown_ref[c * ff_chunk:(c + 1) * ff_chunk, :])
    ms = jnp.mean(h * h, axis=-1, keepdims=True)
    y_ref[...] = (h * lax.rsqrt(ms + EPS)) * gfin_ref[...]


def _merge_ffn(x2d, ygla, ofox, sg, w, tm):
    t_total = x2d.shape[0]
    tok = lambda width: pl.BlockSpec((tm, width), lambda i: (i, 0))
    return pl.pallas_call(
        functools.partial(_merge_ffn_kernel, ff_chunk=1024),
        grid=(t_total // tm,),
        in_specs=[
            tok(D_MODEL), tok(GLA_V), tok(FOX_W), tok(2 * D_MODEL),
            _const_spec((GLA_V, D_MODEL)), _const_spec((FOX_W, D_MODEL)),
            _const_spec((D_MODEL, D_MODEL)), _const_spec((1, D_MODEL)),
            _const_spec((D_MODEL, D_FF)), _const_spec((D_FF, D_MODEL)), _const_spec((1, D_MODEL)),
        ],
        out_specs=tok(D_MODEL),
        out_shape=jax.ShapeDtypeStruct((t_total, D_MODEL), F32),
        compiler_params=pltpu.CompilerParams(
            dimension_semantics=("arbitrary",), vmem_limit_bytes=VMEM_LIMIT),
        name="merge_ffn",
    )(x2d, ygla, ofox, sg, w["w_bg"], w["w_bf"], w["w_out"], w["gn_ffn"], w["w_up"], w["w_down"],
      w["gn_final"])


def _prep_weights(norm_mix, w_in, w_gla_gate, b_gla_gate, g_gla_norm, b_fox_forget, w_branch_gla,
                  w_branch_fox, w_out, norm_ffn, w_up, w_down, norm_final):
    sizes = (GLA_QK, GLA_QK, GLA_V, GLA_V, GLA_RANK, FOX_W, FOX_W, FOX_W, FOX_HEADS, D_MODEL, D_MODEL)
    offs = [0]
    for s in sizes:
        offs.append(offs[-1] + s)
    col = lambda i: w_in[:, offs[i]:offs[i + 1]]
    gq, gk, gv, gr, glr, fq, fk, fv, ff, ga, gb = (col(i) for i in range(11))
    w_nn = jnp.concatenate([gq * (GLA_DK ** -0.5), gv, gr, fq * (FOX_DH ** -0.5 * LOG2E), fk, fv, ga, gb],
                           axis=1).astype(BF16)
    pad = jnp.zeros((D_MODEL, _R_END - _R_FF - FOX_HEADS), F32)
    w_t = jnp.concatenate([gk, glr, ff, pad], axis=1).T.astype(BF16)
    return dict(
        gn_mix=norm_mix.reshape(1, D_MODEL), w_nn=w_nn, w_t=w_t,
        w_gate_t=w_gla_gate.T.astype(BF16), b_gate_col=b_gla_gate.reshape(GLA_QK, 1),
        b_f_col=b_fox_forget.reshape(FOX_HEADS, 1),
        gn_gla=jnp.tile(g_gla_norm.reshape(1, GLA_DV), (1, GLA_HEADS)),
        w_bg=w_branch_gla.astype(BF16), w_bf=w_branch_fox.astype(BF16), w_out=w_out.astype(BF16),
        gn_ffn=norm_ffn.reshape(1, D_MODEL), w_up=w_up.astype(BF16), w_down=w_down.astype(BF16),
        gn_final=norm_final.reshape(1, D_MODEL),
    )


def _state_to_bd(s):
    nbt = s.shape[0]
    eye = jnp.eye(GLA_HEADS, dtype=s.dtype)
    return jnp.einsum('bhkv,hg->bhkgv', s, eye).reshape(nbt, GLA_QK, GLA_V)


def _bd_to_state(s_bd):
    nbt = s_bd.shape[0]
    s5 = s_bd.reshape(nbt, GLA_HEADS, GLA_DK, GLA_HEADS, GLA_DV)
    return jnp.stack([s5[:, h, :, h, :] for h in range(GLA_HEADS)], axis=1)


def _pad_lanes(x):
    n = x.shape[-1]
    npad = -(-n // LANES) * LANES
    return x if npad == n else jnp.pad(x, [(0, 0)] * (x.ndim - 1) + [(0, npad - n)])


def _c_layouts(c, nbatch, n, tq):
    c4 = c.reshape(nbatch, FOX_HEADS // 2, 2, n)
    c_row = jnp.swapaxes(c4.reshape(nbatch, FOX_HEADS // 2, 2, n // tq, tq), 2, 3)
    return c_row, jnp.swapaxes(c4, 2, 3)


def kernel(x_prompt, x_sample, cache_fox_k, cache_fox_v, cache_fox_logf, state_gla, meta_tokens,
           norm_mix, w_in, w_gla_gate, b_gla_gate, g_gla_norm, b_fox_forget, w_branch_gla,
           w_branch_fox, w_out, norm_ffn, w_up, w_down, norm_final):
    bsz, seq, _ = x_prompt.shape
    dbsz, dseq, _ = x_sample.shape
    past = cache_fox_k.shape[3]
    w = _prep_weights(norm_mix[0], w_in[0], w_gla_gate[0], b_gla_gate[0], g_gla_norm[0],
                      b_fox_forget[0], w_branch_gla[0], w_branch_fox[0], w_out[0], norm_ffn[0],
                      w_up[0], w_down[0], norm_final)
    xp = x_prompt.reshape(bsz * seq, D_MODEL)
    xs = x_sample.reshape(dbsz * dseq, D_MODEL)

    (m_qvr, m_kT, m_laT, m_logf, _, _, _, m_k32, m_v32, _) = _inproj(
        meta_tokens.astype(F32), 1, N_META, 1, N_META, N_META, w)
    zero_state = jnp.zeros((1, GLA_QK, GLA_V), F32)
    _, s_meta_bd = _gla(m_qvr, m_kT, m_laT, zero_state, w["gn_gla"], 1, N_META, N_META, N_META, False)

    tm = 512
    (p_qvr, p_kT, p_laT, p_logf, p_fq, p_fkb, p_fvb, p_k32, p_v32, p_sg) = _inproj(
        xp, bsz, seq, 1, tm, LANES, w)
    y_gla, s_p_bd = _gla(p_qvr, p_kT, p_laT, s_meta_bd, w["gn_gla"], bsz, seq, LANES, CHUNK, True)
    lp = _pad_lanes(jnp.broadcast_to(m_logf, (bsz, FOX_HEADS, N_META)).reshape(bsz * FOX_HEADS, N_META))
    cp, cn = _decay(lp, p_logf.reshape(bsz * FOX_HEADS, seq), N_META)
    tq = 256
    c_row, c_col = _c_layouts(cn, bsz, seq, tq)
    cp_row = cp[:FOX_HEADS].reshape(1, FOX_HEADS // 2, 2, LANES)
    kp = jnp.pad(m_k32, ((0, 0), (0, 0), (0, LANES - N_META), (0, 0)))
    vp_t = jnp.swapaxes(jnp.pad(m_v32, ((0, 0), (0, 0), (0, LANES - N_META), (0, 0))), 2, 3)
    o_fox = _fox(p_fq, p_fkb, p_fvb, kp, vp_t, c_col, c_row, cp_row, bsz, seq, tq, LANES, N_META, True)
    y_prompt = _merge_ffn(xp, y_gla, o_fox, p_sg, w, tm).reshape(bsz, seq, D_MODEL)

    (s_qvr, s_kT, s_laT, s_logf, s_fq, s_fkb, s_fvb, s_k32, s_v32, s_sg) = _inproj(
        xs, dbsz, dseq, dbsz, dseq, dseq, w)
    ys_gla, s_s_bd = _gla(s_qvr, s_kT, s_laT, _state_to_bd(state_gla[0].astype(F32)), w["gn_gla"],
                          dbsz, dseq, dseq, dseq, True)
    cps, cns = _decay(cache_fox_logf[0].astype(F32).reshape(dbsz * FOX_HEADS, past),
                      _pad_lanes(s_logf.reshape(dbsz * FOX_HEADS, dseq)), past)
    cs_row, cs_col = _c_layouts(cns[:, :dseq], dbsz, dseq, dseq)
    cps_row = cps.reshape(dbsz, FOX_HEADS // 2, 2, past)
    os_fox = _fox(s_fq, s_fkb, s_fvb, cache_fox_k[0], cache_fox_v[0], cs_col, cs_row, cps_row,
                  dbsz, dseq, dseq, 256, past, False)
    y_sample = _merge_ffn(xs, ys_gla, os_fox, s_sg, w, dbsz * dseq).reshape(dbsz, dseq, D_MODEL)

    bc = lambda a: jnp.broadcast_to(a, (bsz,) + a.shape[1:])
    new_fox_k_prompt = jnp.concatenate([bc(m_k32), p_k32], axis=2)[None]
    new_fox_v_prompt = jnp.concatenate([bc(m_v32), p_v32], axis=2)[None]
    new_fox_logf_prompt = jnp.concatenate([bc(m_logf), p_logf], axis=2)[None]
    new_gla_state_prompt = _bd_to_state(s_p_bd)[None]
    return (y_prompt, y_sample, new_fox_k_prompt, new_fox_v_prompt, new_fox_logf_prompt,
            new_gla_state_prompt, s_k32[None], s_v32[None], s_logf[None], _bd_to_state(s_s_bd)[None])
```

```python
import functools

import jax
import jax.numpy as jnp
from jax import lax
from jax.experimental import pallas as pl
from jax.experimental.pallas import tpu as pltpu

F32 = jnp.float32
BF16 = jnp.bfloat16

D_MODEL = 1024
N_META = 16
EPS = 1e-6
GLA_HEADS = 4
GLA_DK = 64
GLA_DV = 128
GLA_RANK = 16
GLA_TAU = 16.0
GLA_QK = GLA_HEADS * GLA_DK
GLA_V = GLA_HEADS * GLA_DV
FOX_HEADS = 8
FOX_DH = 64
FOX_W = FOX_HEADS * FOX_DH
D_FF = 4 * D_MODEL
NEG = -1e30
CHUNK = 64
LOG2E = 1.4426950408889634

LANES = 128
VMEM_LIMIT = 56 * 1024 * 1024

_C_GQ, _C_GV, _C_GR, _C_FQ, _C_FK, _C_FV, _C_GA, _C_GB, _C_END = (
    0, 256, 768, 1280, 1792, 2304, 2816, 3840, 4864)
_R_K, _R_LR, _R_FF, _R_END = 0, 256, 272, 288


def _dot(a, b):
    return jnp.dot(a, b, preferred_element_type=F32)


def _dot_nt(a, b):
    return lax.dot_general(a, b, (((1,), (1,)), ((), ())), preferred_element_type=F32)


def _split3(x):
    hi = x.astype(BF16)
    r = x - hi.astype(F32)
    mid = r.astype(BF16)
    lo = (r - mid.astype(F32)).astype(BF16)
    return hi, mid, lo


def _dot3(x, m):
    hi, mid, lo = _split3(x)
    return _dot(hi, m) + _dot(mid, m) + _dot(lo, m)


def _log_sigmoid(x):
    return jnp.minimum(x, 0.0) - jnp.log1p(jnp.exp(-jnp.abs(x)))


def _const_spec(shape):
    nd = len(shape)
    return pl.BlockSpec(shape, lambda *_: (0,) * nd, pipeline_mode=pl.Buffered(1))


def _inproj_kernel(x_ref, gn_ref, wnn_ref, wt_ref, wg_ref, bg_ref, bf_ref,
                   qvr_ref, kT_ref, laT_ref, logf_ref, fq_ref, fkb_ref, fvb_ref,
                   fk32_ref, fv32_ref, sg_ref, *, nb, tmb, tw):
    x = x_ref[...]
    ms = jnp.mean(x * x, axis=-1, keepdims=True)
    hn = ((x * lax.rsqrt(ms + EPS)) * gn_ref[...]).astype(BF16)

    qvr_ref[...] = _dot(hn, wnn_ref[:, _C_GQ:_C_FQ]).astype(BF16)

    zf = _dot(hn, wnn_ref[:, _C_FQ:_C_GA])
    fq_ref[...] = zf[:, 0:FOX_W].astype(BF16)
    fkb_ref[...] = zf[:, FOX_W:2 * FOX_W].astype(BF16)
    fvb_ref[...] = zf[:, 2 * FOX_W:3 * FOX_W].astype(BF16)
    for bb in range(nb):
        r0 = bb * tmb
        for h in range(FOX_HEADS):
            c0 = FOX_W + h * FOX_DH
            fk32_ref[bb, h] = zf[r0:r0 + tmb, c0:c0 + FOX_DH]
            fv32_ref[bb, h] = zf[r0:r0 + tmb, c0 + FOX_W:c0 + FOX_W + FOX_DH]

    sg_ref[...] = jax.nn.sigmoid(_dot(hn, wnn_ref[:, _C_GA:_C_END])).astype(BF16)

    zT = _dot_nt(wt_ref[...], hn)
    kT = zT[_R_K:_R_LR].astype(BF16)
    glrT = zT[_R_LR:_R_FF].astype(BF16)
    laT = _log_sigmoid(_dot(wg_ref[...], glrT) + bg_ref[...]) * (1.0 / GLA_TAU)
    logf = _log_sigmoid(zT[_R_FF:_R_FF + FOX_HEADS] + bf_ref[...])
    for bb in range(nb):
        r0 = bb * tmb
        logf_ref[bb] = logf[:, r0:r0 + tmb]
        for j in range(tmb // tw):
            kT_ref[bb, j] = kT[:, r0 + j * tw:r0 + (j + 1) * tw]
            laT_ref[bb, j] = laT[:, r0 + j * tw:r0 + (j + 1) * tw]


def _inproj(x2d, nbatch, n, nb, tmb, tw, w, row_off=0):
    tm = nb * tmb
    ngrp, nt = nbatch // nb, n // tmb
    t_total = nbatch * n
    tok = lambda width: pl.BlockSpec((tm, width), lambda g, i: (g * nt + i, 0))
    out_shape = (
        jax.ShapeDtypeStruct((t_total, _C_FQ), BF16),
        jax.ShapeDtypeStruct((nbatch, n // tw, GLA_QK, tw), BF16),
        jax.ShapeDtypeStruct((nbatch, n // tw, GLA_QK, tw), F32),
        jax.ShapeDtypeStruct((nbatch, FOX_HEADS, n), F32),
        jax.ShapeDtypeStruct((t_total, FOX_W), BF16),
        jax.ShapeDtypeStruct((t_total, FOX_W), BF16),
        jax.ShapeDtypeStruct((t_total, FOX_W), BF16),
        jax.ShapeDtypeStruct((nbatch, FOX_HEADS, row_off + n, FOX_DH), F32),
        jax.ShapeDtypeStruct((nbatch, FOX_HEADS, row_off + n, FOX_DH), F32),
        jax.ShapeDtypeStruct((t_total, 2 * D_MODEL), BF16),
    )
    tblk = pl.BlockSpec((nb, tmb // tw, GLA_QK, tw), lambda g, i: (g, i, 0, 0))
    if row_off:
        hblk = pl.BlockSpec(tuple(pl.Element(d) for d in (nb, FOX_HEADS, tmb, FOX_DH)),
                            lambda g, i: (g * nb, 0, pl.multiple_of(row_off + i * tmb, 8), 0))
    else:
        hblk = pl.BlockSpec((nb, FOX_HEADS, tmb, FOX_DH), lambda g, i: (g, 0, i, 0))
    out_specs = (
        tok(_C_FQ), tblk, tblk,
        pl.BlockSpec((nb, FOX_HEADS, tmb), lambda g, i: (g, 0, i)),
        tok(FOX_W), tok(FOX_W), tok(FOX_W), hblk, hblk, tok(2 * D_MODEL),
    )
    in_specs = [
        tok(D_MODEL),
        _const_spec((1, D_MODEL)),
        _const_spec((D_MODEL, _C_END)),
        _const_spec((_R_END, D_MODEL)),
        _const_spec((GLA_QK, GLA_RANK)),
        _const_spec((GLA_QK, 1)),
        _const_spec((FOX_HEADS, 1)),
    ]
    return pl.pallas_call(
        functools.partial(_inproj_kernel, nb=nb, tmb=tmb, tw=tw),
        grid=(ngrp, nt), in_specs=in_specs, out_specs=out_specs, out_shape=out_shape,
        compiler_params=pltpu.CompilerParams(
            dimension_semantics=("arbitrary", "arbitrary"), vmem_limit_bytes=VMEM_LIMIT),
        name="inproj",
    )(x2d, w["gn_mix"], w["w_nn"], w["w_t"], w["w_gate_t"], w["b_gate_col"], w["b_f_col"])


def _prefix_fill_kernel(mk_ref, mv_ref, kin_ref, vin_ref, k_ref, v_ref):
    del kin_ref, vin_ref
    k_ref[0] = mk_ref[0]
    v_ref[0] = mv_ref[0]


def _prefix_fill(mk, mv, kbuf, vbuf):
    nbatch = kbuf.shape[0]
    p = mk.shape[2]
    shared = pl.BlockSpec((1, FOX_HEADS, p, FOX_DH), lambda b: (0, 0, 0, 0))
    lead = pl.BlockSpec((1, FOX_HEADS, p, FOX_DH), lambda b: (b, 0, 0, 0))
    anyspec = pl.BlockSpec(memory_space=pl.ANY)
    return pl.pallas_call(
        _prefix_fill_kernel, grid=(nbatch,),
        in_specs=[shared, shared, anyspec, anyspec], out_specs=(lead, lead),
        out_shape=(jax.ShapeDtypeStruct(kbuf.shape, kbuf.dtype), jax.ShapeDtypeStruct(vbuf.shape, vbuf.dtype)),
        input_output_aliases={2: 0, 3: 1},
        compiler_params=pltpu.CompilerParams(dimension_semantics=("arbitrary",)),
        name="prefix_fill",
    )(mk, mv, kbuf, vbuf)


def _cumsum_lanes(x, carry):
    n = x.shape[1]
    r = lax.broadcasted_iota(jnp.int32, (LANES, LANES), 0)
    c = lax.broadcasted_iota(jnp.int32, (LANES, LANES), 1)
    upper = jnp.where(r <= c, 1.0, 0.0).astype(BF16)
    outs = []
    for j in range(n // LANES):
        cj = _dot3(x[:, j * LANES:(j + 1) * LANES], upper) + carry
        outs.append(cj)
        carry = cj[:, LANES - 1:LANES]
    return jnp.concatenate(outs, axis=1), carry


def _decay_kernel(lp_ref, ln_ref, cp_ref, cn_ref, *, p_valid):
    lp = lp_ref[...]
    lane = lax.broadcasted_iota(jnp.int32, lp.shape, 1)
    lp = jnp.where(lane < p_valid, lp, 0.0)
    zero = jnp.zeros((lp.shape[0], 1), F32)
    cp, tot = _cumsum_lanes(lp, zero)
    cp_ref[...] = cp * LOG2E
    cn, _ = _cumsum_lanes(ln_ref[...], tot)
    cn_ref[...] = cn * LOG2E


def _decay(logf_p, logf_n, p_valid):
    r, pp = logf_p.shape
    npad = logf_n.shape[1]
    return pl.pallas_call(
        functools.partial(_decay_kernel, p_valid=p_valid),
        out_shape=(jax.ShapeDtypeStruct((r, pp), F32), jax.ShapeDtypeStruct((r, npad), F32)),
        compiler_params=pltpu.CompilerParams(vmem_limit_bytes=VMEM_LIMIT),
        name="decay",
    )(logf_p, logf_n)


def _gla_kernel(qvr_ref, kT_ref, laT_ref, s0_ref, gn_ref, y_ref, sout_ref, s_ref,
                *, n, tw, chunk, need_y):
    nblk = n // tw
    ncb = tw // chunk
    s_ref[...] = s0_ref[0]

    tr = lax.broadcasted_iota(jnp.int32, (tw, tw), 0)
    tc = lax.broadcasted_iota(jnp.int32, (tw, tw), 1)
    same = (tr // chunk) == (tc // chunk)
    m_cum = jnp.where(same & (tr <= tc), 1.0, 0.0).astype(BF16)
    m_tot = jnp.where(same, 1.0, 0.0).astype(BF16)
    sr = lax.broadcasted_iota(jnp.int32, (tw, LANES), 0)
    m_sel = [jnp.where((sr // chunk) == ci, 1.0, 0.0).astype(BF16) for ci in range(ncb)]
    br = lax.broadcasted_iota(jnp.int32, (GLA_QK, GLA_V), 0)
    bc = lax.broadcasted_iota(jnp.int32, (GLA_QK, GLA_V), 1)
    head_mask = (br // GLA_DK) == (bc // GLA_DV)
    gn = gn_ref[...]

    def block(jb, _):
        laT = laT_ref[0, jb]
        hi, mid, lo = _split3(laT)
        d3 = lambda m: _dot(hi, m) + _dot(mid, m) + _dot(lo, m)
        g_cum = d3(m_cum)
        g_tot = d3(m_tot)
        kdec = (kT_ref[0, jb].astype(F32) * jnp.exp(g_tot - g_cum)).astype(BF16)
        for ci in range(ncb):
            row0 = pl.multiple_of(jb * tw, tw) + ci * chunk
            rows = pl.ds(row0, chunk)
            a = jnp.exp(d3(m_sel[ci]))
            a_bd = jnp.concatenate([a] * (GLA_V // LANES), axis=1)
            v = qvr_ref[rows, _C_GV:_C_GR]
            u = _dot(kdec[:, ci * chunk:(ci + 1) * chunk], v)
            s_new = s_ref[...] * a_bd + jnp.where(head_mask, u, 0.0)
            s_ref[...] = s_new
            if need_y:
                q = qvr_ref[rows, _C_GQ:_C_GV]
                o = _dot(q, s_new.astype(BF16))
                gate = jax.nn.silu(qvr_ref[rows, _C_GR:_C_FQ].astype(F32))
                parts = []
                for h in range(GLA_HEADS):
                    oh = o[:, h * GLA_DV:(h + 1) * GLA_DV]
                    msq = jnp.mean(oh * oh, axis=-1, keepdims=True)
                    parts.append(oh * lax.rsqrt(msq + EPS))
                on = jnp.concatenate(parts, axis=1) * gn
                y_ref[rows, :] = (on * gate).astype(BF16)
        return 0

    lax.fori_loop(0, nblk, block, 0)
    sout_ref[0] = s_ref[...]


def _gla(qvr, kT, laT, s0_bd, gn_tiled, nbatch, n, tw, chunk, need_y):
    s0_map = (lambda b: (0, 0, 0)) if s0_bd.shape[0] == 1 else (lambda b: (b, 0, 0))
    kern = functools.partial(_gla_kernel, n=n, tw=tw, chunk=chunk, need_y=need_y)
    if not need_y:
        def kern(qvr_ref, kT_ref, laT_ref, s0_ref, gn_ref, sout_ref, s_ref):
            _gla_kernel(qvr_ref, kT_ref, laT_ref, s0_ref, gn_ref, None, sout_ref, s_ref,
                        n=n, tw=tw, chunk=chunk, need_y=False)
    out_shape = [jax.ShapeDtypeStruct((nbatch, GLA_QK, GLA_V), F32)]
    out_specs = [pl.BlockSpec((1, GLA_QK, GLA_V), lambda b: (b, 0, 0))]
    if need_y:
        out_shape = [jax.ShapeDtypeStruct((nbatch * n, GLA_V), BF16)] + out_shape
        out_specs = [pl.BlockSpec((n, GLA_V), lambda b: (b, 0))] + out_specs
    res = pl.pallas_call(
        kern, grid=(nbatch,),
        in_specs=[
            pl.BlockSpec((n, _C_FQ), lambda b: (b, 0)),
            pl.BlockSpec((1, n // tw, GLA_QK, tw), lambda b: (b, 0, 0, 0)),
            pl.BlockSpec((1, n // tw, GLA_QK, tw), lambda b: (b, 0, 0, 0)),
            pl.BlockSpec((1, GLA_QK, GLA_V), s0_map),
            _const_spec((1, GLA_V)),
        ],
        out_specs=out_specs, out_shape=out_shape,
        scratch_shapes=[pltpu.VMEM((GLA_QK, GLA_V), F32)],
        compiler_params=pltpu.CompilerParams(
            dimension_semantics=("arbitrary",), vmem_limit_bytes=VMEM_LIMIT),
        name="gla",
    )(qvr, kT, laT, s0_bd, gn_tiled)
    return res if need_y else (None, res[0])


def _fold_lanes(x, op):
    w = x.shape[1]
    if w % LANES == 0:
        f = x[:, 0:LANES]
        for g in range(1, w // LANES):
            f = op(f, x[:, g * LANES:(g + 1) * LANES])
        return f
    assert op is jnp.add
    lane = lax.broadcasted_iota(jnp.int32, (x.shape[0], LANES), 1)
    return jnp.where(lane == 0, jnp.sum(x, axis=-1, keepdims=True), 0.0)


def _tile_lanes(x, w):
    return x[:, 0:w] if w < LANES else jnp.concatenate([x] * (w // LANES), axis=1)


def _fox_kernel(q_ref, kn_ref, vn_ref, kp_ref, vp_ref, ccol_ref, crow_ref, cprow_ref, o_ref,
                s_scr, p_scr, vt_scr, sh_scr, l_scr, *, n, tq, tkp, p_pad, p_valid, pv_t):
    nq = n // tq
    rb = min(tq, 64)
    nbuf = s_scr.shape[0] // 2

    class RowReduce:
        def __init__(self, op, red):
            self.op, self.red, self.lane, self.col = op, red, None, None

        def add(self, x):
            if x.shape[1] % LANES == 0:
                f = x[:, 0:LANES]
                for g in range(1, x.shape[1] // LANES):
                    f = self.op(f, x[:, g * LANES:(g + 1) * LANES])
                self.lane = f if self.lane is None else self.op(self.lane, f)
            else:
                r = self.red(x, axis=-1, keepdims=True)
                self.col = r if self.col is None else self.op(self.col, r)

        def result(self):
            out = None if self.lane is None else self.red(self.lane, axis=-1, keepdims=True)
            if self.col is not None:
                out = self.col if out is None else self.op(out, self.col)
            return out

    if pv_t:
        vt_scr[...] = vn_ref[...].astype(F32).T.astype(BF16)
    vp = [vp_ref[0, hh].astype(BF16) for hh in range(2)]
    for qi in range(nq):
        rows = slice(qi * tq, (qi + 1) * tq)
        w_new = (qi + 1) * tq
        chunks = [(c0, tkp) for c0 in range(0, p_pad, tkp)] + [(p_pad + j * tq, tq) for j in range(qi + 1)]
        sbs = [(qi % nbuf) * 2 + hh for hh in range(2)]

        for hh in range(2):
            lo, hi = hh * FOX_DH, (hh + 1) * FOX_DH
            sb = sbs[hh]
            q = q_ref[rows, lo:hi]
            cq = ccol_ref[0, 0, rows, hh:hh + 1]
            mx = RowReduce(jnp.maximum, jnp.max)
            for jp in range(p_pad // tkp):
                cols = slice(jp * tkp, (jp + 1) * tkp)
                k = kp_ref[0, hh, cols, :].astype(BF16)
                s = _dot_nt(q, k) - cprow_ref[0, 0, hh:hh + 1, cols]
                if (jp + 1) * tkp > p_valid:
                    col = lax.broadcasted_iota(jnp.int32, s.shape, 1) + jp * tkp
                    s = jnp.where(col < p_valid, s, NEG)
                s_scr[sb, :, cols] = s
                mx.add(s)
            for j in range(qi + 1):
                krows = slice(j * tq, (j + 1) * tq)
                s = _dot_nt(q, kn_ref[krows, lo:hi]) - crow_ref[0, 0, j, hh:hh + 1, :]
                if j == qi:
                    r = lax.broadcasted_iota(jnp.int32, s.shape, 0)
                    c = lax.broadcasted_iota(jnp.int32, s.shape, 1)
                    s = jnp.where(c <= r, s, NEG)
                s_scr[sb, :, p_pad + j * tq:p_pad + (j + 1) * tq] = s
                mx.add(s)
            m = mx.result() + cq
            sh_scr[hh] = jnp.broadcast_to(m - cq, (tq, LANES))

        def row_group(g, carry):
            rr = pl.ds(pl.multiple_of(g * rb, rb), rb)
            for hh in range(2):
                sh = sh_scr[hh, rr, :]
                lsum = None
                for c0, cw in chunks:
                    p = jnp.exp2(s_scr[sbs[hh], rr, c0:c0 + cw] - _tile_lanes(sh, cw))
                    p_scr[sbs[hh], rr, c0:c0 + cw] = p.astype(BF16)
                    f = _fold_lanes(p, jnp.add)
                    lsum = f if lsum is None else lsum + f
                l_scr[hh, rr, :] = lsum
            return carry

        lax.fori_loop(0, tq // rb, row_group, 0, unroll=True)

        accs = []
        for hh in range(2):
            lo, hi = hh * FOX_DH, (hh + 1) * FOX_DH
            p_pre, p_new = p_scr[sbs[hh], :, 0:p_pad], p_scr[sbs[hh], :, p_pad:p_pad + w_new]
            if pv_t:
                accs.append(_dot_nt(vp[hh], p_pre) + _dot_nt(vt_scr[lo:hi, 0:w_new], p_new))
            else:
                accs.append(_dot(p_pre, vp[hh]) + _dot(p_new, vn_ref[0:w_new, lo:hi]))
        acc = jnp.concatenate(accs, axis=0).T if pv_t else jnp.concatenate(accs, axis=1)
        lane = lax.broadcasted_iota(jnp.int32, acc.shape, 1)
        l0, l1 = (jnp.sum(l_scr[hh], axis=-1, keepdims=True) for hh in range(2))
        o_ref[rows, :] = (acc * jnp.where(lane < FOX_DH, 1.0 / l0, 1.0 / l1)).astype(BF16)


def _fox(q, kn, vn, kp, vp, c_col, c_row, cp_row, nbatch, n, tq, tkp, p_valid, pv_t):
    nq = n // tq
    p_pad = kp.shape[2]
    pmap4 = (lambda b, hp: (0, hp, 0, 0)) if kp.shape[0] == 1 else (lambda b, hp: (b, hp, 0, 0))
    cmap4 = (lambda b, hp: (0, hp, 0, 0)) if cp_row.shape[0] == 1 else (lambda b, hp: (b, hp, 0, 0))
    tok = pl.BlockSpec((n, LANES), lambda b, hp: (b, hp))
    nbuf = min(nq, 2)
    return pl.pallas_call(
        functools.partial(_fox_kernel, n=n, tq=tq, tkp=tkp, p_pad=p_pad, p_valid=p_valid, pv_t=pv_t),
        grid=(nbatch, FOX_HEADS // 2),
        in_specs=[
            tok, tok, tok,
            pl.BlockSpec((1, 2, p_pad, FOX_DH), pmap4),
            pl.BlockSpec((1, 2) + vp.shape[2:], pmap4),
            pl.BlockSpec((1, 1, n, 2), lambda b, hp: (b, hp, 0, 0)),
            pl.BlockSpec((1, 1, nq, 2, tq), lambda b, hp: (b, hp, 0, 0, 0)),
            pl.BlockSpec((1, 1, 2, p_pad), cmap4),
        ],
        out_specs=tok,
        out_shape=jax.ShapeDtypeStruct((nbatch * n, FOX_W), BF16),
        scratch_shapes=[pltpu.VMEM((2 * nbuf, tq, p_pad + n), F32),
                        pltpu.VMEM((2 * nbuf, tq, p_pad + n), BF16),
                        pltpu.VMEM((LANES, n), BF16),
                        pltpu.VMEM((2, tq, LANES), F32), pltpu.VMEM((2, tq, LANES), F32)],
        compiler_params=pltpu.CompilerParams(
            dimension_semantics=("arbitrary", "arbitrary"), vmem_limit_bytes=VMEM_LIMIT),
        name="fox",
    )(q, kn, vn, kp, vp, c_col, c_row, cp_row)


def _merge_ffn_kernel(x_ref, ygla_ref, ofox_ref, sg_ref, wbg_ref, wbf_ref, wout_ref, gffn_ref,
                      wup_ref, wdown_ref, gfin_ref, y_ref, *, ff_chunk):
    ya = _dot(ygla_ref[...], wbg_ref[...])
    yb = _dot(ofox_ref[...], wbf_ref[...])
    m = sg_ref[:, 0:D_MODEL].astype(F32) * ya + sg_ref[:, D_MODEL:2 * D_MODEL].astype(F32) * yb
    h = x_ref[...] + _dot(m.astype(BF16), wout_ref[...])
    ms = jnp.mean(h * h, axis=-1, keepdims=True)
    hn = ((h * lax.rsqrt(ms + EPS)) * gffn_ref[...]).astype(BF16)
    for c in range(D_FF // ff_chunk):
        u = _dot(hn, wup_ref[:, c * ff_chunk:(c + 1) * ff_chunk])
        r = jnp.square(jnp.maximum(u, 0.0)).astype(BF16)
        h = h + _dot(r, wdown_ref[c * ff_chunk:(c + 1) * ff_chunk, :])
    ms = jnp.mean(h * h, axis=-1, keepdims=True)
    y_ref[...] = (h * lax.rsqrt(ms + EPS)) * gfin_ref[...]


def _merge_ffn(x2d, ygla, ofox, sg, w, tm):
    t_total = x2d.shape[0]
    tok = lambda width: pl.BlockSpec((tm, width), lambda i: (i, 0))
    return pl.pallas_call(
        functools.partial(_merge_ffn_kernel, ff_chunk=1024),
        grid=(t_total // tm,),
        in_specs=[
            tok(D_MODEL), tok(GLA_V), tok(FOX_W), tok(2 * D_MODEL),
            _const_spec((GLA_V, D_MODEL)), _const_spec((FOX_W, D_MODEL)),
            _const_spec((D_MODEL, D_MODEL)), _const_spec((1, D_MODEL)),
            _const_spec((D_MODEL, D_FF)), _const_spec((D_FF, D_MODEL)), _const_spec((1, D_MODEL)),
        ],
        out_specs=tok(D_MODEL),
        out_shape=jax.ShapeDtypeStruct((t_total, D_MODEL), F32),
        compiler_params=pltpu.CompilerParams(
            dimension_semantics=("arbitrary",), vmem_limit_bytes=VMEM_LIMIT),
        name="merge_ffn",
    )(x2d, ygla, ofox, sg, w["w_bg"], w["w_bf"], w["w_out"], w["gn_ffn"], w["w_up"], w["w_down"],
      w["gn_final"])


def _prep_weights(norm_mix, w_in, w_gla_gate, b_gla_gate, g_gla_norm, b_fox_forget, w_branch_gla,
                  w_branch_fox, w_out, norm_ffn, w_up, w_down, norm_final):
    sizes = (GLA_QK, GLA_QK, GLA_V, GLA_V, GLA_RANK, FOX_W, FOX_W, FOX_W, FOX_HEADS, D_MODEL, D_MODEL)
    offs = [0]
    for s in sizes:
        offs.append(offs[-1] + s)
    col = lambda i: w_in[:, offs[i]:offs[i + 1]]
    gq, gk, gv, gr, glr, fq, fk, fv, ff, ga, gb = (col(i) for i in range(11))
    w_nn = jnp.concatenate([gq * (GLA_DK ** -0.5), gv, gr, fq * (FOX_DH ** -0.5 * LOG2E), fk, fv, ga, gb],
                           axis=1).astype(BF16)
    pad = jnp.zeros((D_MODEL, _R_END - _R_FF - FOX_HEADS), F32)
    w_t = jnp.concatenate([gk, glr, ff, pad], axis=1).T.astype(BF16)
    return dict(
        gn_mix=norm_mix.reshape(1, D_MODEL), w_nn=w_nn, w_t=w_t,
        w_gate_t=w_gla_gate.T.astype(BF16), b_gate_col=b_gla_gate.reshape(GLA_QK, 1),
        b_f_col=b_fox_forget.reshape(FOX_HEADS, 1),
        gn_gla=jnp.tile(g_gla_norm.reshape(1, GLA_DV), (1, GLA_HEADS)),
        w_bg=w_branch_gla.astype(BF16), w_bf=w_branch_fox.astype(BF16), w_out=w_out.astype(BF16),
        gn_ffn=norm_ffn.reshape(1, D_MODEL), w_up=w_up.astype(BF16), w_down=w_down.astype(BF16),
        gn_final=norm_final.reshape(1, D_MODEL),
    )


def _state_to_bd(s):
    nbt = s.shape[0]
    eye = jnp.eye(GLA_HEADS, dtype=s.dtype)
    return jnp.einsum('bhkv,hg->bhkgv', s, eye).reshape(nbt, GLA_QK, GLA_V)


def _bd_to_state(s_bd):
    nbt = s_bd.shape[0]
    s5 = s_bd.reshape(nbt, GLA_HEADS, GLA_DK, GLA_HEADS, GLA_DV)
    return jnp.stack([s5[:, h, :, h, :] for h in range(GLA_HEADS)], axis=1)


def _pad_lanes(x):
    n = x.shape[-1]
    npad = -(-n // LANES) * LANES
    return x if npad == n else jnp.pad(x, [(0, 0)] * (x.ndim - 1) + [(0, npad - n)])


def _c_layouts(c, nbatch, n, tq):
    c4 = c.reshape(nbatch, FOX_HEADS // 2, 2, n)
    c_row = jnp.swapaxes(c4.reshape(nbatch, FOX_HEADS // 2, 2, n // tq, tq), 2, 3)
    return c_row, jnp.swapaxes(c4, 2, 3)


def kernel(x_prompt, x_sample, cache_fox_k, cache_fox_v, cache_fox_logf, state_gla, meta_tokens,
           norm_mix, w_in, w_gla_gate, b_gla_gate, g_gla_norm, b_fox_forget, w_branch_gla,
           w_branch_fox, w_out, norm_ffn, w_up, w_down, norm_final):
    bsz, seq, _ = x_prompt.shape
    dbsz, dseq, _ = x_sample.shape
    past = cache_fox_k.shape[3]
    w = _prep_weights(norm_mix[0], w_in[0], w_gla_gate[0], b_gla_gate[0], g_gla_norm[0],
                      b_fox_forget[0], w_branch_gla[0], w_branch_fox[0], w_out[0], norm_ffn[0],
                      w_up[0], w_down[0], norm_final)
    xp = x_prompt.reshape(bsz * seq, D_MODEL)
    xs = x_sample.reshape(dbsz * dseq, D_MODEL)

    (m_qvr, m_kT, m_laT, m_logf, _, _, _, m_k32, m_v32, _) = _inproj(
        meta_tokens.astype(F32), 1, N_META, 1, N_META, N_META, w)
    zero_state = jnp.zeros((1, GLA_QK, GLA_V), F32)
    _, s_meta_bd = _gla(m_qvr, m_kT, m_laT, zero_state, w["gn_gla"], 1, N_META, N_META, N_META, False)

    tm = 512
    (p_qvr, p_kT, p_laT, p_logf, p_fq, p_fkb, p_fvb, p_k32, p_v32, p_sg) = _inproj(
        xp, bsz, seq, 1, tm, LANES, w, row_off=N_META)
    p_k32, p_v32 = _prefix_fill(m_k32, m_v32, p_k32, p_v32)
    y_gla, s_p_bd = _gla(p_qvr, p_kT, p_laT, s_meta_bd, w["gn_gla"], bsz, seq, LANES, CHUNK, True)
    lp = _pad_lanes(jnp.broadcast_to(m_logf, (bsz, FOX_HEADS, N_META)).reshape(bsz * FOX_HEADS, N_META))
    cp, cn = _decay(lp, p_logf.reshape(bsz * FOX_HEADS, seq), N_META)
    tq = 256
    c_row, c_col = _c_layouts(cn, bsz, seq, tq)
    cp_row = cp[:FOX_HEADS].reshape(1, FOX_HEADS // 2, 2, LANES)
    kp = jnp.pad(m_k32, ((0, 0), (0, 0), (0, LANES - N_META), (0, 0)))
    vp_t = jnp.swapaxes(jnp.pad(m_v32, ((0, 0), (0, 0), (0, LANES - N_META), (0, 0))), 2, 3)
    o_fox = _fox(p_fq, p_fkb, p_fvb, kp, vp_t, c_col, c_row, cp_row, bsz, seq, tq, LANES, N_META, True)
    y_prompt = _merge_ffn(xp, y_gla, o_fox, p_sg, w, tm).reshape(bsz, seq, D_MODEL)

    (s_qvr, s_kT, s_laT, s_logf, s_fq, s_fkb, s_fvb, s_k32, s_v32, s_sg) = _inproj(
        xs, dbsz, dseq, dbsz, dseq, dseq, w)
    ys_gla, s_s_bd = _gla(s_qvr, s_kT, s_laT, _state_to_bd(state_gla[0].astype(F32)), w["gn_gla"],
                          dbsz, dseq, dseq, dseq, True)
    cps, cns = _decay(cache_fox_logf[0].astype(F32).reshape(dbsz * FOX_HEADS, past),
                      _pad_lanes(s_logf.reshape(dbsz * FOX_HEADS, dseq)), past)
    cs_row, cs_col = _c_layouts(cns[:, :dseq], dbsz, dseq, dseq)
    cps_row = cps.reshape(dbsz, FOX_HEADS // 2, 2, past)
    os_fox = _fox(s_fq, s_fkb, s_fvb, cache_fox_k[0], cache_fox_v[0], cs_col, cs_row, cps_row,
                  dbsz, dseq, dseq, 256, past, False)
    y_sample = _merge_ffn(xs, ys_gla, os_fox, s_sg, w, dbsz * dseq).reshape(dbsz, dseq, D_MODEL)

    bc = lambda a: jnp.broadcast_to(a, (bsz,) + a.shape[1:])
    new_fox_k_prompt, new_fox_v_prompt = p_k32[None], p_v32[None]
    new_fox_logf_prompt = jnp.concatenate([bc(m_logf), p_logf], axis=2)[None]
    new_gla_state_prompt = _bd_to_state(s_p_bd)[None]
    return (y_prompt, y_sample, new_fox_k_prompt, new_fox_v_prompt, new_fox_logf_prompt,
            new_gla_state_prompt, s_k32[None], s_v32[None], s_logf[None], _bd_to_state(s_s_bd)[None])
```

```python
import functools

import jax
import jax.numpy as jnp
from jax import lax
from jax.experimental import pallas as pl
from jax.experimental.pallas import tpu as pltpu

F32 = jnp.float32
BF16 = jnp.bfloat16

D_MODEL = 1024
N_META = 16
EPS = 1e-6
GLA_HEADS = 4
GLA_DK = 64
GLA_DV = 128
GLA_RANK = 16
GLA_TAU = 16.0
GLA_QK = GLA_HEADS * GLA_DK
GLA_V = GLA_HEADS * GLA_DV
FOX_HEADS = 8
FOX_DH = 64
FOX_W = FOX_HEADS * FOX_DH
D_FF = 4 * D_MODEL
NEG = -1e30
CHUNK = 64
LOG2E = 1.4426950408889634

LANES = 128
VMEM_LIMIT = 56 * 1024 * 1024

_C_GQ, _C_GV, _C_GR, _C_FQ, _C_FK, _C_FV, _C_GA, _C_GB, _C_END = (
    0, 256, 768, 1280, 1792, 2304, 2816, 3840, 4864)
_R_K, _R_LR, _R_FF, _R_END = 0, 256, 272, 288


def _dot(a, b):
    return jnp.dot(a, b, preferred_element_type=F32)


def _dot_nt(a, b):
    return lax.dot_general(a, b, (((1,), (1,)), ((), ())), preferred_element_type=F32)


def _split3(x):
    hi = x.astype(BF16)
    r = x - hi.astype(F32)
    mid = r.astype(BF16)
    lo = (r - mid.astype(F32)).astype(BF16)
    return hi, mid, lo


def _dot3(x, m):
    hi, mid, lo = _split3(x)
    return _dot(hi, m) + _dot(mid, m) + _dot(lo, m)


def _log_sigmoid(x):
    return jnp.minimum(x, 0.0) - jnp.log1p(jnp.exp(-jnp.abs(x)))


def _const_spec(shape):
    nd = len(shape)
    return pl.BlockSpec(shape, lambda *_: (0,) * nd, pipeline_mode=pl.Buffered(1))


def _inproj_kernel(x_ref, gn_ref, wnn_ref, wt_ref, wg_ref, bg_ref, bf_ref,
                   qvr_ref, kT_ref, laT_ref, logf_ref, fq_ref, fkb_ref, fvb_ref,
                   fk32_ref, fv32_ref, sg_ref, *, nb, tmb, tw):
    x = x_ref[...]
    ms = jnp.mean(x * x, axis=-1, keepdims=True)
    hn = ((x * lax.rsqrt(ms + EPS)) * gn_ref[...]).astype(BF16)

    qvr_ref[...] = _dot(hn, wnn_ref[:, _C_GQ:_C_FQ]).astype(BF16)

    zf = _dot(hn, wnn_ref[:, _C_FQ:_C_GA])
    fq_ref[...] = zf[:, 0:FOX_W].astype(BF16)
    fkb_ref[...] = zf[:, FOX_W:2 * FOX_W].astype(BF16)
    fvb_ref[...] = zf[:, 2 * FOX_W:3 * FOX_W].astype(BF16)
    for bb in range(nb):
        r0 = bb * tmb
        for h in range(FOX_HEADS):
            c0 = FOX_W + h * FOX_DH
            fk32_ref[bb, h] = zf[r0:r0 + tmb, c0:c0 + FOX_DH]
            fv32_ref[bb, h] = zf[r0:r0 + tmb, c0 + FOX_W:c0 + FOX_W + FOX_DH]

    sg_ref[...] = jax.nn.sigmoid(_dot(hn, wnn_ref[:, _C_GA:_C_END])).astype(BF16)

    zT = _dot_nt(wt_ref[...], hn)
    kT = zT[_R_K:_R_LR].astype(BF16)
    glrT = zT[_R_LR:_R_FF].astype(BF16)
    laT = _log_sigmoid(_dot(wg_ref[...], glrT) + bg_ref[...]) * (1.0 / GLA_TAU)
    logf = _log_sigmoid(zT[_R_FF:_R_FF + FOX_HEADS] + bf_ref[...])
    for bb in range(nb):
        r0 = bb * tmb
        logf_ref[bb] = logf[:, r0:r0 + tmb]
        for j in range(tmb // tw):
            kT_ref[bb, j] = kT[:, r0 + j * tw:r0 + (j + 1) * tw]
            laT_ref[bb, j] = laT[:, r0 + j * tw:r0 + (j + 1) * tw]


def _inproj(x2d, nbatch, n, nb, tmb, tw, w, row_off=0):
    tm = nb * tmb
    ngrp, nt = nbatch // nb, n // tmb
    t_total = nbatch * n
    tok = lambda width: pl.BlockSpec((tm, width), lambda g, i: (g * nt + i, 0))
    out_shape = (
        jax.ShapeDtypeStruct((t_total, _C_FQ), BF16),
        jax.ShapeDtypeStruct((nbatch, n // tw, GLA_QK, tw), BF16),
        jax.ShapeDtypeStruct((nbatch, n // tw, GLA_QK, tw), F32),
        jax.ShapeDtypeStruct((nbatch, FOX_HEADS, n), F32),
        jax.ShapeDtypeStruct((t_total, FOX_W), BF16),
        jax.ShapeDtypeStruct((t_total, FOX_W), BF16),
        jax.ShapeDtypeStruct((t_total, FOX_W), BF16),
        jax.ShapeDtypeStruct((nbatch, FOX_HEADS, row_off + n, FOX_DH), F32),
        jax.ShapeDtypeStruct((nbatch, FOX_HEADS, row_off + n, FOX_DH), F32),
        jax.ShapeDtypeStruct((t_total, 2 * D_MODEL), BF16),
    )
    tblk = pl.BlockSpec((nb, tmb // tw, GLA_QK, tw), lambda g, i: (g, i, 0, 0))
    if row_off:
        hblk = pl.BlockSpec(tuple(pl.Element(d) for d in (nb, FOX_HEADS, tmb, FOX_DH)),
                            lambda g, i: (g * nb, 0, pl.multiple_of(row_off + i * tmb, 8), 0))
    else:
        hblk = pl.BlockSpec((nb, FOX_HEADS, tmb, FOX_DH), lambda g, i: (g, 0, i, 0))
    out_specs = (
        tok(_C_FQ), tblk, tblk,
        pl.BlockSpec((nb, FOX_HEADS, tmb), lambda g, i: (g, 0, i)),
        tok(FOX_W), tok(FOX_W), tok(FOX_W), hblk, hblk, tok(2 * D_MODEL),
    )
    in_specs = [
        tok(D_MODEL),
        _const_spec((1, D_MODEL)),
        _const_spec((D_MODEL, _C_END)),
        _const_spec((_R_END, D_MODEL)),
        _const_spec((GLA_QK, GLA_RANK)),
        _const_spec((GLA_QK, 1)),
        _const_spec((FOX_HEADS, 1)),
    ]
    return pl.pallas_call(
        functools.partial(_inproj_kernel, nb=nb, tmb=tmb, tw=tw),
        grid=(ngrp, nt), in_specs=in_specs, out_specs=out_specs, out_shape=out_shape,
        compiler_params=pltpu.CompilerParams(
            dimension_semantics=("arbitrary", "arbitrary"), vmem_limit_bytes=VMEM_LIMIT),
        name="inproj",
    )(x2d, w["gn_mix"], w["w_nn"], w["w_t"], w["w_gate_t"], w["b_gate_col"], w["b_f_col"])


def _prefix_fill_kernel(mk_ref, mv_ref, kin_ref, vin_ref, k_ref, v_ref):
    del kin_ref, vin_ref
    k_ref[0] = mk_ref[0]
    v_ref[0] = mv_ref[0]


def _prefix_fill(mk, mv, kbuf, vbuf):
    nbatch = kbuf.shape[0]
    p = mk.shape[2]
    shared = pl.BlockSpec((1, FOX_HEADS, p, FOX_DH), lambda b: (0, 0, 0, 0))
    lead = pl.BlockSpec((1, FOX_HEADS, p, FOX_DH), lambda b: (b, 0, 0, 0))
    anyspec = pl.BlockSpec(memory_space=pl.ANY)
    return pl.pallas_call(
        _prefix_fill_kernel, grid=(nbatch,),
        in_specs=[shared, shared, anyspec, anyspec], out_specs=(lead, lead),
        out_shape=(jax.ShapeDtypeStruct(kbuf.shape, kbuf.dtype), jax.ShapeDtypeStruct(vbuf.shape, vbuf.dtype)),
        input_output_aliases={2: 0, 3: 1},
        compiler_params=pltpu.CompilerParams(dimension_semantics=("arbitrary",)),
        name="prefix_fill",
    )(mk, mv, kbuf, vbuf)


def _cumsum_lanes(x, carry):
    n = x.shape[1]
    r = lax.broadcasted_iota(jnp.int32, (LANES, LANES), 0)
    c = lax.broadcasted_iota(jnp.int32, (LANES, LANES), 1)
    upper = jnp.where(r <= c, 1.0, 0.0).astype(BF16)
    outs = []
    for j in range(n // LANES):
        cj = _dot3(x[:, j * LANES:(j + 1) * LANES], upper) + carry
        outs.append(cj)
        carry = cj[:, LANES - 1:LANES]
    return jnp.concatenate(outs, axis=1), carry


def _decay_kernel(lp_ref, ln_ref, cp_ref, cn_ref, *, p_valid):
    lp = lp_ref[...]
    lane = lax.broadcasted_iota(jnp.int32, lp.shape, 1)
    lp = jnp.where(lane < p_valid, lp, 0.0)
    zero = jnp.zeros((lp.shape[0], 1), F32)
    cp, tot = _cumsum_lanes(lp, zero)
    cp_ref[...] = cp * LOG2E
    cn, _ = _cumsum_lanes(ln_ref[...], tot)
    cn_ref[...] = cn * LOG2E


def _decay(logf_p, logf_n, p_valid):
    r, pp = logf_p.shape
    npad = logf_n.shape[1]
    return pl.pallas_call(
        functools.partial(_decay_kernel, p_valid=p_valid),
        out_shape=(jax.ShapeDtypeStruct((r, pp), F32), jax.ShapeDtypeStruct((r, npad), F32)),
        compiler_params=pltpu.CompilerParams(vmem_limit_bytes=VMEM_LIMIT),
        name="decay",
    )(logf_p, logf_n)


def _gla_kernel(qvr_ref, kT_ref, laT_ref, s0_ref, gn_ref, y_ref, sout_ref, *, n, tw, chunk, need_y):
    nblk = n // tw
    ncb = tw // chunk

    sr = lax.broadcasted_iota(jnp.int32, (tw, LANES), 0)
    tr = lax.broadcasted_iota(jnp.int32, (tw, tw), 0)
    tc = lax.broadcasted_iota(jnp.int32, (tw, tw), 1)
    m_cum = ((tr // chunk) == (tc // chunk)) & (tr <= tc)
    m_all = jnp.concatenate(
        [jnp.where((sr // chunk) == ci, 1.0, 0.0) for ci in range(ncb)] + [jnp.where(m_cum, 1.0, 0.0)],
        axis=1).astype(BF16)
    stack_k = tw % LANES == 0
    m_rhs = jnp.concatenate([m_all] * 3, axis=0) if stack_k else m_all
    lane_chunk = lax.broadcasted_iota(jnp.int32, (GLA_QK, tw), 1) // chunk
    zero_blk = jnp.zeros((GLA_DK, GLA_DV), BF16)
    gn = gn_ref[...]

    def group(jg, state):
        chunks = []
        for t in range(nbg):
            jb = jg * nbg + t
            hi, mid, lo = _split3(laT_ref[0, jb])
            if stack_k:
                g = _dot(jnp.concatenate([hi, mid, lo], axis=1), m_rhs)
            else:
                g = _dot(hi, m_rhs) + _dot(mid, m_rhs) + _dot(lo, m_rhs)
            tot = [g[:, ci * LANES:(ci + 1) * LANES] for ci in range(ncb)]
            g_cum = g[:, ncb * LANES:ncb * LANES + tw]
            g_tot = tot[0][:, 0:tw] if tw <= LANES else jnp.concatenate([tot[0]] * (tw // LANES), axis=1)
            for ci in range(1, ncb):
                g_tot = jnp.where(lane_chunk == ci, tot[ci][:, 0:tw], g_tot)
            kdec = (kT_ref[0, jb].astype(F32) * jnp.exp(g_tot - g_cum)).astype(BF16)
            for ci in range(ncb):
                rows = pl.ds(pl.multiple_of(jb * tw, tw) + ci * chunk, chunk)
                v = qvr_ref[rows, _C_GV:_C_GR]
                us = [_dot(kdec[h * GLA_DK:(h + 1) * GLA_DK, ci * chunk:(ci + 1) * chunk],
                           v[:, h * GLA_DV:(h + 1) * GLA_DV]) for h in range(GLA_HEADS)]
                chunks.append((rows, jnp.exp(tot[ci]), us))
        s_bds = []
        for rows, a, us in chunks:
            state = tuple(a[h * GLA_DK:(h + 1) * GLA_DK] * state[h] + us[h] for h in range(GLA_HEADS))
            if need_y:
                s_bds.append(jnp.concatenate(
                    [jnp.concatenate([state[h].astype(BF16) if hc == h else zero_blk
                                      for hc in range(GLA_HEADS)], axis=1) for h in range(GLA_HEADS)], axis=0))
        if need_y:
            for (rows, _, _), s_bd in zip(chunks, s_bds):
                o = _dot(qvr_ref[rows, _C_GQ:_C_GV], s_bd)
                gate = jax.nn.silu(qvr_ref[rows, _C_GR:_C_FQ].astype(F32))
                parts = []
                for h in range(GLA_HEADS):
                    oh = o[:, h * GLA_DV:(h + 1) * GLA_DV]
                    msq = jnp.mean(oh * oh, axis=-1, keepdims=True)
                    parts.append(oh * lax.rsqrt(msq + EPS))
                on = jnp.concatenate(parts, axis=1) * gn
                y_ref[rows, :] = (on * gate).astype(BF16)
        return state

    nbg = 4 if nblk % 4 == 0 else 1
    state = tuple(s0_ref[0, h] for h in range(GLA_HEADS))
    state = lax.fori_loop(0, nblk // nbg, group, state)
    for h in range(GLA_HEADS):
        sout_ref[0, h] = state[h]


def _gla(qvr, kT, laT, s0, gn_tiled, nbatch, n, tw, chunk, need_y):
    s0_map = (lambda b: (0, 0, 0, 0)) if s0.shape[0] == 1 else (lambda b: (b, 0, 0, 0))
    kern = functools.partial(_gla_kernel, n=n, tw=tw, chunk=chunk, need_y=need_y)
    if not need_y:
        def kern(qvr_ref, kT_ref, laT_ref, s0_ref, gn_ref, sout_ref):
            _gla_kernel(qvr_ref, kT_ref, laT_ref, s0_ref, gn_ref, None, sout_ref,
                        n=n, tw=tw, chunk=chunk, need_y=False)
    state_shape = (GLA_HEADS, GLA_DK, GLA_DV)
    out_shape = [jax.ShapeDtypeStruct((nbatch,) + state_shape, F32)]
    out_specs = [pl.BlockSpec((1,) + state_shape, lambda b: (b, 0, 0, 0))]
    if need_y:
        out_shape = [jax.ShapeDtypeStruct((nbatch * n, GLA_V), BF16)] + out_shape
        out_specs = [pl.BlockSpec((n, GLA_V), lambda b: (b, 0))] + out_specs
    res = pl.pallas_call(
        kern, grid=(nbatch,),
        in_specs=[
            pl.BlockSpec((n, _C_FQ), lambda b: (b, 0)),
            pl.BlockSpec((1, n // tw, GLA_QK, tw), lambda b: (b, 0, 0, 0)),
            pl.BlockSpec((1, n // tw, GLA_QK, tw), lambda b: (b, 0, 0, 0)),
            pl.BlockSpec((1,) + state_shape, s0_map),
            _const_spec((1, GLA_V)),
        ],
        out_specs=out_specs, out_shape=out_shape,
        compiler_params=pltpu.CompilerParams(
            dimension_semantics=("arbitrary",), vmem_limit_bytes=VMEM_LIMIT),
        name="gla",
    )(qvr, kT, laT, s0, gn_tiled)
    return res if need_y else (None, res[0])


def _fold_lanes(x, op):
    w = x.shape[1]
    if w % LANES == 0:
        f = x[:, 0:LANES]
        for g in range(1, w // LANES):
            f = op(f, x[:, g * LANES:(g + 1) * LANES])
        return f
    assert op is jnp.add
    lane = lax.broadcasted_iota(jnp.int32, (x.shape[0], LANES), 1)
    return jnp.where(lane == 0, jnp.sum(x, axis=-1, keepdims=True), 0.0)


def _tile_lanes(x, w):
    return x[:, 0:w] if w < LANES else jnp.concatenate([x] * (w // LANES), axis=1)


def _fox_kernel(q_ref, kn_ref, vn_ref, kp_ref, vp_ref, ccol_ref, crow_ref, cprow_ref, o_ref,
                s_scr, p_scr, vt_scr, sh_scr, l_scr, *, n, tq, tkp, p_pad, p_valid, pv_t):
    nq = n // tq
    rb = min(tq, 64)
    nbuf = s_scr.shape[0] // 2

    class RowReduce:
        def __init__(self, op, red):
            self.op, self.red, self.lane, self.col = op, red, None, None

        def add(self, x):
            if x.shape[1] % LANES == 0:
                f = x[:, 0:LANES]
                for g in range(1, x.shape[1] // LANES):
                    f = self.op(f, x[:, g * LANES:(g + 1) * LANES])
                self.lane = f if self.lane is None else self.op(self.lane, f)
            else:
                r = self.red(x, axis=-1, keepdims=True)
                self.col = r if self.col is None else self.op(self.col, r)

        def result(self):
            out = None if self.lane is None else self.red(self.lane, axis=-1, keepdims=True)
            if self.col is not None:
                out = self.col if out is None else self.op(out, self.col)
            return out

    if pv_t:
        vt_scr[...] = vn_ref[...].astype(F32).T.astype(BF16)
    vp = [vp_ref[0, hh].astype(BF16) for hh in range(2)]
    for qi in range(nq):
        rows = slice(qi * tq, (qi + 1) * tq)
        w_new = (qi + 1) * tq
        chunks = [(c0, tkp) for c0 in range(0, p_pad, tkp)] + [(p_pad + j * tq, tq) for j in range(qi + 1)]
        sbs = [(qi % nbuf) * 2 + hh for hh in range(2)]

        for hh in range(2):
            lo, hi = hh * FOX_DH, (hh + 1) * FOX_DH
            sb = sbs[hh]
            q = q_ref[rows, lo:hi]
            cq = ccol_ref[0, 0, rows, hh:hh + 1]
            mx = RowReduce(jnp.maximum, jnp.max)
            for jp in range(p_pad // tkp):
                cols = slice(jp * tkp, (jp + 1) * tkp)
                k = kp_ref[0, hh, cols, :].astype(BF16)
                s = _dot_nt(q, k) - cprow_ref[0, 0, hh:hh + 1, cols]
                if (jp + 1) * tkp > p_valid:
                    col = lax.broadcasted_iota(jnp.int32, s.shape, 1) + jp * tkp
                    s = jnp.where(col < p_valid, s, NEG)
                s_scr[sb, :, cols] = s
                mx.add(s)
            for j in range(qi + 1):
                krows = slice(j * tq, (j + 1) * tq)
                s = _dot_nt(q, kn_ref[krows, lo:hi]) - crow_ref[0, 0, j, hh:hh + 1, :]
                if j == qi:
                    r = lax.broadcasted_iota(jnp.int32, s.shape, 0)
                    c = lax.broadcasted_iota(jnp.int32, s.shape, 1)
                    s = jnp.where(c <= r, s, NEG)
                s_scr[sb, :, p_pad + j * tq:p_pad + (j + 1) * tq] = s
                mx.add(s)
            m = mx.result() + cq
            sh_scr[hh] = jnp.broadcast_to(m - cq, (tq, LANES))

        def row_group(g, carry):
            rr = pl.ds(pl.multiple_of(g * rb, rb), rb)
            for hh in range(2):
                sh = sh_scr[hh, rr, :]
                lsum = None
                for c0, cw in chunks:
                    p = jnp.exp2(s_scr[sbs[hh], rr, c0:c0 + cw] - _tile_lanes(sh, cw))
                    p_scr[sbs[hh], rr, c0:c0 + cw] = p.astype(BF16)
                    f = _fold_lanes(p, jnp.add)
                    lsum = f if lsum is None else lsum + f
                l_scr[hh, rr, :] = lsum
            return carry

        lax.fori_loop(0, tq // rb, row_group, 0, unroll=True)

        accs = []
        for hh in range(2):
            lo, hi = hh * FOX_DH, (hh + 1) * FOX_DH
            p_pre, p_new = p_scr[sbs[hh], :, 0:p_pad], p_scr[sbs[hh], :, p_pad:p_pad + w_new]
            if pv_t:
                accs.append(_dot_nt(vp[hh], p_pre) + _dot_nt(vt_scr[lo:hi, 0:w_new], p_new))
            else:
                accs.append(_dot(p_pre, vp[hh]) + _dot(p_new, vn_ref[0:w_new, lo:hi]))
        acc = jnp.concatenate(accs, axis=0).T if pv_t else jnp.concatenate(accs, axis=1)
        lane = lax.broadcasted_iota(jnp.int32, acc.shape, 1)
        l0, l1 = (jnp.sum(l_scr[hh], axis=-1, keepdims=True) for hh in range(2))
        o_ref[rows, :] = (acc * jnp.where(lane < FOX_DH, 1.0 / l0, 1.0 / l1)).astype(BF16)


def _fox(q, kn, vn, kp, vp, c_col, c_row, cp_row, nbatch, n, tq, tkp, p_valid, pv_t):
    nq = n // tq
    p_pad = kp.shape[2]
    pmap4 = (lambda b, hp: (0, hp, 0, 0)) if kp.shape[0] == 1 else (lambda b, hp: (b, hp, 0, 0))
    cmap4 = (lambda b, hp: (0, hp, 0, 0)) if cp_row.shape[0] == 1 else (lambda b, hp: (b, hp, 0, 0))
    tok = pl.BlockSpec((n, LANES), lambda b, hp: (b, hp))
    nbuf = min(nq, 2)
    return pl.pallas_call(
        functools.partial(_fox_kernel, n=n, tq=tq, tkp=tkp, p_pad=p_pad, p_valid=p_valid, pv_t=pv_t),
        grid=(nbatch, FOX_HEADS // 2),
        in_specs=[
            tok, tok, tok,
            pl.BlockSpec((1, 2, p_pad, FOX_DH), pmap4),
            pl.BlockSpec((1, 2) + vp.shape[2:], pmap4),
            pl.BlockSpec((1, 1, n, 2), lambda b, hp: (b, hp, 0, 0)),
            pl.BlockSpec((1, 1, nq, 2, tq), lambda b, hp: (b, hp, 0, 0, 0)),
            pl.BlockSpec((1, 1, 2, p_pad), cmap4),
        ],
        out_specs=tok,
        out_shape=jax.ShapeDtypeStruct((nbatch * n, FOX_W), BF16),
        scratch_shapes=[pltpu.VMEM((2 * nbuf, tq, p_pad + n), F32),
                        pltpu.VMEM((2 * nbuf, tq, p_pad + n), BF16),
                        pltpu.VMEM((LANES, n), BF16),
                        pltpu.VMEM((2, tq, LANES), F32), pltpu.VMEM((2, tq, LANES), F32)],
        compiler_params=pltpu.CompilerParams(
            dimension_semantics=("arbitrary", "arbitrary"), vmem_limit_bytes=VMEM_LIMIT),
        name="fox",
    )(q, kn, vn, kp, vp, c_col, c_row, cp_row)


def _merge_ffn_kernel(x_ref, ygla_ref, ofox_ref, sg_ref, wbg_ref, wbf_ref, wout_ref, gffn_ref,
                      wup_ref, wdown_ref, gfin_ref, y_ref, *, ff_chunk):
    ya = _dot(ygla_ref[...], wbg_ref[...])
    yb = _dot(ofox_ref[...], wbf_ref[...])
    m = sg_ref[:, 0:D_MODEL].astype(F32) * ya + sg_ref[:, D_MODEL:2 * D_MODEL].astype(F32) * yb
    h = x_ref[...] + _dot(m.astype(BF16), wout_ref[...])
    ms = jnp.mean(h * h, axis=-1, keepdims=True)
    hn = ((h * lax.rsqrt(ms + EPS)) * gffn_ref[...]).astype(BF16)
    for c in range(D_FF // ff_chunk):
        u = _dot(hn, wup_ref[:, c * ff_chunk:(c + 1) * ff_chunk])
        r = jnp.square(jnp.maximum(u, 0.0)).astype(BF16)
        h = h + _dot(r, wdown_ref[c * ff_chunk:(c + 1) * ff_chunk, :])
    ms = jnp.mean(h * h, axis=-1, keepdims=True)
    y_ref[...] = (h * lax.rsqrt(ms + EPS)) * gfin_ref[...]


def _merge_ffn(x2d, ygla, ofox, sg, w, tm):
    t_total = x2d.shape[0]
    tok = lambda width: pl.BlockSpec((tm, width), lambda i: (i, 0))
    return pl.pallas_call(
        functools.partial(_merge_ffn_kernel, ff_chunk=1024),
        grid=(t_total // tm,),
        in_specs=[
            tok(D_MODEL), tok(GLA_V), tok(FOX_W), tok(2 * D_MODEL),
            _const_spec((GLA_V, D_MODEL)), _const_spec((FOX_W, D_MODEL)),
            _const_spec((D_MODEL, D_MODEL)), _const_spec((1, D_MODEL)),
            _const_spec((D_MODEL, D_FF)), _const_spec((D_FF, D_MODEL)), _const_spec((1, D_MODEL)),
        ],
        out_specs=tok(D_MODEL),
        out_shape=jax.ShapeDtypeStruct((t_total, D_MODEL), F32),
        compiler_params=pltpu.CompilerParams(
            dimension_semantics=("arbitrary",), vmem_limit_bytes=VMEM_LIMIT),
        name="merge_ffn",
    )(x2d, ygla, ofox, sg, w["w_bg"], w["w_bf"], w["w_out"], w["gn_ffn"], w["w_up"], w["w_down"],
      w["gn_final"])


def _prep_weights(norm_mix, w_in, w_gla_gate, b_gla_gate, g_gla_norm, b_fox_forget, w_branch_gla,
                  w_branch_fox, w_out, norm_ffn, w_up, w_down, norm_final):
    sizes = (GLA_QK, GLA_QK, GLA_V, GLA_V, GLA_RANK, FOX_W, FOX_W, FOX_W, FOX_HEADS, D_MODEL, D_MODEL)
    offs = [0]
    for s in sizes:
        offs.append(offs[-1] + s)
    col = lambda i: w_in[:, offs[i]:offs[i + 1]]
    gq, gk, gv, gr, glr, fq, fk, fv, ff, ga, gb = (col(i) for i in range(11))
    w_nn = jnp.concatenate([gq * (GLA_DK ** -0.5), gv, gr, fq * (FOX_DH ** -0.5 * LOG2E), fk, fv, ga, gb],
                           axis=1).astype(BF16)
    pad = jnp.zeros((D_MODEL, _R_END - _R_FF - FOX_HEADS), F32)
    w_t = jnp.concatenate([gk, glr, ff, pad], axis=1).T.astype(BF16)
    return dict(
        gn_mix=norm_mix.reshape(1, D_MODEL), w_nn=w_nn, w_t=w_t,
        w_gate_t=w_gla_gate.T.astype(BF16), b_gate_col=b_gla_gate.reshape(GLA_QK, 1),
        b_f_col=b_fox_forget.reshape(FOX_HEADS, 1),
        gn_gla=jnp.tile(g_gla_norm.reshape(1, GLA_DV), (1, GLA_HEADS)),
        w_bg=w_branch_gla.astype(BF16), w_bf=w_branch_fox.astype(BF16), w_out=w_out.astype(BF16),
        gn_ffn=norm_ffn.reshape(1, D_MODEL), w_up=w_up.astype(BF16), w_down=w_down.astype(BF16),
        gn_final=norm_final.reshape(1, D_MODEL),
    )


def _pad_lanes(x):
    n = x.shape[-1]
    npad = -(-n // LANES) * LANES
    return x if npad == n else jnp.pad(x, [(0, 0)] * (x.ndim - 1) + [(0, npad - n)])


def _c_layouts(c, nbatch, n, tq):
    c4 = c.reshape(nbatch, FOX_HEADS // 2, 2, n)
    c_row = jnp.swapaxes(c4.reshape(nbatch, FOX_HEADS // 2, 2, n // tq, tq), 2, 3)
    return c_row, jnp.swapaxes(c4, 2, 3)


def kernel(x_prompt, x_sample, cache_fox_k, cache_fox_v, cache_fox_logf, state_gla, meta_tokens,
           norm_mix, w_in, w_gla_gate, b_gla_gate, g_gla_norm, b_fox_forget, w_branch_gla,
           w_branch_fox, w_out, norm_ffn, w_up, w_down, norm_final):
    bsz, seq, _ = x_prompt.shape
    dbsz, dseq, _ = x_sample.shape
    past = cache_fox_k.shape[3]
    w = _prep_weights(norm_mix[0], w_in[0], w_gla_gate[0], b_gla_gate[0], g_gla_norm[0],
                      b_fox_forget[0], w_branch_gla[0], w_branch_fox[0], w_out[0], norm_ffn[0],
                      w_up[0], w_down[0], norm_final)
    xp = x_prompt.reshape(bsz * seq, D_MODEL)
    xs = x_sample.reshape(dbsz * dseq, D_MODEL)

    (m_qvr, m_kT, m_laT, m_logf, _, _, _, m_k32, m_v32, _) = _inproj(
        meta_tokens.astype(F32), 1, N_META, 1, N_META, N_META, w)
    zero_state = jnp.zeros((1, GLA_HEADS, GLA_DK, GLA_DV), F32)
    _, s_meta = _gla(m_qvr, m_kT, m_laT, zero_state, w["gn_gla"], 1, N_META, N_META, N_META, False)

    tm = 512
    (p_qvr, p_kT, p_laT, p_logf, p_fq, p_fkb, p_fvb, p_k32, p_v32, p_sg) = _inproj(
        xp, bsz, seq, 1, tm, LANES, w, row_off=N_META)
    p_k32, p_v32 = _prefix_fill(m_k32, m_v32, p_k32, p_v32)
    y_gla, s_p = _gla(p_qvr, p_kT, p_laT, s_meta, w["gn_gla"], bsz, seq, LANES, CHUNK, True)
    lp = _pad_lanes(jnp.broadcast_to(m_logf, (bsz, FOX_HEADS, N_META)).reshape(bsz * FOX_HEADS, N_META))
    cp, cn = _decay(lp, p_logf.reshape(bsz * FOX_HEADS, seq), N_META)
    tq = 256
    c_row, c_col = _c_layouts(cn, bsz, seq, tq)
    cp_row = cp[:FOX_HEADS].reshape(1, FOX_HEADS // 2, 2, LANES)
    kp = jnp.pad(m_k32, ((0, 0), (0, 0), (0, LANES - N_META), (0, 0)))
    vp_t = jnp.swapaxes(jnp.pad(m_v32, ((0, 0), (0, 0), (0, LANES - N_META), (0, 0))), 2, 3)
    o_fox = _fox(p_fq, p_fkb, p_fvb, kp, vp_t, c_col, c_row, cp_row, bsz, seq, tq, LANES, N_META, True)
    y_prompt = _merge_ffn(xp, y_gla, o_fox, p_sg, w, tm).reshape(bsz, seq, D_MODEL)

    (s_qvr, s_kT, s_laT, s_logf, s_fq, s_fkb, s_fvb, s_k32, s_v32, s_sg) = _inproj(
        xs, dbsz, dseq, dbsz, dseq, dseq, w)
    ys_gla, s_s = _gla(s_qvr, s_kT, s_laT, state_gla[0].astype(F32), w["gn_gla"],
                       dbsz, dseq, dseq, dseq, True)
    cps, cns = _decay(cache_fox_logf[0].astype(F32).reshape(dbsz * FOX_HEADS, past),
                      _pad_lanes(s_logf.reshape(dbsz * FOX_HEADS, dseq)), past)
    cs_row, cs_col = _c_layouts(cns[:, :dseq], dbsz, dseq, dseq)
    cps_row = cps.reshape(dbsz, FOX_HEADS // 2, 2, past)
    os_fox = _fox(s_fq, s_fkb, s_fvb, cache_fox_k[0], cache_fox_v[0], cs_col, cs_row, cps_row,
                  dbsz, dseq, dseq, 256, past, False)
    y_sample = _merge_ffn(xs, ys_gla, os_fox, s_sg, w, dbsz * dseq).reshape(dbsz, dseq, D_MODEL)

    bc = lambda a: jnp.broadcast_to(a, (bsz,) + a.shape[1:])
    new_fox_k_prompt, new_fox_v_prompt = p_k32[None], p_v32[None]
    new_fox_logf_prompt = jnp.concatenate([bc(m_logf), p_logf], axis=2)[None]
    return (y_prompt, y_sample, new_fox_k_prompt, new_fox_v_prompt, new_fox_logf_prompt,
            s_p[None], s_k32[None], s_v32[None], s_logf[None], s_s[None])
```

```python
import functools

import jax
import jax.numpy as jnp
from jax import lax
from jax.experimental import pallas as pl
from jax.experimental.pallas import tpu as pltpu

F32 = jnp.float32
BF16 = jnp.bfloat16

D_MODEL = 1024
N_META = 16
EPS = 1e-6
GLA_HEADS = 4
GLA_DK = 64
GLA_DV = 128
GLA_RANK = 16
GLA_TAU = 16.0
GLA_QK = GLA_HEADS * GLA_DK
GLA_V = GLA_HEADS * GLA_DV
FOX_HEADS = 8
FOX_DH = 64
FOX_W = FOX_HEADS * FOX_DH
D_FF = 4 * D_MODEL
NEG = -1e30
CHUNK = 64
LOG2E = 1.4426950408889634

LANES = 128
VMEM_LIMIT = 56 * 1024 * 1024

_C_GQ, _C_GV, _C_GR, _C_FQ, _C_GA, _C_GB, _C_END = 0, 256, 768, 1280, 1792, 2816, 3840
_R_K, _R_LR, _R_FF, _R_FK, _R_FV, _R_END = 0, 256, 272, 288, 800, 1312


def _dot(a, b):
    return jnp.dot(a, b, preferred_element_type=F32)


def _dot_nt(a, b):
    return lax.dot_general(a, b, (((1,), (1,)), ((), ())), preferred_element_type=F32)


def _split3(x):
    hi = x.astype(BF16)
    r = x - hi.astype(F32)
    mid = r.astype(BF16)
    lo = (r - mid.astype(F32)).astype(BF16)
    return hi, mid, lo


def _dot3(x, m):
    hi, mid, lo = _split3(x)
    return _dot(hi, m) + _dot(mid, m) + _dot(lo, m)


def _log_sigmoid(x):
    return jnp.minimum(x, 0.0) - jnp.log1p(jnp.exp(-jnp.abs(x)))


def _const_spec(shape):
    nd = len(shape)
    return pl.BlockSpec(shape, lambda *_: (0,) * nd, pipeline_mode=pl.Buffered(1))


def _inproj_kernel(x_ref, gn_ref, wnn_ref, wt_ref, wg_ref, bg_ref, bf_ref, *rest, nb, tmb, tw, nt, row_off):
    if row_off:
        lead_k_ref, lead_v_ref = rest[:2]
        rest = rest[2:]
    (qvr_ref, kT_ref, laT_ref, logf_ref, fq_ref, fkT_ref, fvT_ref, k32_ref, v32_ref, sg_ref) = rest[:10]
    carries = rest[10:]
    tm = nb * tmb

    def project():
        x = x_ref[...]
        ms = jnp.mean(x * x, axis=-1, keepdims=True)
        hn = ((x * lax.rsqrt(ms + EPS)) * gn_ref[...]).astype(BF16)

        qvr_ref[...] = _dot(hn, wnn_ref[:, _C_GQ:_C_FQ]).astype(BF16)
        fq_ref[...] = _dot(hn, wnn_ref[:, _C_FQ:_C_GA]).astype(BF16)
        sg_ref[...] = jax.nn.sigmoid(_dot(hn, wnn_ref[:, _C_GA:_C_END])).astype(BF16)

        zT = _dot_nt(wt_ref[...], hn)
        kT = zT[_R_K:_R_LR].astype(BF16)
        glrT = zT[_R_LR:_R_FF].astype(BF16)
        laT = _log_sigmoid(_dot(wg_ref[...], glrT) + bg_ref[...]) * (1.0 / GLA_TAU)
        logf = _log_sigmoid(zT[_R_FF:_R_FF + FOX_HEADS] + bf_ref[...])
        zk, zv = zT[_R_FK:_R_FV], zT[_R_FV:_R_END]
        for bb in range(nb):
            r0 = bb * tmb
            logf_ref[bb] = logf[:, r0:r0 + tmb]
            fkT_ref[bb] = zk[:, r0:r0 + tmb].astype(BF16)
            fvT_ref[bb] = zv[:, r0:r0 + tmb].astype(BF16)
            for j in range(tmb // tw):
                kT_ref[bb, j] = kT[:, r0 + j * tw:r0 + (j + 1) * tw]
                laT_ref[bb, j] = laT[:, r0 + j * tw:r0 + (j + 1) * tw]
            if not row_off:
                for h in range(FOX_HEADS):
                    k32_ref[bb, h] = zk[h * FOX_DH:(h + 1) * FOX_DH, r0:r0 + tmb]
                    v32_ref[bb, h] = zv[h * FOX_DH:(h + 1) * FOX_DH, r0:r0 + tmb]
        if row_off:
            lane = lax.broadcasted_iota(jnp.int32, (FOX_W, LANES), 1)
            for z, carry, o_ref in ((zk, carries[0], k32_ref), (zv, carries[1], v32_ref)):
                rolled = pltpu.roll(z, row_off, axis=1)
                first = jnp.where(lane < row_off, carry[...], rolled[:, 0:LANES])
                carry[...] = rolled[:, 0:LANES]
                for h in range(FOX_HEADS):
                    hr = slice(h * FOX_DH, (h + 1) * FOX_DH)
                    o_ref[0, h, :, 0:LANES] = first[hr]
                    o_ref[0, h, :, LANES:tm] = rolled[hr, LANES:tm]

    if not row_off:
        project()
        return

    i = pl.program_id(1)

    @pl.when(i == 0)
    def _():
        carries[0][...] = lead_k_ref[...]
        carries[1][...] = lead_v_ref[...]

    pl.when(i < nt)(project)

    @pl.when(i == nt)
    def _():
        for carry, o_ref in ((carries[0], k32_ref), (carries[1], v32_ref)):
            for h in range(FOX_HEADS):
                o_ref[0, h, :, 0:LANES] = carry[h * FOX_DH:(h + 1) * FOX_DH, :]


def _inproj(x2d, nbatch, n, nb, tmb, tw, w, lead=None):
    tm = nb * tmb
    ngrp, nt = nbatch // nb, n // tmb
    t_total = nbatch * n
    row_off = N_META if lead is not None else 0
    steps = nt + 1 if row_off else nt
    blk = (lambda i: jnp.minimum(i, nt - 1)) if row_off else (lambda i: i)
    tok = lambda width: pl.BlockSpec((tm, width), lambda g, i: (g * nt + blk(i), 0))
    out_shape = (
        jax.ShapeDtypeStruct((t_total, _C_FQ), BF16),
        jax.ShapeDtypeStruct((nbatch, n // tw, GLA_QK, tw), BF16),
        jax.ShapeDtypeStruct((nbatch, n // tw, GLA_QK, tw), F32),
        jax.ShapeDtypeStruct((nbatch, FOX_HEADS, n), F32),
        jax.ShapeDtypeStruct((t_total, FOX_W), BF16),
        jax.ShapeDtypeStruct((nbatch, FOX_W, n), BF16),
        jax.ShapeDtypeStruct((nbatch, FOX_W, n), BF16),
        jax.ShapeDtypeStruct((nbatch, FOX_HEADS, FOX_DH, row_off + n), F32),
        jax.ShapeDtypeStruct((nbatch, FOX_HEADS, FOX_DH, row_off + n), F32),
        jax.ShapeDtypeStruct((t_total, 2 * D_MODEL), BF16),
    )
    tblk = pl.BlockSpec((nb, tmb // tw, GLA_QK, tw), lambda g, i: (g, blk(i), 0, 0))
    lblk = pl.BlockSpec((nb, FOX_W, tmb), lambda g, i: (g, 0, blk(i)))
    sblk = pl.BlockSpec((nb, FOX_HEADS, FOX_DH, tmb), lambda g, i: (g, 0, 0, i))
    out_specs = (
        tok(_C_FQ), tblk, tblk,
        pl.BlockSpec((nb, FOX_HEADS, tmb), lambda g, i: (g, 0, blk(i))),
        tok(FOX_W), lblk, lblk, sblk, sblk, tok(2 * D_MODEL),
    )
    in_specs = [
        tok(D_MODEL),
        _const_spec((1, D_MODEL)),
        _const_spec((D_MODEL, _C_END)),
        _const_spec((_R_END, D_MODEL)),
        _const_spec((GLA_QK, GLA_RANK)),
        _const_spec((GLA_QK, 1)),
        _const_spec((FOX_HEADS, 1)),
    ]
    args = [x2d, w["gn_mix"], w["w_nn"], w["w_t"], w["w_gate_t"], w["b_gate_col"], w["b_f_col"]]
    scratch = []
    if row_off:
        assert nb == 1 and tmb % LANES == 0
        in_specs += [_const_spec((FOX_W, LANES))] * 2
        args += list(lead)
        scratch = [pltpu.VMEM((FOX_W, LANES), F32)] * 2
    return pl.pallas_call(
        functools.partial(_inproj_kernel, nb=nb, tmb=tmb, tw=tw, nt=nt, row_off=row_off),
        grid=(ngrp, steps), in_specs=in_specs, out_specs=out_specs, out_shape=out_shape,
        scratch_shapes=scratch,
        compiler_params=pltpu.CompilerParams(
            dimension_semantics=("arbitrary", "arbitrary"), vmem_limit_bytes=VMEM_LIMIT),
        name="inproj",
    )(*args)


def _cumsum_lanes(x, carry):
    n = x.shape[1]
    r = lax.broadcasted_iota(jnp.int32, (LANES, LANES), 0)
    c = lax.broadcasted_iota(jnp.int32, (LANES, LANES), 1)
    upper = jnp.where(r <= c, 1.0, 0.0).astype(BF16)
    outs = []
    for j in range(n // LANES):
        cj = _dot3(x[:, j * LANES:(j + 1) * LANES], upper) + carry
        outs.append(cj)
        carry = cj[:, LANES - 1:LANES]
    return jnp.concatenate(outs, axis=1), carry


def _decay_kernel(lp_ref, ln_ref, cp_ref, cn_ref, *, p_valid):
    lp = lp_ref[...]
    lane = lax.broadcasted_iota(jnp.int32, lp.shape, 1)
    lp = jnp.where(lane < p_valid, lp, 0.0)
    zero = jnp.zeros((lp.shape[0], 1), F32)
    cp, tot = _cumsum_lanes(lp, zero)
    cp_ref[...] = cp * LOG2E
    cn, _ = _cumsum_lanes(ln_ref[...], tot)
    cn_ref[...] = cn * LOG2E


def _decay(logf_p, logf_n, p_valid):
    r, pp = logf_p.shape
    npad = logf_n.shape[1]
    return pl.pallas_call(
        functools.partial(_decay_kernel, p_valid=p_valid),
        out_shape=(jax.ShapeDtypeStruct((r, pp), F32), jax.ShapeDtypeStruct((r, npad), F32)),
        compiler_params=pltpu.CompilerParams(vmem_limit_bytes=VMEM_LIMIT),
        name="decay",
    )(logf_p, logf_n)


def _gla_kernel(qvr_ref, kT_ref, laT_ref, s0_ref, gn_ref, y_ref, sout_ref, *, n, tw, chunk, need_y):
    nblk = n // tw
    ncb = tw // chunk

    sr = lax.broadcasted_iota(jnp.int32, (tw, LANES), 0)
    tr = lax.broadcasted_iota(jnp.int32, (tw, tw), 0)
    tc = lax.broadcasted_iota(jnp.int32, (tw, tw), 1)
    m_cum = ((tr // chunk) == (tc // chunk)) & (tr <= tc)
    m_all = jnp.concatenate(
        [jnp.where((sr // chunk) == ci, 1.0, 0.0) for ci in range(ncb)] + [jnp.where(m_cum, 1.0, 0.0)],
        axis=1).astype(BF16)
    stack_k = tw % LANES == 0
    m_rhs = jnp.concatenate([m_all] * 3, axis=0) if stack_k else m_all
    lane_chunk = lax.broadcasted_iota(jnp.int32, (GLA_QK, tw), 1) // chunk
    zero_blk = jnp.zeros((GLA_DK, GLA_DV), BF16)
    gn = gn_ref[...]

    def group(jg, state):
        chunks = []
        for t in range(nbg):
            jb = jg * nbg + t
            hi, mid, lo = _split3(laT_ref[0, jb])
            if stack_k:
                g = _dot(jnp.concatenate([hi, mid, lo], axis=1), m_rhs)
            else:
                g = _dot(hi, m_rhs) + _dot(mid, m_rhs) + _dot(lo, m_rhs)
            tot = [g[:, ci * LANES:(ci + 1) * LANES] for ci in range(ncb)]
            g_cum = g[:, ncb * LANES:ncb * LANES + tw]
            g_tot = tot[0][:, 0:tw] if tw <= LANES else jnp.concatenate([tot[0]] * (tw // LANES), axis=1)
            for ci in range(1, ncb):
                g_tot = jnp.where(lane_chunk == ci, tot[ci][:, 0:tw], g_tot)
            kdec = (kT_ref[0, jb].astype(F32) * jnp.exp(g_tot - g_cum)).astype(BF16)
            for ci in range(ncb):
                rows = pl.ds(pl.multiple_of(jb * tw, tw) + ci * chunk, chunk)
                v = qvr_ref[rows, _C_GV:_C_GR]
                us = [_dot(kdec[h * GLA_DK:(h + 1) * GLA_DK, ci * chunk:(ci + 1) * chunk],
                           v[:, h * GLA_DV:(h + 1) * GLA_DV]) for h in range(GLA_HEADS)]
                chunks.append((rows, jnp.exp(tot[ci]), us))
        s_bds = []
        for rows, a, us in chunks:
            state = tuple(a[h * GLA_DK:(h + 1) * GLA_DK] * state[h] + us[h] for h in range(GLA_HEADS))
            if need_y:
                s_bds.append(jnp.concatenate(
                    [jnp.concatenate([state[h].astype(BF16) if hc == h else zero_blk
                                      for hc in range(GLA_HEADS)], axis=1) for h in range(GLA_HEADS)], axis=0))
        if need_y:
            for (rows, _, _), s_bd in zip(chunks, s_bds):
                o = _dot(qvr_ref[rows, _C_GQ:_C_GV], s_bd)
                gate = jax.nn.silu(qvr_ref[rows, _C_GR:_C_FQ].astype(F32))
                parts = []
                for h in range(GLA_HEADS):
                    oh = o[:, h * GLA_DV:(h + 1) * GLA_DV]
                    msq = jnp.mean(oh * oh, axis=-1, keepdims=True)
                    parts.append(oh * lax.rsqrt(msq + EPS))
                on = jnp.concatenate(parts, axis=1) * gn
                y_ref[rows, :] = (on * gate).astype(BF16)
        return state

    nbg = 4 if nblk % 4 == 0 else 1
    state = tuple(s0_ref[0, h] for h in range(GLA_HEADS))
    state = lax.fori_loop(0, nblk // nbg, group, state)
    for h in range(GLA_HEADS):
        sout_ref[0, h] = state[h]


def _gla(qvr, kT, laT, s0, gn_tiled, nbatch, n, tw, chunk, need_y):
    s0_map = (lambda b: (0, 0, 0, 0)) if s0.shape[0] == 1 else (lambda b: (b, 0, 0, 0))
    kern = functools.partial(_gla_kernel, n=n, tw=tw, chunk=chunk, need_y=need_y)
    if not need_y:
        def kern(qvr_ref, kT_ref, laT_ref, s0_ref, gn_ref, sout_ref):
            _gla_kernel(qvr_ref, kT_ref, laT_ref, s0_ref, gn_ref, None, sout_ref,
                        n=n, tw=tw, chunk=chunk, need_y=False)
    state_shape = (GLA_HEADS, GLA_DK, GLA_DV)
    out_shape = [jax.ShapeDtypeStruct((nbatch,) + state_shape, F32)]
    out_specs = [pl.BlockSpec((1,) + state_shape, lambda b: (b, 0, 0, 0))]
    if need_y:
        out_shape = [jax.ShapeDtypeStruct((nbatch * n, GLA_V), BF16)] + out_shape
        out_specs = [pl.BlockSpec((n, GLA_V), lambda b: (b, 0))] + out_specs
    res = pl.pallas_call(
        kern, grid=(nbatch,),
        in_specs=[
            pl.BlockSpec((n, _C_FQ), lambda b: (b, 0)),
            pl.BlockSpec((1, n // tw, GLA_QK, tw), lambda b: (b, 0, 0, 0)),
            pl.BlockSpec((1, n // tw, GLA_QK, tw), lambda b: (b, 0, 0, 0)),
            pl.BlockSpec((1,) + state_shape, s0_map),
            _const_spec((1, GLA_V)),
        ],
        out_specs=out_specs, out_shape=out_shape,
        compiler_params=pltpu.CompilerParams(
            dimension_semantics=("arbitrary",), vmem_limit_bytes=VMEM_LIMIT),
        name="gla",
    )(qvr, kT, laT, s0, gn_tiled)
    return res if need_y else (None, res[0])


def _fold_lanes(x, op):
    w = x.shape[1]
    if w % LANES == 0:
        f = x[:, 0:LANES]
        for g in range(1, w // LANES):
            f = op(f, x[:, g * LANES:(g + 1) * LANES])
        return f
    assert op is jnp.add
    lane = lax.broadcasted_iota(jnp.int32, (x.shape[0], LANES), 1)
    return jnp.where(lane == 0, jnp.sum(x, axis=-1, keepdims=True), 0.0)


def _tile_lanes(x, w):
    return x[:, 0:w] if w < LANES else jnp.concatenate([x] * (w // LANES), axis=1)


def _fox_kernel(q_ref, kn_ref, vn_ref, kp_ref, vp_ref, ccol_ref, crow_ref, cprow_ref, o_ref,
                s_scr, p_scr, sh_scr, l_scr, *, n, tq, tkp, p_pad, p_valid, pv_t):
    nq = n // tq
    rb = min(tq, 64)
    nbuf = s_scr.shape[0] // 2

    class RowReduce:
        def __init__(self, op, red):
            self.op, self.red, self.lane, self.col = op, red, None, None

        def add(self, x):
            if x.shape[1] % LANES == 0:
                f = x[:, 0:LANES]
                for g in range(1, x.shape[1] // LANES):
                    f = self.op(f, x[:, g * LANES:(g + 1) * LANES])
                self.lane = f if self.lane is None else self.op(self.lane, f)
            else:
                r = self.red(x, axis=-1, keepdims=True)
                self.col = r if self.col is None else self.op(self.col, r)

        def result(self):
            out = None if self.lane is None else self.red(self.lane, axis=-1, keepdims=True)
            if self.col is not None:
                out = self.col if out is None else self.op(out, self.col)
            return out

    vp = [vp_ref[0, hh].astype(BF16) for hh in range(2)]
    for qi in range(nq):
        rows = slice(qi * tq, (qi + 1) * tq)
        w_new = (qi + 1) * tq
        chunks = [(c0, tkp) for c0 in range(0, p_pad, tkp)] + [(p_pad + j * tq, tq) for j in range(qi + 1)]
        sbs = [(qi % nbuf) * 2 + hh for hh in range(2)]

        for hh in range(2):
            lo, hi = hh * FOX_DH, (hh + 1) * FOX_DH
            sb = sbs[hh]
            q = q_ref[rows, lo:hi]
            cq = ccol_ref[0, 0, rows, hh:hh + 1]
            mx = RowReduce(jnp.maximum, jnp.max)
            for jp in range(p_pad // tkp):
                cols = slice(jp * tkp, (jp + 1) * tkp)
                k = kp_ref[0, hh, :, cols].astype(BF16)
                s = _dot(q, k) - cprow_ref[0, 0, hh:hh + 1, cols]
                if (jp + 1) * tkp > p_valid:
                    col = lax.broadcasted_iota(jnp.int32, s.shape, 1) + jp * tkp
                    s = jnp.where(col < p_valid, s, NEG)
                s_scr[sb, :, cols] = s
                mx.add(s)
            for j in range(qi + 1):
                s = _dot(q, kn_ref[0, lo:hi, j * tq:(j + 1) * tq]) - crow_ref[0, 0, j, hh:hh + 1, :]
                if j == qi:
                    r = lax.broadcasted_iota(jnp.int32, s.shape, 0)
                    c = lax.broadcasted_iota(jnp.int32, s.shape, 1)
                    s = jnp.where(c <= r, s, NEG)
                s_scr[sb, :, p_pad + j * tq:p_pad + (j + 1) * tq] = s
                mx.add(s)
            m = mx.result() + cq
            sh_scr[hh] = jnp.broadcast_to(m - cq, (tq, LANES))

        def row_group(g, carry):
            rr = pl.ds(pl.multiple_of(g * rb, rb), rb)
            for hh in range(2):
                sh = sh_scr[hh, rr, :]
                lsum = None
                for c0, cw in chunks:
                    p = jnp.exp2(s_scr[sbs[hh], rr, c0:c0 + cw] - _tile_lanes(sh, cw))
                    p_scr[sbs[hh], rr, c0:c0 + cw] = p.astype(BF16)
                    f = _fold_lanes(p, jnp.add)
                    lsum = f if lsum is None else lsum + f
                l_scr[hh, rr, :] = lsum
            return carry

        lax.fori_loop(0, tq // rb, row_group, 0, unroll=True)

        accs = []
        for hh in range(2):
            lo, hi = hh * FOX_DH, (hh + 1) * FOX_DH
            p_pre, p_new = p_scr[sbs[hh], :, 0:p_pad], p_scr[sbs[hh], :, p_pad:p_pad + w_new]
            v_new = vn_ref[0, lo:hi, 0:w_new]
            if pv_t:
                accs.append(_dot_nt(vp[hh], p_pre) + _dot_nt(v_new, p_new))
            else:
                accs.append(_dot_nt(p_pre, vp[hh]) + _dot_nt(p_new, v_new))
        acc = jnp.concatenate(accs, axis=0).T if pv_t else jnp.concatenate(accs, axis=1)
        lane = lax.broadcasted_iota(jnp.int32, acc.shape, 1)
        l0, l1 = (jnp.sum(l_scr[hh], axis=-1, keepdims=True) for hh in range(2))
        o_ref[rows, :] = (acc * jnp.where(lane < FOX_DH, 1.0 / l0, 1.0 / l1)).astype(BF16)


def _fox(q, kn, vn, kp, vp, c_col, c_row, cp_row, nbatch, n, tq, tkp, p_valid, pv_t):
    nq = n // tq
    p_pad = kp.shape[3]
    pmap4 = (lambda b, hp: (0, hp, 0, 0)) if kp.shape[0] == 1 else (lambda b, hp: (b, hp, 0, 0))
    cmap4 = (lambda b, hp: (0, hp, 0, 0)) if cp_row.shape[0] == 1 else (lambda b, hp: (b, hp, 0, 0))
    tok = pl.BlockSpec((n, LANES), lambda b, hp: (b, hp))
    tlanes = pl.BlockSpec((1, LANES, n), lambda b, hp: (b, hp, 0))
    nbuf = min(nq, 2)
    return pl.pallas_call(
        functools.partial(_fox_kernel, n=n, tq=tq, tkp=tkp, p_pad=p_pad, p_valid=p_valid, pv_t=pv_t),
        grid=(nbatch, FOX_HEADS // 2),
        in_specs=[
            tok, tlanes, tlanes,
            pl.BlockSpec((1, 2, FOX_DH, p_pad), pmap4),
            pl.BlockSpec((1, 2, FOX_DH, p_pad), pmap4),
            pl.BlockSpec((1, 1, n, 2), lambda b, hp: (b, hp, 0, 0)),
            pl.BlockSpec((1, 1, nq, 2, tq), lambda b, hp: (b, hp, 0, 0, 0)),
            pl.BlockSpec((1, 1, 2, p_pad), cmap4),
        ],
        out_specs=tok,
        out_shape=jax.ShapeDtypeStruct((nbatch * n, FOX_W), BF16),
        scratch_shapes=[pltpu.VMEM((2 * nbuf, tq, p_pad + n), F32),
                        pltpu.VMEM((2 * nbuf, tq, p_pad + n), BF16),
                        pltpu.VMEM((2, tq, LANES), F32), pltpu.VMEM((2, tq, LANES), F32)],
        compiler_params=pltpu.CompilerParams(
            dimension_semantics=("arbitrary", "arbitrary"), vmem_limit_bytes=VMEM_LIMIT),
        name="fox",
    )(q, kn, vn, kp, vp, c_col, c_row, cp_row)


def _merge_ffn_kernel(x_ref, ygla_ref, ofox_ref, sg_ref, wbg_ref, wbf_ref, wout_ref, gffn_ref,
                      wup_ref, wdown_ref, gfin_ref, y_ref, *, ff_chunk):
    ya = _dot(ygla_ref[...], wbg_ref[...])
    yb = _dot(ofox_ref[...], wbf_ref[...])
    m = sg_ref[:, 0:D_MODEL].astype(F32) * ya + sg_ref[:, D_MODEL:2 * D_MODEL].astype(F32) * yb
    h = x_ref[...] + _dot(m.astype(BF16), wout_ref[...])
    ms = jnp.mean(h * h, axis=-1, keepdims=True)
    hn = ((h * lax.rsqrt(ms + EPS)) * gffn_ref[...]).astype(BF16)
    for c in range(D_FF // ff_chunk):
        u = _dot(hn, wup_ref[:, c * ff_chunk:(c + 1) * ff_chunk])
        r = jnp.square(jnp.maximum(u, 0.0)).astype(BF16)
        h = h + _dot(r, wdown_ref[c * ff_chunk:(c + 1) * ff_chunk, :])
    ms = jnp.mean(h * h, axis=-1, keepdims=True)
    y_ref[...] = (h * lax.rsqrt(ms + EPS)) * gfin_ref[...]


def _merge_ffn(x2d, ygla, ofox, sg, w, tm):
    t_total = x2d.shape[0]
    tok = lambda width: pl.BlockSpec((tm, width), lambda i: (i, 0))
    return pl.pallas_call(
        functools.partial(_merge_ffn_kernel, ff_chunk=1024),
        grid=(t_total // tm,),
        in_specs=[
            tok(D_MODEL), tok(GLA_V), tok(FOX_W), tok(2 * D_MODEL),
            _const_spec((GLA_V, D_MODEL)), _const_spec((FOX_W, D_MODEL)),
            _const_spec((D_MODEL, D_MODEL)), _const_spec((1, D_MODEL)),
            _const_spec((D_MODEL, D_FF)), _const_spec((D_FF, D_MODEL)), _const_spec((1, D_MODEL)),
        ],
        out_specs=tok(D_MODEL),
        out_shape=jax.ShapeDtypeStruct((t_total, D_MODEL), F32),
        compiler_params=pltpu.CompilerParams(
            dimension_semantics=("arbitrary",), vmem_limit_bytes=VMEM_LIMIT),
        name="merge_ffn",
    )(x2d, ygla, ofox, sg, w["w_bg"], w["w_bf"], w["w_out"], w["gn_ffn"], w["w_up"], w["w_down"],
      w["gn_final"])


def _prep_weights(norm_mix, w_in, w_gla_gate, b_gla_gate, g_gla_norm, b_fox_forget, w_branch_gla,
                  w_branch_fox, w_out, norm_ffn, w_up, w_down, norm_final):
    sizes = (GLA_QK, GLA_QK, GLA_V, GLA_V, GLA_RANK, FOX_W, FOX_W, FOX_W, FOX_HEADS, D_MODEL, D_MODEL)
    offs = [0]
    for s in sizes:
        offs.append(offs[-1] + s)
    col = lambda i: w_in[:, offs[i]:offs[i + 1]]
    gq, gk, gv, gr, glr, fq, fk, fv, ff, ga, gb = (col(i) for i in range(11))
    w_nn = jnp.concatenate([gq * (GLA_DK ** -0.5), gv, gr, fq * (FOX_DH ** -0.5 * LOG2E), ga, gb],
                           axis=1).astype(BF16)
    pad = jnp.zeros((D_MODEL, _R_FK - _R_FF - FOX_HEADS), F32)
    w_t = jnp.concatenate([gk, glr, ff, pad, fk, fv], axis=1).T.astype(BF16)
    return dict(
        gn_mix=norm_mix.reshape(1, D_MODEL), w_nn=w_nn, w_t=w_t,
        w_gate_t=w_gla_gate.T.astype(BF16), b_gate_col=b_gla_gate.reshape(GLA_QK, 1),
        b_f_col=b_fox_forget.reshape(FOX_HEADS, 1),
        gn_gla=jnp.tile(g_gla_norm.reshape(1, GLA_DV), (1, GLA_HEADS)),
        w_bg=w_branch_gla.astype(BF16), w_bf=w_branch_fox.astype(BF16), w_out=w_out.astype(BF16),
        gn_ffn=norm_ffn.reshape(1, D_MODEL), w_up=w_up.astype(BF16), w_down=w_down.astype(BF16),
        gn_final=norm_final.reshape(1, D_MODEL),
    )


def _pad_lanes(x):
    n = x.shape[-1]
    npad = -(-n // LANES) * LANES
    return x if npad == n else jnp.pad(x, [(0, 0)] * (x.ndim - 1) + [(0, npad - n)])


def _c_layouts(c, nbatch, n, tq):
    c4 = c.reshape(nbatch, FOX_HEADS // 2, 2, n)
    c_row = jnp.swapaxes(c4.reshape(nbatch, FOX_HEADS // 2, 2, n // tq, tq), 2, 3)
    return c_row, jnp.swapaxes(c4, 2, 3)


def kernel(x_prompt, x_sample, cache_fox_k, cache_fox_v, cache_fox_logf, state_gla, meta_tokens,
           norm_mix, w_in, w_gla_gate, b_gla_gate, g_gla_norm, b_fox_forget, w_branch_gla,
           w_branch_fox, w_out, norm_ffn, w_up, w_down, norm_final):
    bsz, seq, _ = x_prompt.shape
    dbsz, dseq, _ = x_sample.shape
    past = cache_fox_k.shape[3]
    w = _prep_weights(norm_mix[0], w_in[0], w_gla_gate[0], b_gla_gate[0], g_gla_norm[0],
                      b_fox_forget[0], w_branch_gla[0], w_branch_fox[0], w_out[0], norm_ffn[0],
                      w_up[0], w_down[0], norm_final)
    xp = x_prompt.reshape(bsz * seq, D_MODEL)
    xs = x_sample.reshape(dbsz * dseq, D_MODEL)

    (m_qvr, m_kT, m_laT, m_logf, _, _, _, m_kT32, m_vT32, _) = _inproj(
        meta_tokens.astype(F32), 1, N_META, 1, N_META, N_META, w)
    zero_state = jnp.zeros((1, GLA_HEADS, GLA_DK, GLA_DV), F32)
    _, s_meta = _gla(m_qvr, m_kT, m_laT, zero_state, w["gn_gla"], 1, N_META, N_META, N_META, False)
    kp, vp = _pad_lanes(m_kT32), _pad_lanes(m_vT32)

    tm = 512
    lead = (kp.reshape(FOX_W, LANES), vp.reshape(FOX_W, LANES))
    (p_qvr, p_kT, p_laT, p_logf, p_fq, p_fkT, p_fvT, p_kT32, p_vT32, p_sg) = _inproj(
        xp, bsz, seq, 1, tm, LANES, w, lead=lead)
    y_gla, s_p = _gla(p_qvr, p_kT, p_laT, s_meta, w["gn_gla"], bsz, seq, LANES, CHUNK, True)
    lp = _pad_lanes(jnp.broadcast_to(m_logf, (bsz, FOX_HEADS, N_META)).reshape(bsz * FOX_HEADS, N_META))
    cp, cn = _decay(lp, p_logf.reshape(bsz * FOX_HEADS, seq), N_META)
    tq = 256
    c_row, c_col = _c_layouts(cn, bsz, seq, tq)
    cp_row = cp[:FOX_HEADS].reshape(1, FOX_HEADS // 2, 2, LANES)
    o_fox = _fox(p_fq, p_fkT, p_fvT, kp, vp, c_col, c_row, cp_row, bsz, seq, tq, LANES, N_META, True)
    y_prompt = _merge_ffn(xp, y_gla, o_fox, p_sg, w, tm).reshape(bsz, seq, D_MODEL)

    (s_qvr, s_kT, s_laT, s_logf, s_fq, s_fkT, s_fvT, s_kT32, s_vT32, s_sg) = _inproj(
        xs, dbsz, dseq, dbsz, dseq, dseq, w)
    ys_gla, s_s = _gla(s_qvr, s_kT, s_laT, state_gla[0].astype(F32), w["gn_gla"],
                       dbsz, dseq, dseq, dseq, True)
    cps, cns = _decay(cache_fox_logf[0].astype(F32).reshape(dbsz * FOX_HEADS, past),
                      _pad_lanes(s_logf.reshape(dbsz * FOX_HEADS, dseq)), past)
    cs_row, cs_col = _c_layouts(cns[:, :dseq], dbsz, dseq, dseq)
    cps_row = cps.reshape(dbsz, FOX_HEADS // 2, 2, past)
    os_fox = _fox(s_fq, s_fkT, s_fvT, jnp.swapaxes(cache_fox_k[0], 2, 3).astype(F32),
                  jnp.swapaxes(cache_fox_v[0], 2, 3).astype(F32), cs_col, cs_row, cps_row,
                  dbsz, dseq, dseq, 256, past, False)
    y_sample = _merge_ffn(xs, ys_gla, os_fox, s_sg, w, dbsz * dseq).reshape(dbsz, dseq, D_MODEL)

    to_state = lambda a: jnp.swapaxes(a, 2, 3)[None]
    new_fox_logf_prompt = jnp.concatenate(
        [jnp.broadcast_to(m_logf, (bsz, FOX_HEADS, N_META)), p_logf], axis=2)[None]
    return (y_prompt, y_sample, to_state(p_kT32), to_state(p_vT32), new_fox_logf_prompt,
            s_p[None], to_state(s_kT32), to_state(s_vT32), s_logf[None], s_s[None])
```

```python
import functools

import jax
import jax.numpy as jnp
from jax import lax
from jax.experimental import pallas as pl
from jax.experimental.pallas import tpu as pltpu

F32 = jnp.float32
BF16 = jnp.bfloat16

D_MODEL = 1024
N_META = 16
EPS = 1e-6
GLA_HEADS = 4
GLA_DK = 64
GLA_DV = 128
GLA_RANK = 16
GLA_TAU = 16.0
GLA_QK = GLA_HEADS * GLA_DK
GLA_V = GLA_HEADS * GLA_DV
FOX_HEADS = 8
FOX_DH = 64
FOX_W = FOX_HEADS * FOX_DH
D_FF = 4 * D_MODEL
NEG = -1e30
CHUNK = 64
LOG2E = 1.4426950408889634

LANES = 128
VMEM_LIMIT = 56 * 1024 * 1024

_C_GQ, _C_GV, _C_GR, _C_FQ, _C_GA, _C_GB, _C_END = 0, 256, 768, 1280, 1792, 2816, 3840
_R_K, _R_LR, _R_FF, _R_FK, _R_FV, _R_END = 0, 256, 272, 288, 800, 1312


def _dot(a, b):
    return jnp.dot(a, b, preferred_element_type=F32)


def _dot_nt(a, b):
    return lax.dot_general(a, b, (((1,), (1,)), ((), ())), preferred_element_type=F32)


def _split3(x):
    hi = x.astype(BF16)
    r = x - hi.astype(F32)
    mid = r.astype(BF16)
    lo = (r - mid.astype(F32)).astype(BF16)
    return hi, mid, lo


def _dot3(x, m):
    hi, mid, lo = _split3(x)
    return _dot(hi, m) + _dot(mid, m) + _dot(lo, m)


def _log_sigmoid(x):
    return jnp.minimum(x, 0.0) - jnp.log1p(jnp.exp(-jnp.abs(x)))


def _const_spec(shape):
    nd = len(shape)
    return pl.BlockSpec(shape, lambda *_: (0,) * nd, pipeline_mode=pl.Buffered(1))


def _inproj_kernel(x_ref, gn_ref, wnn_ref, wt_ref, wg_ref, bg_ref, bf_ref, *rest, nb, tmb, tw, nt, row_off):
    if row_off:
        lead_k_ref, lead_v_ref = rest[:2]
        rest = rest[2:]
    (qvr_ref, kT_ref, laT_ref, logf_ref, fq_ref, fkT_ref, fvT_ref, k32_ref, v32_ref, sg_ref) = rest[:10]
    carries = rest[10:]
    tm = nb * tmb

    def project():
        x = x_ref[...]
        ms = jnp.mean(x * x, axis=-1, keepdims=True)
        hn = ((x * lax.rsqrt(ms + EPS)) * gn_ref[...]).astype(BF16)

        qvr_ref[...] = _dot(hn, wnn_ref[:, _C_GQ:_C_FQ]).astype(BF16)
        fq_ref[...] = _dot(hn, wnn_ref[:, _C_FQ:_C_GA]).astype(BF16)
        sg_ref[...] = jax.nn.sigmoid(_dot(hn, wnn_ref[:, _C_GA:_C_END])).astype(BF16)

        zT = _dot_nt(wt_ref[...], hn)
        kT = zT[_R_K:_R_LR].astype(BF16)
        glrT = zT[_R_LR:_R_FF].astype(BF16)
        laT = _log_sigmoid(_dot(wg_ref[...], glrT) + bg_ref[...]) * (1.0 / GLA_TAU)
        logf = _log_sigmoid(zT[_R_FF:_R_FF + FOX_HEADS] + bf_ref[...])
        zk, zv = zT[_R_FK:_R_FV], zT[_R_FV:_R_END]
        for bb in range(nb):
            r0 = bb * tmb
            logf_ref[bb] = logf[:, r0:r0 + tmb]
            fkT_ref[bb] = zk[:, r0:r0 + tmb].astype(BF16)
            fvT_ref[bb] = zv[:, r0:r0 + tmb].astype(BF16)
            for j in range(tmb // tw):
                kT_ref[bb, j] = kT[:, r0 + j * tw:r0 + (j + 1) * tw]
                laT_ref[bb, j] = laT[:, r0 + j * tw:r0 + (j + 1) * tw]
            if not row_off:
                for h in range(FOX_HEADS):
                    k32_ref[bb, h] = zk[h * FOX_DH:(h + 1) * FOX_DH, r0:r0 + tmb]
                    v32_ref[bb, h] = zv[h * FOX_DH:(h + 1) * FOX_DH, r0:r0 + tmb]
        if row_off:
            lane = lax.broadcasted_iota(jnp.int32, (FOX_W, LANES), 1)
            for z, carry, o_ref in ((zk, carries[0], k32_ref), (zv, carries[1], v32_ref)):
                rolled = pltpu.roll(z, row_off, axis=1)
                first = jnp.where(lane < row_off, carry[...], rolled[:, 0:LANES])
                carry[...] = rolled[:, 0:LANES]
                for h in range(FOX_HEADS):
                    hr = slice(h * FOX_DH, (h + 1) * FOX_DH)
                    o_ref[0, h, :, 0:LANES] = first[hr]
                    o_ref[0, h, :, LANES:tm] = rolled[hr, LANES:tm]

    if not row_off:
        project()
        return

    i = pl.program_id(1)

    @pl.when(i == 0)
    def _():
        carries[0][...] = lead_k_ref[...]
        carries[1][...] = lead_v_ref[...]

    pl.when(i < nt)(project)

    @pl.when(i == nt)
    def _():
        for carry, o_ref in ((carries[0], k32_ref), (carries[1], v32_ref)):
            for h in range(FOX_HEADS):
                o_ref[0, h, :, 0:LANES] = carry[h * FOX_DH:(h + 1) * FOX_DH, :]


def _inproj(x2d, nbatch, n, nb, tmb, tw, w, lead=None):
    tm = nb * tmb
    ngrp, nt = nbatch // nb, n // tmb
    t_total = nbatch * n
    row_off = N_META if lead is not None else 0
    steps = nt + 1 if row_off else nt
    blk = (lambda i: jnp.minimum(i, nt - 1)) if row_off else (lambda i: i)
    tok = lambda width: pl.BlockSpec((tm, width), lambda g, i: (g * nt + blk(i), 0))
    out_shape = (
        jax.ShapeDtypeStruct((t_total, _C_FQ), BF16),
        jax.ShapeDtypeStruct((nbatch, n // tw, GLA_QK, tw), BF16),
        jax.ShapeDtypeStruct((nbatch, n // tw, GLA_QK, tw), F32),
        jax.ShapeDtypeStruct((nbatch, FOX_HEADS, n), F32),
        jax.ShapeDtypeStruct((t_total, FOX_W), BF16),
        jax.ShapeDtypeStruct((nbatch, FOX_W, n), BF16),
        jax.ShapeDtypeStruct((nbatch, FOX_W, n), BF16),
        jax.ShapeDtypeStruct((nbatch, FOX_HEADS, FOX_DH, row_off + n), F32),
        jax.ShapeDtypeStruct((nbatch, FOX_HEADS, FOX_DH, row_off + n), F32),
        jax.ShapeDtypeStruct((t_total, 2 * D_MODEL), BF16),
    )
    tblk = pl.BlockSpec((nb, tmb // tw, GLA_QK, tw), lambda g, i: (g, blk(i), 0, 0))
    lblk = pl.BlockSpec((nb, FOX_W, tmb), lambda g, i: (g, 0, blk(i)))
    sblk = pl.BlockSpec((nb, FOX_HEADS, FOX_DH, tmb), lambda g, i: (g, 0, 0, i))
    out_specs = (
        tok(_C_FQ), tblk, tblk,
        pl.BlockSpec((nb, FOX_HEADS, tmb), lambda g, i: (g, 0, blk(i))),
        tok(FOX_W), lblk, lblk, sblk, sblk, tok(2 * D_MODEL),
    )
    in_specs = [
        tok(D_MODEL),
        _const_spec((1, D_MODEL)),
        _const_spec((D_MODEL, _C_END)),
        _const_spec((_R_END, D_MODEL)),
        _const_spec((GLA_QK, GLA_RANK)),
        _const_spec((GLA_QK, 1)),
        _const_spec((FOX_HEADS, 1)),
    ]
    args = [x2d, w["gn_mix"], w["w_nn"], w["w_t"], w["w_gate_t"], w["b_gate_col"], w["b_f_col"]]
    scratch = []
    if row_off:
        assert nb == 1 and tmb % LANES == 0
        in_specs += [_const_spec((FOX_W, LANES))] * 2
        args += list(lead)
        scratch = [pltpu.VMEM((FOX_W, LANES), F32)] * 2
    return pl.pallas_call(
        functools.partial(_inproj_kernel, nb=nb, tmb=tmb, tw=tw, nt=nt, row_off=row_off),
        grid=(ngrp, steps), in_specs=in_specs, out_specs=out_specs, out_shape=out_shape,
        scratch_shapes=scratch,
        compiler_params=pltpu.CompilerParams(
            dimension_semantics=("arbitrary", "arbitrary"), vmem_limit_bytes=VMEM_LIMIT),
        name="inproj",
    )(*args)


def _cumsum_lanes(x, carry):
    n = x.shape[1]
    r = lax.broadcasted_iota(jnp.int32, (LANES, LANES), 0)
    c = lax.broadcasted_iota(jnp.int32, (LANES, LANES), 1)
    upper = jnp.where(r <= c, 1.0, 0.0).astype(BF16)
    outs = []
    for j in range(n // LANES):
        cj = _dot3(x[:, j * LANES:(j + 1) * LANES], upper) + carry
        outs.append(cj)
        carry = cj[:, LANES - 1:LANES]
    return jnp.concatenate(outs, axis=1), carry


def _decay_kernel(lp_ref, ln_ref, cp_ref, cn_ref, *, p_valid):
    lp = lp_ref[...]
    lane = lax.broadcasted_iota(jnp.int32, lp.shape, 1)
    lp = jnp.where(lane < p_valid, lp, 0.0)
    zero = jnp.zeros((lp.shape[0], 1), F32)
    cp, tot = _cumsum_lanes(lp, zero)
    cp_ref[...] = cp * LOG2E
    cn, _ = _cumsum_lanes(ln_ref[...], tot)
    cn_ref[...] = cn * LOG2E


def _decay(logf_p, logf_n, p_valid):
    r, pp = logf_p.shape
    npad = logf_n.shape[1]
    return pl.pallas_call(
        functools.partial(_decay_kernel, p_valid=p_valid),
        out_shape=(jax.ShapeDtypeStruct((r, pp), F32), jax.ShapeDtypeStruct((r, npad), F32)),
        compiler_params=pltpu.CompilerParams(vmem_limit_bytes=VMEM_LIMIT),
        name="decay",
    )(logf_p, logf_n)


def _gla_kernel(qvr_ref, kT_ref, laT_ref, s0_ref, gn_ref, y_ref, sout_ref, *, n, tw, chunk, need_y):
    nblk = n // tw
    ncb = tw // chunk

    sr = lax.broadcasted_iota(jnp.int32, (tw, LANES), 0)
    tr = lax.broadcasted_iota(jnp.int32, (tw, tw), 0)
    tc = lax.broadcasted_iota(jnp.int32, (tw, tw), 1)
    m_cum = ((tr // chunk) == (tc // chunk)) & (tr <= tc)
    m_all = jnp.concatenate(
        [jnp.where((sr // chunk) == ci, 1.0, 0.0) for ci in range(ncb)] + [jnp.where(m_cum, 1.0, 0.0)],
        axis=1).astype(BF16)
    stack_k = tw % LANES == 0
    m_rhs = jnp.concatenate([m_all] * 3, axis=0) if stack_k else m_all
    lane_chunk = lax.broadcasted_iota(jnp.int32, (GLA_QK, tw), 1) // chunk
    zero_blk = jnp.zeros((GLA_DK, GLA_DV), BF16)
    gn = gn_ref[...]

    def group(jg, state):
        chunks = []
        for t in range(nbg):
            jb = jg * nbg + t
            hi, mid, lo = _split3(laT_ref[0, jb])
            if stack_k:
                g = _dot(jnp.concatenate([hi, mid, lo], axis=1), m_rhs)
            else:
                g = _dot(hi, m_rhs) + _dot(mid, m_rhs) + _dot(lo, m_rhs)
            tot = [g[:, ci * LANES:(ci + 1) * LANES] for ci in range(ncb)]
            g_cum = g[:, ncb * LANES:ncb * LANES + tw]
            g_tot = tot[0][:, 0:tw] if tw <= LANES else jnp.concatenate([tot[0]] * (tw // LANES), axis=1)
            for ci in range(1, ncb):
                g_tot = jnp.where(lane_chunk == ci, tot[ci][:, 0:tw], g_tot)
            kdec = (kT_ref[0, jb].astype(F32) * jnp.exp(g_tot - g_cum)).astype(BF16)
            for ci in range(ncb):
                rows = pl.ds(pl.multiple_of(jb * tw, tw) + ci * chunk, chunk)
                v = qvr_ref[rows, _C_GV:_C_GR]
                us = [_dot(kdec[h * GLA_DK:(h + 1) * GLA_DK, ci * chunk:(ci + 1) * chunk],
                           v[:, h * GLA_DV:(h + 1) * GLA_DV]) for h in range(GLA_HEADS)]
                chunks.append((rows, jnp.exp(tot[ci]), us))
        s_bds = []
        for rows, a, us in chunks:
            state = tuple(a[h * GLA_DK:(h + 1) * GLA_DK] * state[h] + us[h] for h in range(GLA_HEADS))
            if need_y:
                s_bds.append(jnp.concatenate(
                    [jnp.concatenate([state[h].astype(BF16) if hc == h else zero_blk
                                      for hc in range(GLA_HEADS)], axis=1) for h in range(GLA_HEADS)], axis=0))
        if need_y:
            for (rows, _, _), s_bd in zip(chunks, s_bds):
                o = _dot(qvr_ref[rows, _C_GQ:_C_GV], s_bd)
                gate = jax.nn.silu(qvr_ref[rows, _C_GR:_C_FQ].astype(F32))
                parts = []
                for h in range(GLA_HEADS):
                    oh = o[:, h * GLA_DV:(h + 1) * GLA_DV]
                    msq = jnp.mean(oh * oh, axis=-1, keepdims=True)
                    parts.append(oh * lax.rsqrt(msq + EPS))
                on = jnp.concatenate(parts, axis=1) * gn
                y_ref[rows, :] = (on * gate).astype(BF16)
        return state

    nbg = 4 if nblk % 4 == 0 else 1
    state = tuple(s0_ref[0, h] for h in range(GLA_HEADS))
    state = lax.fori_loop(0, nblk // nbg, group, state)
    for h in range(GLA_HEADS):
        sout_ref[0, h] = state[h]


def _gla(qvr, kT, laT, s0, gn_tiled, nbatch, n, tw, chunk, need_y):
    s0_map = (lambda b: (0, 0, 0, 0)) if s0.shape[0] == 1 else (lambda b: (b, 0, 0, 0))
    kern = functools.partial(_gla_kernel, n=n, tw=tw, chunk=chunk, need_y=need_y)
    if not need_y:
        def kern(qvr_ref, kT_ref, laT_ref, s0_ref, gn_ref, sout_ref):
            _gla_kernel(qvr_ref, kT_ref, laT_ref, s0_ref, gn_ref, None, sout_ref,
                        n=n, tw=tw, chunk=chunk, need_y=False)
    state_shape = (GLA_HEADS, GLA_DK, GLA_DV)
    out_shape = [jax.ShapeDtypeStruct((nbatch,) + state_shape, F32)]
    out_specs = [pl.BlockSpec((1,) + state_shape, lambda b: (b, 0, 0, 0))]
    if need_y:
        out_shape = [jax.ShapeDtypeStruct((nbatch * n, GLA_V), BF16)] + out_shape
        out_specs = [pl.BlockSpec((n, GLA_V), lambda b: (b, 0))] + out_specs
    res = pl.pallas_call(
        kern, grid=(nbatch,),
        in_specs=[
            pl.BlockSpec((n, _C_FQ), lambda b: (b, 0)),
            pl.BlockSpec((1, n // tw, GLA_QK, tw), lambda b: (b, 0, 0, 0)),
            pl.BlockSpec((1, n // tw, GLA_QK, tw), lambda b: (b, 0, 0, 0)),
            pl.BlockSpec((1,) + state_shape, s0_map),
            _const_spec((1, GLA_V)),
        ],
        out_specs=out_specs, out_shape=out_shape,
        compiler_params=pltpu.CompilerParams(
            dimension_semantics=("arbitrary",), vmem_limit_bytes=VMEM_LIMIT),
        name="gla",
    )(qvr, kT, laT, s0, gn_tiled)
    return res if need_y else (None, res[0])


def _fold_lanes(x, op):
    w = x.shape[1]
    if w % LANES == 0:
        f = x[:, 0:LANES]
        for g in range(1, w // LANES):
            f = op(f, x[:, g * LANES:(g + 1) * LANES])
        return f
    assert op is jnp.add
    lane = lax.broadcasted_iota(jnp.int32, (x.shape[0], LANES), 1)
    return jnp.where(lane == 0, jnp.sum(x, axis=-1, keepdims=True), 0.0)


def _tile_lanes(x, w):
    return x[:, 0:w] if w < LANES else jnp.concatenate([x] * (w // LANES), axis=1)


def _fox_kernel(q_ref, kn_ref, vn_ref, kp_ref, vp_ref, crow_ref, cprow_ref, o_ref, ka_scr, va_scr, *scr,
                n, tq, tkp, p_pad, p_valid, pv_t):
    nq = n // tq
    rb = min(tq, 64)
    nbuf = len(scr) // 6
    s_scr, p_scr, sh_scr = (scr[i * 2 * nbuf:(i + 1) * 2 * nbuf] for i in range(3))
    nparts = 3
    ones_rows = 16

    def c_block(c_row):
        parts = _split3(-c_row)
        r = lax.broadcasted_iota(jnp.int32, (FOX_DH, c_row.shape[1]), 0)
        blk = jnp.zeros((FOX_DH, c_row.shape[1]), F32)
        for i, part in enumerate(parts):
            blk = jnp.where(r == i, part.astype(F32), blk)
        return blk.astype(BF16)

    for hh in range(2):
        for k_src, c_src, c0, w in ((kp_ref[0, hh].astype(BF16), cprow_ref[0, 0, hh:hh + 1, :], 0, p_pad),
                                    (kn_ref[0, hh * FOX_DH:(hh + 1) * FOX_DH, :], crow_ref[0, 0, hh:hh + 1, :], p_pad, n)):
            halves = (k_src, c_block(c_src)) if hh == 0 else (c_block(c_src), k_src)
            ka_scr[hh, :, c0:c0 + w] = jnp.concatenate(halves, axis=0)
        va_scr[hh, 0:FOX_DH, 0:p_pad] = vp_ref[0, hh].astype(BF16)
        va_scr[hh, 0:FOX_DH, p_pad:p_pad + n] = vn_ref[0, hh * FOX_DH:(hh + 1) * FOX_DH, :]
        va_scr[hh, FOX_DH:FOX_DH + ones_rows, :] = jnp.ones((ones_rows, p_pad + n), BF16)

    lane_q = lax.broadcasted_iota(jnp.int32, (tq, LANES), 1)
    eye = jnp.where(lax.broadcasted_iota(jnp.int32, (tq, tq), 0) == lax.broadcasted_iota(jnp.int32, (tq, tq), 1),
                    1.0, 0.0).astype(BF16)

    def block_chunks(qi):
        return [(c0, tkp) for c0 in range(0, p_pad, tkp)] + [(p_pad + j * tq, tq) for j in range(qi + 1)]

    def pass1_items(qi):
        rows = slice(qi * tq, (qi + 1) * tq)
        chunks = block_chunks(qi)
        qp = q_ref[rows, :]
        state = {}

        def start():
            c_parts = [part for hh in range(2) for part in _split3(crow_ref[0, 0, hh:hh + 1, rows])]
            c_rows = jnp.concatenate(c_parts + [jnp.zeros((ones_rows - 2 * nparts, tq), BF16)], axis=0)
            state["c_cols"] = _dot_nt(eye, c_rows)
            for hh in range(2):
                c_lane0 = FOX_DH if hh == 0 else 0
                own = (lane_q < FOX_DH) if hh == 0 else (lane_q >= FOX_DH)
                ones_pat = jnp.where((lane_q >= c_lane0) & (lane_q < c_lane0 + nparts), 1.0, 0.0).astype(BF16)
                state["q", hh] = jnp.where(own, qp, ones_pat)
                state["mx", hh] = None

        def chunk(hh, ci):
            c0, cw = chunks[ci]
            s = _dot(state["q", hh], ka_scr[hh, :, c0:c0 + cw])
            if c0 < p_pad and c0 + cw > p_valid:
                col = lax.broadcasted_iota(jnp.int32, s.shape, 1) + c0
                s = jnp.where(col < p_valid, s, NEG)
            if ci == len(chunks) - 1:
                r = lax.broadcasted_iota(jnp.int32, s.shape, 0)
                c = lax.broadcasted_iota(jnp.int32, s.shape, 1)
                s = jnp.where(c <= r, s, NEG)
            s_scr[buf(qi, hh)][:, c0:c0 + cw] = s
            f = _fold_lanes(s, jnp.maximum) if cw % LANES == 0 else jnp.max(s, axis=-1, keepdims=True)
            mx = state["mx", hh]
            if mx is None or mx.shape == f.shape:
                mx = f if mx is None else jnp.maximum(mx, f)
            else:
                mx = jnp.maximum(jnp.max(mx, axis=-1, keepdims=True), jnp.max(f, axis=-1, keepdims=True))
            state["mx", hh] = mx

        def finish(hh):
            cq = sum(state["c_cols"][:, hh * nparts + i:hh * nparts + i + 1] for i in range(nparts))
            m = jnp.max(state["mx", hh], axis=-1, keepdims=True) + cq
            sh_scr[buf(qi, hh)][...] = jnp.broadcast_to(m - cq, (tq, LANES))

        items = [start]
        for ci in range(len(chunks)):
            items += [functools.partial(chunk, hh, ci) for hh in range(2)]
        return items + [functools.partial(finish, hh) for hh in range(2)]

    def pass2_items(qi):
        def item(r0, hh, c0, cw):
            rr = slice(r0, r0 + rb)
            sh = sh_scr[buf(qi, hh)][rr, :]
            p = jnp.exp2(s_scr[buf(qi, hh)][rr, c0:c0 + cw] - _tile_lanes(sh, cw))
            p_scr[buf(qi, hh)][rr, c0:c0 + cw] = p.astype(BF16)
        return [functools.partial(item, r0, hh, c0, cw)
                for r0 in range(0, tq, rb) for hh in range(2) for c0, cw in block_chunks(qi)]

    def pv_store(qi):
        w_tot = p_pad + (qi + 1) * tq
        outs = []
        for hh in range(2):
            p_all, v_aug = p_scr[buf(qi, hh)][:, 0:w_tot], va_scr[hh, :, 0:w_tot]
            if pv_t:
                acc = _dot_nt(v_aug, p_all)
                outs.append(acc[0:FOX_DH] * (1.0 / acc[FOX_DH:FOX_DH + 1]))
            else:
                acc = _dot_nt(p_all, v_aug)
                outs.append(acc[:, 0:FOX_DH] * (1.0 / acc[:, FOX_DH:FOX_DH + 1]))
        o = jnp.concatenate(outs, axis=0).T if pv_t else jnp.concatenate(outs, axis=1)
        o_ref[qi * tq:(qi + 1) * tq, :] = o.astype(BF16)

    buf = lambda qi, hh: (qi % nbuf) * 2 + hh
    for item in pass1_items(0):
        item()
    for qi in range(nq):
        a = pass1_items(qi + 1) if qi + 1 < nq else []
        b = pass2_items(qi)
        ia = ib = 0
        while ia < len(a) or ib < len(b):
            if ib >= len(b) or (ia < len(a) and ia * len(b) <= ib * len(a)):
                a[ia]()
                ia += 1
            else:
                b[ib]()
                ib += 1
        pv_store(qi)


def _fox(q, kn, vn, kp, vp, c_row, cp_row, nbatch, n, tq, tkp, p_valid, pv_t):
    nq = n // tq
    p_pad = kp.shape[3]
    pmap4 = (lambda b, hp: (0, hp, 0, 0)) if kp.shape[0] == 1 else (lambda b, hp: (b, hp, 0, 0))
    cmap4 = (lambda b, hp: (0, hp, 0, 0)) if cp_row.shape[0] == 1 else (lambda b, hp: (b, hp, 0, 0))
    tok = pl.BlockSpec((n, LANES), lambda b, hp: (b, hp))
    tlanes = pl.BlockSpec((1, LANES, n), lambda b, hp: (b, hp, 0))
    nbuf = min(nq, 2)
    return pl.pallas_call(
        functools.partial(_fox_kernel, n=n, tq=tq, tkp=tkp, p_pad=p_pad, p_valid=p_valid, pv_t=pv_t),
        grid=(nbatch, FOX_HEADS // 2),
        in_specs=[
            tok, tlanes, tlanes,
            pl.BlockSpec((1, 2, FOX_DH, p_pad), pmap4),
            pl.BlockSpec((1, 2, FOX_DH, p_pad), pmap4),
            pl.BlockSpec((1, 1, 2, n), lambda b, hp: (b, hp, 0, 0)),
            pl.BlockSpec((1, 1, 2, p_pad), cmap4),
        ],
        out_specs=tok,
        out_shape=jax.ShapeDtypeStruct((nbatch * n, FOX_W), BF16),
        scratch_shapes=([pltpu.VMEM((2, LANES, p_pad + n), BF16), pltpu.VMEM((2, FOX_DH + 16, p_pad + n), BF16)]
                        + [pltpu.VMEM((tq, p_pad + n), F32)] * (2 * nbuf)
                        + [pltpu.VMEM((tq, p_pad + n), BF16)] * (2 * nbuf)
                        + [pltpu.VMEM((tq, LANES), F32)] * (2 * nbuf)),
        compiler_params=pltpu.CompilerParams(
            dimension_semantics=("arbitrary", "arbitrary"), vmem_limit_bytes=VMEM_LIMIT),
        name="fox",
    )(q, kn, vn, kp, vp, c_row, cp_row)


def _merge_ffn_kernel(x_ref, ygla_ref, ofox_ref, sg_ref, wbg_ref, wbf_ref, wout_ref, gffn_ref,
                      wup_ref, wdown_ref, gfin_ref, y_ref, *, ff_chunk):
    ya = _dot(ygla_ref[...], wbg_ref[...])
    yb = _dot(ofox_ref[...], wbf_ref[...])
    m = sg_ref[:, 0:D_MODEL].astype(F32) * ya + sg_ref[:, D_MODEL:2 * D_MODEL].astype(F32) * yb
    h = x_ref[...] + _dot(m.astype(BF16), wout_ref[...])
    ms = jnp.mean(h * h, axis=-1, keepdims=True)
    hn = ((h * lax.rsqrt(ms + EPS)) * gffn_ref[...]).astype(BF16)
    for c in range(D_FF // ff_chunk):
        u = _dot(hn, wup_ref[:, c * ff_chunk:(c + 1) * ff_chunk])
        r = jnp.square(jnp.maximum(u, 0.0)).astype(BF16)
        h = h + _dot(r, wdown_ref[c * ff_chunk:(c + 1) * ff_chunk, :])
    ms = jnp.mean(h * h, axis=-1, keepdims=True)
    y_ref[...] = (h * lax.rsqrt(ms + EPS)) * gfin_ref[...]


def _merge_ffn(x2d, ygla, ofox, sg, w, tm):
    t_total = x2d.shape[0]
    tok = lambda width: pl.BlockSpec((tm, width), lambda i: (i, 0))
    return pl.pallas_call(
        functools.partial(_merge_ffn_kernel, ff_chunk=1024),
        grid=(t_total // tm,),
        in_specs=[
            tok(D_MODEL), tok(GLA_V), tok(FOX_W), tok(2 * D_MODEL),
            _const_spec((GLA_V, D_MODEL)), _const_spec((FOX_W, D_MODEL)),
            _const_spec((D_MODEL, D_MODEL)), _const_spec((1, D_MODEL)),
            _const_spec((D_MODEL, D_FF)), _const_spec((D_FF, D_MODEL)), _const_spec((1, D_MODEL)),
        ],
        out_specs=tok(D_MODEL),
        out_shape=jax.ShapeDtypeStruct((t_total, D_MODEL), F32),
        compiler_params=pltpu.CompilerParams(
            dimension_semantics=("arbitrary",), vmem_limit_bytes=VMEM_LIMIT),
        name="merge_ffn",
    )(x2d, ygla, ofox, sg, w["w_bg"], w["w_bf"], w["w_out"], w["gn_ffn"], w["w_up"], w["w_down"],
      w["gn_final"])


def _prep_weights(norm_mix, w_in, w_gla_gate, b_gla_gate, g_gla_norm, b_fox_forget, w_branch_gla,
                  w_branch_fox, w_out, norm_ffn, w_up, w_down, norm_final):
    sizes = (GLA_QK, GLA_QK, GLA_V, GLA_V, GLA_RANK, FOX_W, FOX_W, FOX_W, FOX_HEADS, D_MODEL, D_MODEL)
    offs = [0]
    for s in sizes:
        offs.append(offs[-1] + s)
    col = lambda i: w_in[:, offs[i]:offs[i + 1]]
    gq, gk, gv, gr, glr, fq, fk, fv, ff, ga, gb = (col(i) for i in range(11))
    w_nn = jnp.concatenate([gq * (GLA_DK ** -0.5), gv, gr, fq * (FOX_DH ** -0.5 * LOG2E), ga, gb],
                           axis=1).astype(BF16)
    pad = jnp.zeros((D_MODEL, _R_FK - _R_FF - FOX_HEADS), F32)
    w_t = jnp.concatenate([gk, glr, ff, pad, fk, fv], axis=1).T.astype(BF16)
    return dict(
        gn_mix=norm_mix.reshape(1, D_MODEL), w_nn=w_nn, w_t=w_t,
        w_gate_t=w_gla_gate.T.astype(BF16), b_gate_col=b_gla_gate.reshape(GLA_QK, 1),
        b_f_col=b_fox_forget.reshape(FOX_HEADS, 1),
        gn_gla=jnp.tile(g_gla_norm.reshape(1, GLA_DV), (1, GLA_HEADS)),
        w_bg=w_branch_gla.astype(BF16), w_bf=w_branch_fox.astype(BF16), w_out=w_out.astype(BF16),
        gn_ffn=norm_ffn.reshape(1, D_MODEL), w_up=w_up.astype(BF16), w_down=w_down.astype(BF16),
        gn_final=norm_final.reshape(1, D_MODEL),
    )


def _pad_lanes(x):
    n = x.shape[-1]
    npad = -(-n // LANES) * LANES
    return x if npad == n else jnp.pad(x, [(0, 0)] * (x.ndim - 1) + [(0, npad - n)])


def kernel(x_prompt, x_sample, cache_fox_k, cache_fox_v, cache_fox_logf, state_gla, meta_tokens,
           norm_mix, w_in, w_gla_gate, b_gla_gate, g_gla_norm, b_fox_forget, w_branch_gla,
           w_branch_fox, w_out, norm_ffn, w_up, w_down, norm_final):
    bsz, seq, _ = x_prompt.shape
    dbsz, dseq, _ = x_sample.shape
    past = cache_fox_k.shape[3]
    w = _prep_weights(norm_mix[0], w_in[0], w_gla_gate[0], b_gla_gate[0], g_gla_norm[0],
                      b_fox_forget[0], w_branch_gla[0], w_branch_fox[0], w_out[0], norm_ffn[0],
                      w_up[0], w_down[0], norm_final)
    xp = x_prompt.reshape(bsz * seq, D_MODEL)
    xs = x_sample.reshape(dbsz * dseq, D_MODEL)

    (m_qvr, m_kT, m_laT, m_logf, _, _, _, m_kT32, m_vT32, _) = _inproj(
        meta_tokens.astype(F32), 1, N_META, 1, N_META, N_META, w)
    zero_state = jnp.zeros((1, GLA_HEADS, GLA_DK, GLA_DV), F32)
    _, s_meta = _gla(m_qvr, m_kT, m_laT, zero_state, w["gn_gla"], 1, N_META, N_META, N_META, False)
    kp, vp = _pad_lanes(m_kT32), _pad_lanes(m_vT32)

    tm = 512
    lead = (kp.reshape(FOX_W, LANES), vp.reshape(FOX_W, LANES))
    (p_qvr, p_kT, p_laT, p_logf, p_fq, p_fkT, p_fvT, p_kT32, p_vT32, p_sg) = _inproj(
        xp, bsz, seq, 1, tm, LANES, w, lead=lead)
    y_gla, s_p = _gla(p_qvr, p_kT, p_laT, s_meta, w["gn_gla"], bsz, seq, LANES, CHUNK, True)
    lp = _pad_lanes(jnp.broadcast_to(m_logf, (bsz, FOX_HEADS, N_META)).reshape(bsz * FOX_HEADS, N_META))
    cp, cn = _decay(lp, p_logf.reshape(bsz * FOX_HEADS, seq), N_META)
    tq = 256
    c_row = cn.reshape(bsz, FOX_HEADS // 2, 2, seq)
    cp_row = cp[:FOX_HEADS].reshape(1, FOX_HEADS // 2, 2, LANES)
    o_fox = _fox(p_fq, p_fkT, p_fvT, kp, vp, c_row, cp_row, bsz, seq, tq, LANES, N_META, True)
    y_prompt = _merge_ffn(xp, y_gla, o_fox, p_sg, w, tm).reshape(bsz, seq, D_MODEL)

    (s_qvr, s_kT, s_laT, s_logf, s_fq, s_fkT, s_fvT, s_kT32, s_vT32, s_sg) = _inproj(
        xs, dbsz, dseq, dbsz, dseq, dseq, w)
    ys_gla, s_s = _gla(s_qvr, s_kT, s_laT, state_gla[0].astype(F32), w["gn_gla"],
                       dbsz, dseq, dseq, dseq, True)
    cps, cns = _decay(cache_fox_logf[0].astype(F32).reshape(dbsz * FOX_HEADS, past),
                      _pad_lanes(s_logf.reshape(dbsz * FOX_HEADS, dseq)), past)
    cs_row = cns[:, :dseq].reshape(dbsz, FOX_HEADS // 2, 2, dseq)
    cps_row = cps.reshape(dbsz, FOX_HEADS // 2, 2, past)
    os_fox = _fox(s_fq, s_fkT, s_fvT, jnp.swapaxes(cache_fox_k[0], 2, 3).astype(F32),
                  jnp.swapaxes(cache_fox_v[0], 2, 3).astype(F32), cs_row, cps_row,
                  dbsz, dseq, dseq, 256, past, False)
    y_sample = _merge_ffn(xs, ys_gla, os_fox, s_sg, w, dbsz * dseq).reshape(dbsz, dseq, D_MODEL)

    to_state = lambda a: jnp.swapaxes(a, 2, 3)[None]
    new_fox_logf_prompt = jnp.concatenate(
        [jnp.broadcast_to(m_logf, (bsz, FOX_HEADS, N_META)), p_logf], axis=2)[None]
    return (y_prompt, y_sample, to_state(p_kT32), to_state(p_vT32), new_fox_logf_prompt,
            s_p[None], to_state(s_kT32), to_state(s_vT32), s_logf[None], s_s[None])
```

```python
import functools

import jax
import jax.numpy as jnp
from jax import lax
from jax.experimental import pallas as pl
from jax.experimental.pallas import tpu as pltpu

F32 = jnp.float32
BF16 = jnp.bfloat16

D_MODEL = 1024
N_META = 16
EPS = 1e-6
GLA_HEADS = 4
GLA_DK = 64
GLA_DV = 128
GLA_RANK = 16
GLA_TAU = 16.0
GLA_QK = GLA_HEADS * GLA_DK
GLA_V = GLA_HEADS * GLA_DV
FOX_HEADS = 8
FOX_DH = 64
FOX_W = FOX_HEADS * FOX_DH
D_FF = 4 * D_MODEL
NEG = -1e30
CHUNK = 64
LOG2E = 1.4426950408889634

LANES = 128
VMEM_LIMIT = 56 * 1024 * 1024

_C_GQ, _C_GV, _C_GR, _C_FQ, _C_GA, _C_GB, _C_END = 0, 256, 768, 1280, 1792, 2816, 3840
_R_K, _R_LR, _R_FF, _R_FK, _R_FV, _R_END = 0, 256, 272, 288, 800, 1312


def _dot(a, b):
    return jnp.dot(a, b, preferred_element_type=F32)


def _dot_nt(a, b):
    return lax.dot_general(a, b, (((1,), (1,)), ((), ())), preferred_element_type=F32)


def _split3(x):
    hi = x.astype(BF16)
    r = x - hi.astype(F32)
    mid = r.astype(BF16)
    lo = (r - mid.astype(F32)).astype(BF16)
    return hi, mid, lo


def _dot3(x, m):
    hi, mid, lo = _split3(x)
    return _dot(hi, m) + _dot(mid, m) + _dot(lo, m)


def _log_sigmoid(x):
    return jnp.minimum(x, 0.0) - jnp.log1p(jnp.exp(-jnp.abs(x)))


def _const_spec(shape):
    nd = len(shape)
    return pl.BlockSpec(shape, lambda *_: (0,) * nd, pipeline_mode=pl.Buffered(1))


def _inproj_kernel(x_ref, gn_ref, wnn_ref, wt_ref, wg_ref, bg_ref, bf_ref, *rest, nb, tmb, tw, nt, row_off):
    if row_off:
        lead_k_ref, lead_v_ref = rest[:2]
        rest = rest[2:]
    (qvr_ref, kT_ref, laT_ref, logf_ref, fq_ref, fkT_ref, fvT_ref, k32_ref, v32_ref, sg_ref) = rest[:10]
    carries = rest[10:]
    tm = nb * tmb

    def project():
        x = x_ref[...]
        ms = jnp.mean(x * x, axis=-1, keepdims=True)
        hn = ((x * lax.rsqrt(ms + EPS)) * gn_ref[...]).astype(BF16)

        qvr_ref[...] = _dot(hn, wnn_ref[:, _C_GQ:_C_FQ]).astype(BF16)
        fq_ref[...] = _dot(hn, wnn_ref[:, _C_FQ:_C_GA]).astype(BF16)
        sg_ref[...] = jax.nn.sigmoid(_dot(hn, wnn_ref[:, _C_GA:_C_END])).astype(BF16)

        zT = _dot_nt(wt_ref[...], hn)
        kT = zT[_R_K:_R_LR].astype(BF16)
        glrT = zT[_R_LR:_R_FF].astype(BF16)
        laT = _log_sigmoid(_dot(wg_ref[...], glrT) + bg_ref[...]) * (1.0 / GLA_TAU)
        logf = _log_sigmoid(zT[_R_FF:_R_FF + FOX_HEADS] + bf_ref[...])
        zk, zv = zT[_R_FK:_R_FV], zT[_R_FV:_R_END]
        for bb in range(nb):
            r0 = bb * tmb
            logf_ref[bb] = logf[:, r0:r0 + tmb]
            fkT_ref[bb] = zk[:, r0:r0 + tmb].astype(BF16)
            fvT_ref[bb] = zv[:, r0:r0 + tmb].astype(BF16)
            for j in range(tmb // tw):
                kT_ref[bb, j] = kT[:, r0 + j * tw:r0 + (j + 1) * tw]
                laT_ref[bb, j] = laT[:, r0 + j * tw:r0 + (j + 1) * tw]
            if not row_off:
                for h in range(FOX_HEADS):
                    k32_ref[bb, h] = zk[h * FOX_DH:(h + 1) * FOX_DH, r0:r0 + tmb]
                    v32_ref[bb, h] = zv[h * FOX_DH:(h + 1) * FOX_DH, r0:r0 + tmb]
        if row_off:
            lane = lax.broadcasted_iota(jnp.int32, (FOX_W, LANES), 1)
            for z, carry, o_ref in ((zk, carries[0], k32_ref), (zv, carries[1], v32_ref)):
                rolled = pltpu.roll(z, row_off, axis=1)
                first = jnp.where(lane < row_off, carry[...], rolled[:, 0:LANES])
                carry[...] = rolled[:, 0:LANES]
                for h in range(FOX_HEADS):
                    hr = slice(h * FOX_DH, (h + 1) * FOX_DH)
                    o_ref[0, h, :, 0:LANES] = first[hr]
                    o_ref[0, h, :, LANES:tm] = rolled[hr, LANES:tm]

    if not row_off:
        project()
        return

    i = pl.program_id(1)

    @pl.when(i == 0)
    def _():
        carries[0][...] = lead_k_ref[...]
        carries[1][...] = lead_v_ref[...]

    pl.when(i < nt)(project)

    @pl.when(i == nt)
    def _():
        for carry, o_ref in ((carries[0], k32_ref), (carries[1], v32_ref)):
            for h in range(FOX_HEADS):
                o_ref[0, h, :, 0:LANES] = carry[h * FOX_DH:(h + 1) * FOX_DH, :]


def _inproj(x2d, nbatch, n, nb, tmb, tw, w, lead=None):
    tm = nb * tmb
    ngrp, nt = nbatch // nb, n // tmb
    t_total = nbatch * n
    row_off = N_META if lead is not None else 0
    steps = nt + 1 if row_off else nt
    blk = (lambda i: jnp.minimum(i, nt - 1)) if row_off else (lambda i: i)
    tok = lambda width: pl.BlockSpec((tm, width), lambda g, i: (g * nt + blk(i), 0))
    out_shape = (
        jax.ShapeDtypeStruct((t_total, _C_FQ), BF16),
        jax.ShapeDtypeStruct((nbatch, n // tw, GLA_QK, tw), BF16),
        jax.ShapeDtypeStruct((nbatch, n // tw, GLA_QK, tw), F32),
        jax.ShapeDtypeStruct((nbatch, FOX_HEADS, n), F32),
        jax.ShapeDtypeStruct((t_total, FOX_W), BF16),
        jax.ShapeDtypeStruct((nbatch, FOX_W, n), BF16),
        jax.ShapeDtypeStruct((nbatch, FOX_W, n), BF16),
        jax.ShapeDtypeStruct((nbatch, FOX_HEADS, FOX_DH, row_off + n), F32),
        jax.ShapeDtypeStruct((nbatch, FOX_HEADS, FOX_DH, row_off + n), F32),
        jax.ShapeDtypeStruct((t_total, 2 * D_MODEL), BF16),
    )
    tblk = pl.BlockSpec((nb, tmb // tw, GLA_QK, tw), lambda g, i: (g, blk(i), 0, 0))
    lblk = pl.BlockSpec((nb, FOX_W, tmb), lambda g, i: (g, 0, blk(i)))
    sblk = pl.BlockSpec((nb, FOX_HEADS, FOX_DH, tmb), lambda g, i: (g, 0, 0, i))
    out_specs = (
        tok(_C_FQ), tblk, tblk,
        pl.BlockSpec((nb, FOX_HEADS, tmb), lambda g, i: (g, 0, blk(i))),
        tok(FOX_W), lblk, lblk, sblk, sblk, tok(2 * D_MODEL),
    )
    in_specs = [
        tok(D_MODEL),
        _const_spec((1, D_MODEL)),
        _const_spec((D_MODEL, _C_END)),
        _const_spec((_R_END, D_MODEL)),
        _const_spec((GLA_QK, GLA_RANK)),
        _const_spec((GLA_QK, 1)),
        _const_spec((FOX_HEADS, 1)),
    ]
    args = [x2d, w["gn_mix"], w["w_nn"], w["w_t"], w["w_gate_t"], w["b_gate_col"], w["b_f_col"]]
    scratch = []
    if row_off:
        assert nb == 1 and tmb % LANES == 0
        in_specs += [_const_spec((FOX_W, LANES))] * 2
        args += list(lead)
        scratch = [pltpu.VMEM((FOX_W, LANES), F32)] * 2
    return pl.pallas_call(
        functools.partial(_inproj_kernel, nb=nb, tmb=tmb, tw=tw, nt=nt, row_off=row_off),
        grid=(ngrp, steps), in_specs=in_specs, out_specs=out_specs, out_shape=out_shape,
        scratch_shapes=scratch,
        compiler_params=pltpu.CompilerParams(
            dimension_semantics=("arbitrary", "arbitrary"), vmem_limit_bytes=VMEM_LIMIT),
        name="inproj",
    )(*args)


def _cumsum_lanes(x, carry):
    n = x.shape[1]
    r = lax.broadcasted_iota(jnp.int32, (LANES, LANES), 0)
    c = lax.broadcasted_iota(jnp.int32, (LANES, LANES), 1)
    upper = jnp.where(r <= c, 1.0, 0.0).astype(BF16)
    outs = []
    for j in range(n // LANES):
        cj = _dot3(x[:, j * LANES:(j + 1) * LANES], upper) + carry
        outs.append(cj)
        carry = cj[:, LANES - 1:LANES]
    return jnp.concatenate(outs, axis=1), carry


def _decay_kernel(lp_ref, ln_ref, cp_ref, cn_ref, *, p_valid):
    lp = lp_ref[...]
    lane = lax.broadcasted_iota(jnp.int32, lp.shape, 1)
    lp = jnp.where(lane < p_valid, lp, 0.0)
    zero = jnp.zeros((lp.shape[0], 1), F32)
    cp, tot = _cumsum_lanes(lp, zero)
    cp_ref[...] = cp * LOG2E
    cn, _ = _cumsum_lanes(ln_ref[...], tot)
    cn_ref[...] = cn * LOG2E


def _decay(logf_p, logf_n, p_valid):
    r, pp = logf_p.shape
    npad = logf_n.shape[1]
    return pl.pallas_call(
        functools.partial(_decay_kernel, p_valid=p_valid),
        out_shape=(jax.ShapeDtypeStruct((r, pp), F32), jax.ShapeDtypeStruct((r, npad), F32)),
        compiler_params=pltpu.CompilerParams(vmem_limit_bytes=VMEM_LIMIT),
        name="decay",
    )(logf_p, logf_n)


def _gla_kernel(qvr_ref, kT_ref, laT_ref, s0_ref, gn_ref, y_ref, sout_ref, *, n, tw, chunk, need_y):
    nblk = n // tw
    ncb = tw // chunk

    sr = lax.broadcasted_iota(jnp.int32, (tw, LANES), 0)
    tr = lax.broadcasted_iota(jnp.int32, (tw, tw), 0)
    tc = lax.broadcasted_iota(jnp.int32, (tw, tw), 1)
    m_cum = ((tr // chunk) == (tc // chunk)) & (tr <= tc)
    m_all = jnp.concatenate(
        [jnp.where((sr // chunk) == ci, 1.0, 0.0) for ci in range(ncb)] + [jnp.where(m_cum, 1.0, 0.0)],
        axis=1).astype(BF16)
    stack_k = tw % LANES == 0
    m_rhs = jnp.concatenate([m_all] * 3, axis=0) if stack_k else m_all
    lane_chunk = lax.broadcasted_iota(jnp.int32, (GLA_QK, tw), 1) // chunk
    zero_blk = jnp.zeros((GLA_DK, GLA_DV), BF16)
    gn = gn_ref[...]

    def group(jg, state):
        chunks = []
        for t in range(nbg):
            jb = jg * nbg + t
            hi, mid, lo = _split3(laT_ref[0, jb])
            if stack_k:
                g = _dot(jnp.concatenate([hi, mid, lo], axis=1), m_rhs)
            else:
                g = _dot(hi, m_rhs) + _dot(mid, m_rhs) + _dot(lo, m_rhs)
            tot = [g[:, ci * LANES:(ci + 1) * LANES] for ci in range(ncb)]
            g_cum = g[:, ncb * LANES:ncb * LANES + tw]
            g_tot = tot[0][:, 0:tw] if tw <= LANES else jnp.concatenate([tot[0]] * (tw // LANES), axis=1)
            for ci in range(1, ncb):
                g_tot = jnp.where(lane_chunk == ci, tot[ci][:, 0:tw], g_tot)
            kdec = (kT_ref[0, jb].astype(F32) * jnp.exp(g_tot - g_cum)).astype(BF16)
            for ci in range(ncb):
                rows = pl.ds(pl.multiple_of(jb * tw, tw) + ci * chunk, chunk)
                v = qvr_ref[rows, _C_GV:_C_GR]
                us = [_dot(kdec[h * GLA_DK:(h + 1) * GLA_DK, ci * chunk:(ci + 1) * chunk],
                           v[:, h * GLA_DV:(h + 1) * GLA_DV]) for h in range(GLA_HEADS)]
                chunks.append((rows, jnp.exp(tot[ci]), us))
        s_bds = []
        for rows, a, us in chunks:
            state = tuple(a[h * GLA_DK:(h + 1) * GLA_DK] * state[h] + us[h] for h in range(GLA_HEADS))
            if need_y:
                s_bds.append(jnp.concatenate(
                    [jnp.concatenate([state[h].astype(BF16) if hc == h else zero_blk
                                      for hc in range(GLA_HEADS)], axis=1) for h in range(GLA_HEADS)], axis=0))
        if need_y:
            for (rows, _, _), s_bd in zip(chunks, s_bds):
                o = _dot(qvr_ref[rows, _C_GQ:_C_GV], s_bd)
                gate = jax.nn.silu(qvr_ref[rows, _C_GR:_C_FQ].astype(F32))
                parts = []
                for h in range(GLA_HEADS):
                    oh = o[:, h * GLA_DV:(h + 1) * GLA_DV]
                    msq = jnp.mean(oh * oh, axis=-1, keepdims=True)
                    parts.append(oh * lax.rsqrt(msq + EPS))
                on = jnp.concatenate(parts, axis=1) * gn
                y_ref[rows, :] = (on * gate).astype(BF16)
        return state

    nbg = 8 if nblk % 8 == 0 else 1
    state = tuple(s0_ref[0, h] for h in range(GLA_HEADS))
    state = lax.fori_loop(0, nblk // nbg, group, state)
    for h in range(GLA_HEADS):
        sout_ref[0, h] = state[h]


def _gla(qvr, kT, laT, s0, gn_tiled, nbatch, n, tw, chunk, need_y):
    s0_map = (lambda b: (0, 0, 0, 0)) if s0.shape[0] == 1 else (lambda b: (b, 0, 0, 0))
    kern = functools.partial(_gla_kernel, n=n, tw=tw, chunk=chunk, need_y=need_y)
    if not need_y:
        def kern(qvr_ref, kT_ref, laT_ref, s0_ref, gn_ref, sout_ref):
            _gla_kernel(qvr_ref, kT_ref, laT_ref, s0_ref, gn_ref, None, sout_ref,
                        n=n, tw=tw, chunk=chunk, need_y=False)
    state_shape = (GLA_HEADS, GLA_DK, GLA_DV)
    out_shape = [jax.ShapeDtypeStruct((nbatch,) + state_shape, F32)]
    out_specs = [pl.BlockSpec((1,) + state_shape, lambda b: (b, 0, 0, 0))]
    if need_y:
        out_shape = [jax.ShapeDtypeStruct((nbatch * n, GLA_V), BF16)] + out_shape
        out_specs = [pl.BlockSpec((n, GLA_V), lambda b: (b, 0))] + out_specs
    res = pl.pallas_call(
        kern, grid=(nbatch,),
        in_specs=[
            pl.BlockSpec((n, _C_FQ), lambda b: (b, 0)),
            pl.BlockSpec((1, n // tw, GLA_QK, tw), lambda b: (b, 0, 0, 0)),
            pl.BlockSpec((1, n // tw, GLA_QK, tw), lambda b: (b, 0, 0, 0)),
            pl.BlockSpec((1,) + state_shape, s0_map),
            _const_spec((1, GLA_V)),
        ],
        out_specs=out_specs, out_shape=out_shape,
        compiler_params=pltpu.CompilerParams(
            dimension_semantics=("arbitrary",), vmem_limit_bytes=VMEM_LIMIT),
        name="gla",
    )(qvr, kT, laT, s0, gn_tiled)
    return res if need_y else (None, res[0])


def _fold_lanes(x, op):
    w = x.shape[1]
    if w % LANES == 0:
        f = x[:, 0:LANES]
        for g in range(1, w // LANES):
            f = op(f, x[:, g * LANES:(g + 1) * LANES])
        return f
    assert op is jnp.add
    lane = lax.broadcasted_iota(jnp.int32, (x.shape[0], LANES), 1)
    return jnp.where(lane == 0, jnp.sum(x, axis=-1, keepdims=True), 0.0)


def _tile_lanes(x, w):
    return x[:, 0:w] if w < LANES else jnp.concatenate([x] * (w // LANES), axis=1)


def _fox_kernel(q_ref, kn_ref, vn_ref, kp_ref, vp_ref, crow_ref, cprow_ref, o_ref, ka_scr, va_scr, *scr,
                n, tq, tkp, p_pad, p_valid, pv_t, nhp):
    nq = n // tq
    rb = min(tq, 64)
    nh = 2 * nhp
    nbuf = len(scr) // (3 * nh)
    s_scr, p_scr, sh_scr = (scr[i * nh * nbuf:(i + 1) * nh * nbuf] for i in range(3))
    nparts = 3
    ones_rows = 16

    def c_block(c_row):
        parts = _split3(-c_row)
        r = lax.broadcasted_iota(jnp.int32, (FOX_DH, c_row.shape[1]), 0)
        blk = jnp.zeros((FOX_DH, c_row.shape[1]), F32)
        for i, part in enumerate(parts):
            blk = jnp.where(r == i, part.astype(F32), blk)
        return blk.astype(BF16)

    for h in range(nh):
        hp, hh = divmod(h, 2)
        hrows = slice(h * FOX_DH, (h + 1) * FOX_DH)
        for k_src, c_src, c0, w in ((kp_ref[0, h].astype(BF16), cprow_ref[0, hp, hh:hh + 1, :], 0, p_pad),
                                    (kn_ref[0, hrows, :], crow_ref[0, hp, hh:hh + 1, :], p_pad, n)):
            halves = (k_src, c_block(c_src)) if hh == 0 else (c_block(c_src), k_src)
            ka_scr[h, :, c0:c0 + w] = jnp.concatenate(halves, axis=0)
        va_scr[h, 0:FOX_DH, 0:p_pad] = vp_ref[0, h].astype(BF16)
        va_scr[h, 0:FOX_DH, p_pad:p_pad + n] = vn_ref[0, hrows, :]
        va_scr[h, FOX_DH:FOX_DH + ones_rows, :] = jnp.ones((ones_rows, p_pad + n), BF16)

    lane_q = lax.broadcasted_iota(jnp.int32, (tq, LANES), 1)
    eye = jnp.where(lax.broadcasted_iota(jnp.int32, (tq, tq), 0) == lax.broadcasted_iota(jnp.int32, (tq, tq), 1),
                    1.0, 0.0).astype(BF16)

    def block_chunks(qi):
        return [(c0, tkp) for c0 in range(0, p_pad, tkp)] + [(p_pad + j * tq, tq) for j in range(qi + 1)]

    def pass1_items(qi):
        rows = slice(qi * tq, (qi + 1) * tq)
        chunks = block_chunks(qi)
        state = {}

        def start():
            c_parts = [part for h in range(nh) for part in _split3(crow_ref[0, h // 2, h % 2:h % 2 + 1, rows])]
            pad_rows = -len(c_parts) % ones_rows
            c_rows = jnp.concatenate(c_parts + [jnp.zeros((pad_rows, tq), BF16)], axis=0)
            state["c_cols"] = _dot_nt(eye, c_rows)
            for h in range(nh):
                hp, hh = divmod(h, 2)
                qp = q_ref[rows, hp * LANES:(hp + 1) * LANES]
                c_lane0 = FOX_DH if hh == 0 else 0
                own = (lane_q < FOX_DH) if hh == 0 else (lane_q >= FOX_DH)
                ones_pat = jnp.where((lane_q >= c_lane0) & (lane_q < c_lane0 + nparts), 1.0, 0.0).astype(BF16)
                state["q", h] = jnp.where(own, qp, ones_pat)
                state["mx", h] = None

        def chunk(h, ci):
            c0, cw = chunks[ci]
            s = _dot(state["q", h], ka_scr[h, :, c0:c0 + cw])
            if c0 < p_pad and c0 + cw > p_valid:
                col = lax.broadcasted_iota(jnp.int32, s.shape, 1) + c0
                s = jnp.where(col < p_valid, s, NEG)
            if ci == len(chunks) - 1:
                r = lax.broadcasted_iota(jnp.int32, s.shape, 0)
                c = lax.broadcasted_iota(jnp.int32, s.shape, 1)
                s = jnp.where(c <= r, s, NEG)
            s_scr[buf(qi, h)][:, c0:c0 + cw] = s
            f = _fold_lanes(s, jnp.maximum) if cw % LANES == 0 else jnp.max(s, axis=-1, keepdims=True)
            mx = state["mx", h]
            if mx is None or mx.shape == f.shape:
                mx = f if mx is None else jnp.maximum(mx, f)
            else:
                mx = jnp.maximum(jnp.max(mx, axis=-1, keepdims=True), jnp.max(f, axis=-1, keepdims=True))
            state["mx", h] = mx

        def finish(h):
            cq = sum(state["c_cols"][:, h * nparts + i:h * nparts + i + 1] for i in range(nparts))
            m = jnp.max(state["mx", h], axis=-1, keepdims=True) + cq
            sh_scr[buf(qi, h)][...] = jnp.broadcast_to(m - cq, (tq, LANES))

        items = [start]
        for ci in range(len(chunks)):
            items += [functools.partial(chunk, h, ci) for h in range(nh)]
        return items + [functools.partial(finish, h) for h in range(nh)]

    def pass2_items(qi):
        def item(r0, h, c0, cw):
            rr = slice(r0, r0 + rb)
            sh = sh_scr[buf(qi, h)][rr, :]
            p = jnp.exp2(s_scr[buf(qi, h)][rr, c0:c0 + cw] - _tile_lanes(sh, cw))
            p_scr[buf(qi, h)][rr, c0:c0 + cw] = p.astype(BF16)
        return [functools.partial(item, r0, h, c0, cw)
                for r0 in range(0, tq, rb) for h in range(nh) for c0, cw in block_chunks(qi)]

    def pv_items(qi):
        w_tot = p_pad + (qi + 1) * tq
        acc = {}

        def matmul(h):
            p_all, v_aug = p_scr[buf(qi, h)][:, 0:w_tot], va_scr[h, :, 0:w_tot]
            acc[h] = _dot_nt(v_aug, p_all) if pv_t else _dot_nt(p_all, v_aug)

        def store():
            if pv_t:
                outs = [acc[h][0:FOX_DH] * (1.0 / acc[h][FOX_DH:FOX_DH + 1]) for h in range(nh)]
                o = jnp.concatenate(outs, axis=0).T
            else:
                outs = [acc[h][:, 0:FOX_DH] * (1.0 / acc[h][:, FOX_DH:FOX_DH + 1]) for h in range(nh)]
                o = jnp.concatenate(outs, axis=1)
            o_ref[qi * tq:(qi + 1) * tq, :] = o.astype(BF16)

        return [functools.partial(matmul, h) for h in range(nh)] + [store]

    def emit_merged(*lists):
        total = max(len(l) for l in lists)
        pos = [0] * len(lists)
        for t in range(total):
            for k, l in enumerate(lists):
                while pos[k] < len(l) and pos[k] * total <= t * len(l):
                    l[pos[k]]()
                    pos[k] += 1
        for k, l in enumerate(lists):
            for item in l[pos[k]:]:
                item()

    buf = lambda qi, h: (qi % nbuf) * nh + h
    emit_merged(pass1_items(0))
    for qi in range(nq):
        emit_merged(pass2_items(qi),
                    pass1_items(qi + 1) if qi + 1 < nq else [],
                    pv_items(qi - 1) if qi > 0 else [])
    emit_merged(pv_items(nq - 1))


def _fox(q, kn, vn, kp, vp, c_row, cp_row, nbatch, n, tq, tkp, p_valid, pv_t, nhp):
    nq = n // tq
    p_pad = kp.shape[3]
    pmap4 = (lambda b, hp: (0, hp, 0, 0)) if kp.shape[0] == 1 else (lambda b, hp: (b, hp, 0, 0))
    cmap4 = (lambda b, hp: (0, hp, 0, 0)) if cp_row.shape[0] == 1 else (lambda b, hp: (b, hp, 0, 0))
    nh = 2 * nhp
    tok = pl.BlockSpec((n, nhp * LANES), lambda b, hp: (b, hp))
    tlanes = pl.BlockSpec((1, nhp * LANES, n), lambda b, hp: (b, hp, 0))
    nbuf = min(nq, 2)
    return pl.pallas_call(
        functools.partial(_fox_kernel, n=n, tq=tq, tkp=tkp, p_pad=p_pad, p_valid=p_valid, pv_t=pv_t, nhp=nhp),
        grid=(nbatch, FOX_HEADS // nh),
        in_specs=[
            tok, tlanes, tlanes,
            pl.BlockSpec((1, nh, FOX_DH, p_pad), pmap4),
            pl.BlockSpec((1, nh, FOX_DH, p_pad), pmap4),
            pl.BlockSpec((1, nhp, 2, n), lambda b, hp: (b, hp, 0, 0)),
            pl.BlockSpec((1, nhp, 2, p_pad), cmap4),
        ],
        out_specs=tok,
        out_shape=jax.ShapeDtypeStruct((nbatch * n, FOX_W), BF16),
        scratch_shapes=([pltpu.VMEM((nh, LANES, p_pad + n), BF16), pltpu.VMEM((nh, FOX_DH + 16, p_pad + n), BF16)]
                        + [pltpu.VMEM((tq, p_pad + n), F32)] * (nh * nbuf)
                        + [pltpu.VMEM((tq, p_pad + n), BF16)] * (nh * nbuf)
                        + [pltpu.VMEM((tq, LANES), F32)] * (nh * nbuf)),
        compiler_params=pltpu.CompilerParams(
            dimension_semantics=("arbitrary", "arbitrary"), vmem_limit_bytes=VMEM_LIMIT),
        name="fox",
    )(q, kn, vn, kp, vp, c_row, cp_row)


def _merge_ffn_kernel(x_ref, ygla_ref, ofox_ref, sg_ref, wbg_ref, wbf_ref, wout_ref, gffn_ref,
                      wup_ref, wdown_ref, gfin_ref, y_ref, *, ff_chunk):
    ya = _dot(ygla_ref[...], wbg_ref[...])
    yb = _dot(ofox_ref[...], wbf_ref[...])
    m = sg_ref[:, 0:D_MODEL].astype(F32) * ya + sg_ref[:, D_MODEL:2 * D_MODEL].astype(F32) * yb
    h = x_ref[...] + _dot(m.astype(BF16), wout_ref[...])
    ms = jnp.mean(h * h, axis=-1, keepdims=True)
    hn = ((h * lax.rsqrt(ms + EPS)) * gffn_ref[...]).astype(BF16)
    for c in range(D_FF // ff_chunk):
        u = _dot(hn, wup_ref[:, c * ff_chunk:(c + 1) * ff_chunk])
        r = jnp.square(jnp.maximum(u, 0.0)).astype(BF16)
        h = h + _dot(r, wdown_ref[c * ff_chunk:(c + 1) * ff_chunk, :])
    ms = jnp.mean(h * h, axis=-1, keepdims=True)
    y_ref[...] = (h * lax.rsqrt(ms + EPS)) * gfin_ref[...]


def _merge_ffn(x2d, ygla, ofox, sg, w, tm):
    t_total = x2d.shape[0]
    tok = lambda width: pl.BlockSpec((tm, width), lambda i: (i, 0))
    return pl.pallas_call(
        functools.partial(_merge_ffn_kernel, ff_chunk=1024),
        grid=(t_total // tm,),
        in_specs=[
            tok(D_MODEL), tok(GLA_V), tok(FOX_W), tok(2 * D_MODEL),
            _const_spec((GLA_V, D_MODEL)), _const_spec((FOX_W, D_MODEL)),
            _const_spec((D_MODEL, D_MODEL)), _const_spec((1, D_MODEL)),
            _const_spec((D_MODEL, D_FF)), _const_spec((D_FF, D_MODEL)), _const_spec((1, D_MODEL)),
        ],
        out_specs=tok(D_MODEL),
        out_shape=jax.ShapeDtypeStruct((t_total, D_MODEL), F32),
        compiler_params=pltpu.CompilerParams(
            dimension_semantics=("arbitrary",), vmem_limit_bytes=VMEM_LIMIT),
        name="merge_ffn",
    )(x2d, ygla, ofox, sg, w["w_bg"], w["w_bf"], w["w_out"], w["gn_ffn"], w["w_up"], w["w_down"],
      w["gn_final"])


def _prep_weights(norm_mix, w_in, w_gla_gate, b_gla_gate, g_gla_norm, b_fox_forget, w_branch_gla,
                  w_branch_fox, w_out, norm_ffn, w_up, w_down, norm_final):
    sizes = (GLA_QK, GLA_QK, GLA_V, GLA_V, GLA_RANK, FOX_W, FOX_W, FOX_W, FOX_HEADS, D_MODEL, D_MODEL)
    offs = [0]
    for s in sizes:
        offs.append(offs[-1] + s)
    col = lambda i: w_in[:, offs[i]:offs[i + 1]]
    gq, gk, gv, gr, glr, fq, fk, fv, ff, ga, gb = (col(i) for i in range(11))
    w_nn = jnp.concatenate([gq * (GLA_DK ** -0.5), gv, gr, fq * (FOX_DH ** -0.5 * LOG2E), ga, gb],
                           axis=1).astype(BF16)
    pad = jnp.zeros((D_MODEL, _R_FK - _R_FF - FOX_HEADS), F32)
    w_t = jnp.concatenate([gk, glr, ff, pad, fk, fv], axis=1).T.astype(BF16)
    return dict(
        gn_mix=norm_mix.reshape(1, D_MODEL), w_nn=w_nn, w_t=w_t,
        w_gate_t=w_gla_gate.T.astype(BF16), b_gate_col=b_gla_gate.reshape(GLA_QK, 1),
        b_f_col=b_fox_forget.reshape(FOX_HEADS, 1),
        gn_gla=jnp.tile(g_gla_norm.reshape(1, GLA_DV), (1, GLA_HEADS)),
        w_bg=w_branch_gla.astype(BF16), w_bf=w_branch_fox.astype(BF16), w_out=w_out.astype(BF16),
        gn_ffn=norm_ffn.reshape(1, D_MODEL), w_up=w_up.astype(BF16), w_down=w_down.astype(BF16),
        gn_final=norm_final.reshape(1, D_MODEL),
    )


def _pad_lanes(x):
    n = x.shape[-1]
    npad = -(-n // LANES) * LANES
    return x if npad == n else jnp.pad(x, [(0, 0)] * (x.ndim - 1) + [(0, npad - n)])


def kernel(x_prompt, x_sample, cache_fox_k, cache_fox_v, cache_fox_logf, state_gla, meta_tokens,
           norm_mix, w_in, w_gla_gate, b_gla_gate, g_gla_norm, b_fox_forget, w_branch_gla,
           w_branch_fox, w_out, norm_ffn, w_up, w_down, norm_final):
    bsz, seq, _ = x_prompt.shape
    dbsz, dseq, _ = x_sample.shape
    past = cache_fox_k.shape[3]
    w = _prep_weights(norm_mix[0], w_in[0], w_gla_gate[0], b_gla_gate[0], g_gla_norm[0],
                      b_fox_forget[0], w_branch_gla[0], w_branch_fox[0], w_out[0], norm_ffn[0],
                      w_up[0], w_down[0], norm_final)
    xp = x_prompt.reshape(bsz * seq, D_MODEL)
    xs = x_sample.reshape(dbsz * dseq, D_MODEL)

    (m_qvr, m_kT, m_laT, m_logf, _, _, _, m_kT32, m_vT32, _) = _inproj(
        meta_tokens.astype(F32), 1, N_META, 1, N_META, N_META, w)
    zero_state = jnp.zeros((1, GLA_HEADS, GLA_DK, GLA_DV), F32)
    _, s_meta = _gla(m_qvr, m_kT, m_laT, zero_state, w["gn_gla"], 1, N_META, N_META, N_META, False)
    kp, vp = _pad_lanes(m_kT32), _pad_lanes(m_vT32)

    tm = 512
    lead = (kp.reshape(FOX_W, LANES), vp.reshape(FOX_W, LANES))
    (p_qvr, p_kT, p_laT, p_logf, p_fq, p_fkT, p_fvT, p_kT32, p_vT32, p_sg) = _inproj(
        xp, bsz, seq, 1, tm, LANES, w, lead=lead)
    y_gla, s_p = _gla(p_qvr, p_kT, p_laT, s_meta, w["gn_gla"], bsz, seq, LANES, CHUNK, True)
    lp = _pad_lanes(jnp.broadcast_to(m_logf, (bsz, FOX_HEADS, N_META)).reshape(bsz * FOX_HEADS, N_META))
    cp, cn = _decay(lp, p_logf.reshape(bsz * FOX_HEADS, seq), N_META)
    tq = 256
    c_row = cn.reshape(bsz, FOX_HEADS // 2, 2, seq)
    cp_row = cp[:FOX_HEADS].reshape(1, FOX_HEADS // 2, 2, LANES)
    o_fox = _fox(p_fq, p_fkT, p_fvT, kp, vp, c_row, cp_row, bsz, seq, tq, LANES, N_META, True, 1)
    y_prompt = _merge_ffn(xp, y_gla, o_fox, p_sg, w, tm).reshape(bsz, seq, D_MODEL)

    (s_qvr, s_kT, s_laT, s_logf, s_fq, s_fkT, s_fvT, s_kT32, s_vT32, s_sg) = _inproj(
        xs, dbsz, dseq, dbsz, dseq, dseq, w)
    ys_gla, s_s = _gla(s_qvr, s_kT, s_laT, state_gla[0].astype(F32), w["gn_gla"],
                       dbsz, dseq, dseq, dseq, True)
    cps, cns = _decay(cache_fox_logf[0].astype(F32).reshape(dbsz * FOX_HEADS, past),
                      _pad_lanes(s_logf.reshape(dbsz * FOX_HEADS, dseq)), past)
    cs_row = cns[:, :dseq].reshape(dbsz, FOX_HEADS // 2, 2, dseq)
    cps_row = cps.reshape(dbsz, FOX_HEADS // 2, 2, past)
    os_fox = _fox(s_fq, s_fkT, s_fvT, jnp.swapaxes(cache_fox_k[0], 2, 3).astype(F32),
                  jnp.swapaxes(cache_fox_v[0], 2, 3).astype(F32), cs_row, cps_row,
                  dbsz, dseq, dseq, 256, past, False, FOX_HEADS // 2)
    y_sample = _merge_ffn(xs, ys_gla, os_fox, s_sg, w, dbsz * dseq).reshape(dbsz, dseq, D_MODEL)

    to_state = lambda a: jnp.swapaxes(a, 2, 3)[None]
    new_fox_logf_prompt = jnp.concatenate(
        [jnp.broadcast_to(m_logf, (bsz, FOX_HEADS, N_META)), p_logf], axis=2)[None]
    return (y_prompt, y_sample, to_state(p_kT32), to_state(p_vT32), new_fox_logf_prompt,
            s_p[None], to_state(s_kT32), to_state(s_vT32), s_logf[None], s_s[None])
```

```python
import functools

import jax
import jax.numpy as jnp
from jax import lax
from jax.experimental import pallas as pl
from jax.experimental.pallas import tpu as pltpu

F32 = jnp.float32
BF16 = jnp.bfloat16

D_MODEL = 1024
N_META = 16
EPS = 1e-6
GLA_HEADS = 4
GLA_DK = 64
GLA_DV = 128
GLA_RANK = 16
GLA_TAU = 16.0
GLA_QK = GLA_HEADS * GLA_DK
GLA_V = GLA_HEADS * GLA_DV
FOX_HEADS = 8
FOX_DH = 64
FOX_W = FOX_HEADS * FOX_DH
D_FF = 4 * D_MODEL
NEG = -1e30
CHUNK = 64
LOG2E = 1.4426950408889634

LANES = 128
VMEM_LIMIT = 56 * 1024 * 1024

_C_GQ, _C_GV, _C_GR, _C_FQ, _C_GA, _C_GB, _C_END = 0, 256, 768, 1280, 1792, 2816, 3840
_R_K, _R_LR, _R_FF, _R_FK, _R_FV, _R_END = 0, 256, 272, 288, 800, 1312


def _dot(a, b):
    return jnp.dot(a, b, preferred_element_type=F32)


def _dot_nt(a, b):
    return lax.dot_general(a, b, (((1,), (1,)), ((), ())), preferred_element_type=F32)


def _split3(x):
    hi = x.astype(BF16)
    r = x - hi.astype(F32)
    mid = r.astype(BF16)
    lo = (r - mid.astype(F32)).astype(BF16)
    return hi, mid, lo


def _dot3(x, m):
    hi, mid, lo = _split3(x)
    return _dot(hi, m) + _dot(mid, m) + _dot(lo, m)


def _log_sigmoid(x):
    return jnp.minimum(x, 0.0) - jnp.log1p(jnp.exp(-jnp.abs(x)))


def _const_spec(shape):
    nd = len(shape)
    return pl.BlockSpec(shape, lambda *_: (0,) * nd, pipeline_mode=pl.Buffered(1))


def _inproj_kernel(x_ref, gn_ref, wnn_ref, wt_ref, wg_ref, bg_ref, bf_ref, *rest, nb, tmb, tw, nt, row_off):
    if row_off:
        lead_k_ref, lead_v_ref = rest[:2]
        rest = rest[2:]
    (qvr_ref, kT_ref, laT_ref, logf_ref, fq_ref, fkT_ref, fvT_ref, k32_ref, v32_ref, sg_ref) = rest[:10]
    carries = rest[10:]
    tm = nb * tmb

    def project():
        x = x_ref[...]
        ms = jnp.mean(x * x, axis=-1, keepdims=True)
        hn = ((x * lax.rsqrt(ms + EPS)) * gn_ref[...]).astype(BF16)

        zT = _dot_nt(wt_ref[...], hn)
        kT = zT[_R_K:_R_LR].astype(BF16)
        glrT = zT[_R_LR:_R_FF].astype(BF16)
        laT = _log_sigmoid(_dot(wg_ref[...], glrT) + bg_ref[...]) * (1.0 / GLA_TAU)
        logf = _log_sigmoid(zT[_R_FF:_R_FF + FOX_HEADS] + bf_ref[...])
        zk, zv = zT[_R_FK:_R_FV], zT[_R_FV:_R_END]
        for bb in range(nb):
            r0 = bb * tmb
            logf_ref[bb] = logf[:, r0:r0 + tmb]
            fkT_ref[bb] = zk[:, r0:r0 + tmb].astype(BF16)
            fvT_ref[bb] = zv[:, r0:r0 + tmb].astype(BF16)
            for j in range(tmb // tw):
                kT_ref[bb, j] = kT[:, r0 + j * tw:r0 + (j + 1) * tw]
                laT_ref[bb, j] = laT[:, r0 + j * tw:r0 + (j + 1) * tw]
            if not row_off:
                for h in range(FOX_HEADS):
                    k32_ref[bb, h] = zk[h * FOX_DH:(h + 1) * FOX_DH, r0:r0 + tmb]
                    v32_ref[bb, h] = zv[h * FOX_DH:(h + 1) * FOX_DH, r0:r0 + tmb]
        if row_off:
            lane = lax.broadcasted_iota(jnp.int32, (FOX_W, LANES), 1)
            for z, carry, o_ref in ((zk, carries[0], k32_ref), (zv, carries[1], v32_ref)):
                rolled = pltpu.roll(z, row_off, axis=1)
                first = jnp.where(lane < row_off, carry[...], rolled[:, 0:LANES])
                carry[...] = rolled[:, 0:LANES]
                for h in range(FOX_HEADS):
                    hr = slice(h * FOX_DH, (h + 1) * FOX_DH)
                    o_ref[0, h, :, 0:LANES] = first[hr]
                    o_ref[0, h, :, LANES:tm] = rolled[hr, LANES:tm]

        sg_ref[...] = jax.nn.sigmoid(_dot(hn, wnn_ref[:, _C_GA:_C_END])).astype(BF16)
        qvr_ref[...] = _dot(hn, wnn_ref[:, _C_GQ:_C_FQ]).astype(BF16)
        fq_ref[...] = _dot(hn, wnn_ref[:, _C_FQ:_C_GA]).astype(BF16)

    if not row_off:
        project()
        return

    i = pl.program_id(1)

    @pl.when(i == 0)
    def _():
        carries[0][...] = lead_k_ref[...]
        carries[1][...] = lead_v_ref[...]

    pl.when(i < nt)(project)

    @pl.when(i == nt)
    def _():
        for carry, o_ref in ((carries[0], k32_ref), (carries[1], v32_ref)):
            for h in range(FOX_HEADS):
                o_ref[0, h, :, 0:LANES] = carry[h * FOX_DH:(h + 1) * FOX_DH, :]


def _inproj(x2d, nbatch, n, nb, tmb, tw, w, lead=None):
    tm = nb * tmb
    ngrp, nt = nbatch // nb, n // tmb
    t_total = nbatch * n
    row_off = N_META if lead is not None else 0
    steps = nt + 1 if row_off else nt
    blk = (lambda i: jnp.minimum(i, nt - 1)) if row_off else (lambda i: i)
    tok = lambda width: pl.BlockSpec((tm, width), lambda g, i: (g * nt + blk(i), 0))
    out_shape = (
        jax.ShapeDtypeStruct((t_total, _C_FQ), BF16),
        jax.ShapeDtypeStruct((nbatch, n // tw, GLA_QK, tw), BF16),
        jax.ShapeDtypeStruct((nbatch, n // tw, GLA_QK, tw), F32),
        jax.ShapeDtypeStruct((nbatch, FOX_HEADS, n), F32),
        jax.ShapeDtypeStruct((t_total, FOX_W), BF16),
        jax.ShapeDtypeStruct((nbatch, FOX_W, n), BF16),
        jax.ShapeDtypeStruct((nbatch, FOX_W, n), BF16),
        jax.ShapeDtypeStruct((nbatch, FOX_HEADS, FOX_DH, row_off + n), F32),
        jax.ShapeDtypeStruct((nbatch, FOX_HEADS, FOX_DH, row_off + n), F32),
        jax.ShapeDtypeStruct((t_total, 2 * D_MODEL), BF16),
    )
    tblk = pl.BlockSpec((nb, tmb // tw, GLA_QK, tw), lambda g, i: (g, blk(i), 0, 0))
    lblk = pl.BlockSpec((nb, FOX_W, tmb), lambda g, i: (g, 0, blk(i)))
    sblk = pl.BlockSpec((nb, FOX_HEADS, FOX_DH, tmb), lambda g, i: (g, 0, 0, i))
    out_specs = (
        tok(_C_FQ), tblk, tblk,
        pl.BlockSpec((nb, FOX_HEADS, tmb), lambda g, i: (g, 0, blk(i))),
        tok(FOX_W), lblk, lblk, sblk, sblk, tok(2 * D_MODEL),
    )
    in_specs = [
        tok(D_MODEL),
        _const_spec((1, D_MODEL)),
        _const_spec((D_MODEL, _C_END)),
        _const_spec((_R_END, D_MODEL)),
        _const_spec((GLA_QK, GLA_RANK)),
        _const_spec((GLA_QK, 1)),
        _const_spec((FOX_HEADS, 1)),
    ]
    args = [x2d, w["gn_mix"], w["w_nn"], w["w_t"], w["w_gate_t"], w["b_gate_col"], w["b_f_col"]]
    scratch = []
    if row_off:
        assert nb == 1 and tmb % LANES == 0
        in_specs += [_const_spec((FOX_W, LANES))] * 2
        args += list(lead)
        scratch = [pltpu.VMEM((FOX_W, LANES), F32)] * 2
    return pl.pallas_call(
        functools.partial(_inproj_kernel, nb=nb, tmb=tmb, tw=tw, nt=nt, row_off=row_off),
        grid=(ngrp, steps), in_specs=in_specs, out_specs=out_specs, out_shape=out_shape,
        scratch_shapes=scratch,
        compiler_params=pltpu.CompilerParams(
            dimension_semantics=("arbitrary", "arbitrary"), vmem_limit_bytes=VMEM_LIMIT),
        name="inproj",
    )(*args)


def _cumsum_lanes(x, carry):
    n = x.shape[1]
    r = lax.broadcasted_iota(jnp.int32, (LANES, LANES), 0)
    c = lax.broadcasted_iota(jnp.int32, (LANES, LANES), 1)
    upper = jnp.where(r <= c, 1.0, 0.0).astype(BF16)
    outs = []
    for j in range(n // LANES):
        cj = _dot3(x[:, j * LANES:(j + 1) * LANES], upper) + carry
        outs.append(cj)
        carry = cj[:, LANES - 1:LANES]
    return jnp.concatenate(outs, axis=1), carry


def _decay_kernel(lp_ref, ln_ref, cp_ref, cn_ref, *, p_valid):
    lp = lp_ref[...]
    lane = lax.broadcasted_iota(jnp.int32, lp.shape, 1)
    lp = jnp.where(lane < p_valid, lp, 0.0)
    zero = jnp.zeros((lp.shape[0], 1), F32)
    cp, tot = _cumsum_lanes(lp, zero)
    cp_ref[...] = cp * LOG2E
    cn, _ = _cumsum_lanes(ln_ref[...], tot)
    cn_ref[...] = cn * LOG2E


def _decay(logf_p, logf_n, p_valid):
    r, pp = logf_p.shape
    npad = logf_n.shape[1]
    return pl.pallas_call(
        functools.partial(_decay_kernel, p_valid=p_valid),
        out_shape=(jax.ShapeDtypeStruct((r, pp), F32), jax.ShapeDtypeStruct((r, npad), F32)),
        compiler_params=pltpu.CompilerParams(vmem_limit_bytes=VMEM_LIMIT),
        name="decay",
    )(logf_p, logf_n)


def _gla_kernel(qvr_ref, kT_ref, laT_ref, s0_ref, gn_ref, y_ref, sout_ref, *, n, tw, chunk, need_y):
    nblk = n // tw
    ncb = tw // chunk

    sr = lax.broadcasted_iota(jnp.int32, (tw, LANES), 0)
    tr = lax.broadcasted_iota(jnp.int32, (tw, tw), 0)
    tc = lax.broadcasted_iota(jnp.int32, (tw, tw), 1)
    m_cum = ((tr // chunk) == (tc // chunk)) & (tr <= tc)
    m_all = jnp.concatenate(
        [jnp.where((sr // chunk) == ci, 1.0, 0.0) for ci in range(ncb)] + [jnp.where(m_cum, 1.0, 0.0)],
        axis=1).astype(BF16)
    stack_k = tw % LANES == 0
    m_rhs = jnp.concatenate([m_all] * 3, axis=0) if stack_k else m_all
    lane_chunk = lax.broadcasted_iota(jnp.int32, (GLA_QK, tw), 1) // chunk
    zero_blk = jnp.zeros((GLA_DK, GLA_DV), BF16)
    gn = gn_ref[...]

    def group(jg, state):
        chunks = []
        for t in range(nbg):
            jb = jg * nbg + t
            hi, mid, lo = _split3(laT_ref[0, jb])
            if stack_k:
                g = _dot(jnp.concatenate([hi, mid, lo], axis=1), m_rhs)
            else:
                g = _dot(hi, m_rhs) + _dot(mid, m_rhs) + _dot(lo, m_rhs)
            tot = [g[:, ci * LANES:(ci + 1) * LANES] for ci in range(ncb)]
            g_cum = g[:, ncb * LANES:ncb * LANES + tw]
            g_tot = tot[0][:, 0:tw] if tw <= LANES else jnp.concatenate([tot[0]] * (tw // LANES), axis=1)
            for ci in range(1, ncb):
                g_tot = jnp.where(lane_chunk == ci, tot[ci][:, 0:tw], g_tot)
            kdec = (kT_ref[0, jb].astype(F32) * jnp.exp(g_tot - g_cum)).astype(BF16)
            for ci in range(ncb):
                rows = pl.ds(pl.multiple_of(jb * tw, tw) + ci * chunk, chunk)
                v = qvr_ref[rows, _C_GV:_C_GR]
                us = [_dot(kdec[h * GLA_DK:(h + 1) * GLA_DK, ci * chunk:(ci + 1) * chunk],
                           v[:, h * GLA_DV:(h + 1) * GLA_DV]) for h in range(GLA_HEADS)]
                chunks.append((rows, jnp.exp(tot[ci]), us))
        s_bds = []
        for rows, a, us in chunks:
            state = tuple(a[h * GLA_DK:(h + 1) * GLA_DK] * state[h] + us[h] for h in range(GLA_HEADS))
            if need_y:
                s_bds.append(jnp.concatenate(
                    [jnp.concatenate([state[h].astype(BF16) if hc == h else zero_blk
                                      for hc in range(GLA_HEADS)], axis=1) for h in range(GLA_HEADS)], axis=0))
        if need_y:
            for (rows, _, _), s_bd in zip(chunks, s_bds):
                o = _dot(qvr_ref[rows, _C_GQ:_C_GV], s_bd)
                gate = jax.nn.silu(qvr_ref[rows, _C_GR:_C_FQ].astype(F32))
                parts = []
                for h in range(GLA_HEADS):
                    oh = o[:, h * GLA_DV:(h + 1) * GLA_DV]
                    msq = jnp.mean(oh * oh, axis=-1, keepdims=True)
                    parts.append(oh * lax.rsqrt(msq + EPS))
                on = jnp.concatenate(parts, axis=1) * gn
                y_ref[rows, :] = (on * gate).astype(BF16)
        return state

    nbg = 8 if nblk % 8 == 0 else 1
    state = tuple(s0_ref[0, h] for h in range(GLA_HEADS))
    state = lax.fori_loop(0, nblk // nbg, group, state)
    for h in range(GLA_HEADS):
        sout_ref[0, h] = state[h]


def _gla(qvr, kT, laT, s0, gn_tiled, nbatch, n, tw, chunk, need_y):
    s0_map = (lambda b: (0, 0, 0, 0)) if s0.shape[0] == 1 else (lambda b: (b, 0, 0, 0))
    kern = functools.partial(_gla_kernel, n=n, tw=tw, chunk=chunk, need_y=need_y)
    if not need_y:
        def kern(qvr_ref, kT_ref, laT_ref, s0_ref, gn_ref, sout_ref):
            _gla_kernel(qvr_ref, kT_ref, laT_ref, s0_ref, gn_ref, None, sout_ref,
                        n=n, tw=tw, chunk=chunk, need_y=False)
    state_shape = (GLA_HEADS, GLA_DK, GLA_DV)
    out_shape = [jax.ShapeDtypeStruct((nbatch,) + state_shape, F32)]
    out_specs = [pl.BlockSpec((1,) + state_shape, lambda b: (b, 0, 0, 0))]
    if need_y:
        out_shape = [jax.ShapeDtypeStruct((nbatch * n, GLA_V), BF16)] + out_shape
        out_specs = [pl.BlockSpec((n, GLA_V), lambda b: (b, 0))] + out_specs
    res = pl.pallas_call(
        kern, grid=(nbatch,),
        in_specs=[
            pl.BlockSpec((n, _C_FQ), lambda b: (b, 0)),
            pl.BlockSpec((1, n // tw, GLA_QK, tw), lambda b: (b, 0, 0, 0)),
            pl.BlockSpec((1, n // tw, GLA_QK, tw), lambda b: (b, 0, 0, 0)),
            pl.BlockSpec((1,) + state_shape, s0_map),
            _const_spec((1, GLA_V)),
        ],
        out_specs=out_specs, out_shape=out_shape,
        compiler_params=pltpu.CompilerParams(
            dimension_semantics=("arbitrary",), vmem_limit_bytes=VMEM_LIMIT),
        name="gla",
    )(qvr, kT, laT, s0, gn_tiled)
    return res if need_y else (None, res[0])


def _fold_lanes(x, op):
    w = x.shape[1]
    if w % LANES == 0:
        f = x[:, 0:LANES]
        for g in range(1, w // LANES):
            f = op(f, x[:, g * LANES:(g + 1) * LANES])
        return f
    assert op is jnp.add
    lane = lax.broadcasted_iota(jnp.int32, (x.shape[0], LANES), 1)
    return jnp.where(lane == 0, jnp.sum(x, axis=-1, keepdims=True), 0.0)


def _tile_lanes(x, w):
    return x[:, 0:w] if w < LANES else jnp.concatenate([x] * (w // LANES), axis=1)


def _fox_kernel(q_ref, kn_ref, vn_ref, kp_ref, vp_ref, crow_ref, cprow_ref, o_ref, ka_scr, va_scr, *scr,
                n, tq, tkp, p_pad, p_valid, pv_t, nhp):
    nq = n // tq
    rb = min(tq, 64)
    nh = 2 * nhp
    nbuf = len(scr) // (3 * nh)
    s_scr, p_scr, sh_scr = (scr[i * nh * nbuf:(i + 1) * nh * nbuf] for i in range(3))
    nparts = 3
    ones_rows = 16

    def c_block(c_row):
        parts = _split3(-c_row)
        r = lax.broadcasted_iota(jnp.int32, (FOX_DH, c_row.shape[1]), 0)
        blk = jnp.zeros((FOX_DH, c_row.shape[1]), F32)
        for i, part in enumerate(parts):
            blk = jnp.where(r == i, part.astype(F32), blk)
        return blk.astype(BF16)

    for h in range(nh):
        hp, hh = divmod(h, 2)
        hrows = slice(h * FOX_DH, (h + 1) * FOX_DH)
        for k_src, c_src, c0, w in ((kp_ref[0, h].astype(BF16), cprow_ref[0, hp, hh:hh + 1, :], 0, p_pad),
                                    (kn_ref[0, hrows, :], crow_ref[0, hp, hh:hh + 1, :], p_pad, n)):
            halves = (k_src, c_block(c_src)) if hh == 0 else (c_block(c_src), k_src)
            ka_scr[h, :, c0:c0 + w] = jnp.concatenate(halves, axis=0)
        va_scr[h, 0:FOX_DH, 0:p_pad] = vp_ref[0, h].astype(BF16)
        va_scr[h, 0:FOX_DH, p_pad:p_pad + n] = vn_ref[0, hrows, :]
        va_scr[h, FOX_DH:FOX_DH + ones_rows, :] = jnp.ones((ones_rows, p_pad + n), BF16)

    lane_q = lax.broadcasted_iota(jnp.int32, (tq, LANES), 1)
    eye = jnp.where(lax.broadcasted_iota(jnp.int32, (tq, tq), 0) == lax.broadcasted_iota(jnp.int32, (tq, tq), 1),
                    1.0, 0.0).astype(BF16)

    def block_chunks(qi):
        return [(c0, tkp) for c0 in range(0, p_pad, tkp)] + [(p_pad + j * tq, tq) for j in range(qi + 1)]

    def pass1_items(qi):
        rows = slice(qi * tq, (qi + 1) * tq)
        chunks = block_chunks(qi)
        state = {}

        def start():
            c_parts = [part for h in range(nh) for part in _split3(crow_ref[0, h // 2, h % 2:h % 2 + 1, rows])]
            pad_rows = -len(c_parts) % ones_rows
            c_rows = jnp.concatenate(c_parts + [jnp.zeros((pad_rows, tq), BF16)], axis=0)
            state["c_cols"] = _dot_nt(eye, c_rows)
            for h in range(nh):
                hp, hh = divmod(h, 2)
                qp = q_ref[rows, hp * LANES:(hp + 1) * LANES]
                c_lane0 = FOX_DH if hh == 0 else 0
                own = (lane_q < FOX_DH) if hh == 0 else (lane_q >= FOX_DH)
                ones_pat = jnp.where((lane_q >= c_lane0) & (lane_q < c_lane0 + nparts), 1.0, 0.0).astype(BF16)
                state["q", h] = jnp.where(own, qp, ones_pat)
                state["mx", h] = None

        def chunk(h, ci):
            c0, cw = chunks[ci]
            s = _dot(state["q", h], ka_scr[h, :, c0:c0 + cw])
            if c0 < p_pad and c0 + cw > p_valid:
                col = lax.broadcasted_iota(jnp.int32, s.shape, 1) + c0
                s = jnp.where(col < p_valid, s, NEG)
            if ci == len(chunks) - 1:
                r = lax.broadcasted_iota(jnp.int32, s.shape, 0)
                c = lax.broadcasted_iota(jnp.int32, s.shape, 1)
                s = jnp.where(c <= r, s, NEG)
            s_scr[buf(qi, h)][:, c0:c0 + cw] = s
            f = _fold_lanes(s, jnp.maximum) if cw % LANES == 0 else jnp.max(s, axis=-1, keepdims=True)
            mx = state["mx", h]
            if mx is None or mx.shape == f.shape:
                mx = f if mx is None else jnp.maximum(mx, f)
            else:
                mx = jnp.maximum(jnp.max(mx, axis=-1, keepdims=True), jnp.max(f, axis=-1, keepdims=True))
            state["mx", h] = mx

        def finish(h):
            cq = sum(state["c_cols"][:, h * nparts + i:h * nparts + i + 1] for i in range(nparts))
            m = jnp.max(state["mx", h], axis=-1, keepdims=True) + cq
            sh_scr[buf(qi, h)][...] = jnp.broadcast_to(m - cq, (tq, LANES))

        items = [start]
        for ci in range(len(chunks)):
            items += [functools.partial(chunk, h, ci) for h in range(nh)]
        return items + [functools.partial(finish, h) for h in range(nh)]

    def pass2_items(qi):
        def item(r0, h, c0, cw):
            rr = slice(r0, r0 + rb)
            sh = sh_scr[buf(qi, h)][rr, :]
            p = jnp.exp2(s_scr[buf(qi, h)][rr, c0:c0 + cw] - _tile_lanes(sh, cw))
            p_scr[buf(qi, h)][rr, c0:c0 + cw] = p.astype(BF16)
        return [functools.partial(item, r0, h, c0, cw)
                for r0 in range(0, tq, rb) for h in range(nh) for c0, cw in block_chunks(qi)]

    def pv_items(qi):
        w_tot = p_pad + (qi + 1) * tq
        acc = {}

        def matmul(h):
            p_all, v_aug = p_scr[buf(qi, h)][:, 0:w_tot], va_scr[h, :, 0:w_tot]
            acc[h] = _dot_nt(v_aug, p_all) if pv_t else _dot_nt(p_all, v_aug)

        def store():
            if pv_t:
                outs = [acc[h][0:FOX_DH] * (1.0 / acc[h][FOX_DH:FOX_DH + 1]) for h in range(nh)]
                o = jnp.concatenate(outs, axis=0).T
            else:
                outs = [acc[h][:, 0:FOX_DH] * (1.0 / acc[h][:, FOX_DH:FOX_DH + 1]) for h in range(nh)]
                o = jnp.concatenate(outs, axis=1)
            o_ref[qi * tq:(qi + 1) * tq, :] = o.astype(BF16)

        return [functools.partial(matmul, h) for h in range(nh)] + [store]

    def emit_merged(*lists):
        total = max(len(l) for l in lists)
        pos = [0] * len(lists)
        for t in range(total):
            for k, l in enumerate(lists):
                while pos[k] < len(l) and pos[k] * total <= t * len(l):
                    l[pos[k]]()
                    pos[k] += 1
        for k, l in enumerate(lists):
            for item in l[pos[k]:]:
                item()

    buf = lambda qi, h: (qi % nbuf) * nh + h
    emit_merged(pass1_items(0))
    for qi in range(nq):
        emit_merged(pass2_items(qi),
                    pass1_items(qi + 1) if qi + 1 < nq else [],
                    pv_items(qi - 1) if qi > 0 else [])
    emit_merged(pv_items(nq - 1))


def _fox(q, kn, vn, kp, vp, c_row, cp_row, nbatch, n, tq, tkp, p_valid, pv_t, nhp):
    nq = n // tq
    p_pad = kp.shape[3]
    pmap4 = (lambda b, hp: (0, hp, 0, 0)) if kp.shape[0] == 1 else (lambda b, hp: (b, hp, 0, 0))
    cmap4 = (lambda b, hp: (0, hp, 0, 0)) if cp_row.shape[0] == 1 else (lambda b, hp: (b, hp, 0, 0))
    nh = 2 * nhp
    tok = pl.BlockSpec((n, nhp * LANES), lambda b, hp: (b, hp))
    tlanes = pl.BlockSpec((1, nhp * LANES, n), lambda b, hp: (b, hp, 0))
    nbuf = min(nq, 2)
    return pl.pallas_call(
        functools.partial(_fox_kernel, n=n, tq=tq, tkp=tkp, p_pad=p_pad, p_valid=p_valid, pv_t=pv_t, nhp=nhp),
        grid=(nbatch, FOX_HEADS // nh),
        in_specs=[
            tok, tlanes, tlanes,
            pl.BlockSpec((1, nh, FOX_DH, p_pad), pmap4),
            pl.BlockSpec((1, nh, FOX_DH, p_pad), pmap4),
            pl.BlockSpec((1, nhp, 2, n), lambda b, hp: (b, hp, 0, 0)),
            pl.BlockSpec((1, nhp, 2, p_pad), cmap4),
        ],
        out_specs=tok,
        out_shape=jax.ShapeDtypeStruct((nbatch * n, FOX_W), BF16),
        scratch_shapes=([pltpu.VMEM((nh, LANES, p_pad + n), BF16), pltpu.VMEM((nh, FOX_DH + 16, p_pad + n), BF16)]
                        + [pltpu.VMEM((tq, p_pad + n), F32)] * (nh * nbuf)
                        + [pltpu.VMEM((tq, p_pad + n), BF16)] * (nh * nbuf)
                        + [pltpu.VMEM((tq, LANES), F32)] * (nh * nbuf)),
        compiler_params=pltpu.CompilerParams(
            dimension_semantics=("arbitrary", "arbitrary"), vmem_limit_bytes=VMEM_LIMIT),
        name="fox",
    )(q, kn, vn, kp, vp, c_row, cp_row)


def _merge_ffn_kernel(x_ref, ygla_ref, ofox_ref, sg_ref, wbg_ref, wbf_ref, wout_ref, gffn_ref,
                      wup_ref, wdown_ref, gfin_ref, y_ref, *, ff_chunk):
    ya = _dot(ygla_ref[...], wbg_ref[...])
    yb = _dot(ofox_ref[...], wbf_ref[...])
    m = sg_ref[:, 0:D_MODEL].astype(F32) * ya + sg_ref[:, D_MODEL:2 * D_MODEL].astype(F32) * yb
    h = x_ref[...] + _dot(m.astype(BF16), wout_ref[...])
    ms = jnp.mean(h * h, axis=-1, keepdims=True)
    hn = ((h * lax.rsqrt(ms + EPS)) * gffn_ref[...]).astype(BF16)
    for c in range(D_FF // ff_chunk):
        u = _dot(hn, wup_ref[:, c * ff_chunk:(c + 1) * ff_chunk])
        r = jnp.square(jnp.maximum(u, 0.0)).astype(BF16)
        h = h + _dot(r, wdown_ref[c * ff_chunk:(c + 1) * ff_chunk, :])
    ms = jnp.mean(h * h, axis=-1, keepdims=True)
    y_ref[...] = (h * lax.rsqrt(ms + EPS)) * gfin_ref[...]


def _merge_ffn(x2d, ygla, ofox, sg, w, tm):
    t_total = x2d.shape[0]
    tok = lambda width: pl.BlockSpec((tm, width), lambda i: (i, 0))
    return pl.pallas_call(
        functools.partial(_merge_ffn_kernel, ff_chunk=1024),
        grid=(t_total // tm,),
        in_specs=[
            tok(D_MODEL), tok(GLA_V), tok(FOX_W), tok(2 * D_MODEL),
            _const_spec((GLA_V, D_MODEL)), _const_spec((FOX_W, D_MODEL)),
            _const_spec((D_MODEL, D_MODEL)), _const_spec((1, D_MODEL)),
            _const_spec((D_MODEL, D_FF)), _const_spec((D_FF, D_MODEL)), _const_spec((1, D_MODEL)),
        ],
        out_specs=tok(D_MODEL),
        out_shape=jax.ShapeDtypeStruct((t_total, D_MODEL), F32),
        compiler_params=pltpu.CompilerParams(
            dimension_semantics=("arbitrary",), vmem_limit_bytes=VMEM_LIMIT),
        name="merge_ffn",
    )(x2d, ygla, ofox, sg, w["w_bg"], w["w_bf"], w["w_out"], w["gn_ffn"], w["w_up"], w["w_down"],
      w["gn_final"])


def _prep_weights(norm_mix, w_in, w_gla_gate, b_gla_gate, g_gla_norm, b_fox_forget, w_branch_gla,
                  w_branch_fox, w_out, norm_ffn, w_up, w_down, norm_final):
    sizes = (GLA_QK, GLA_QK, GLA_V, GLA_V, GLA_RANK, FOX_W, FOX_W, FOX_W, FOX_HEADS, D_MODEL, D_MODEL)
    offs = [0]
    for s in sizes:
        offs.append(offs[-1] + s)
    col = lambda i: w_in[:, offs[i]:offs[i + 1]]
    gq, gk, gv, gr, glr, fq, fk, fv, ff, ga, gb = (col(i) for i in range(11))
    w_nn = jnp.concatenate([gq * (GLA_DK ** -0.5), gv, gr, fq * (FOX_DH ** -0.5 * LOG2E), ga, gb],
                           axis=1).astype(BF16)
    pad = jnp.zeros((D_MODEL, _R_FK - _R_FF - FOX_HEADS), F32)
    w_t = jnp.concatenate([gk, glr, ff, pad, fk, fv], axis=1).T.astype(BF16)
    return dict(
        gn_mix=norm_mix.reshape(1, D_MODEL), w_nn=w_nn, w_t=w_t,
        w_gate_t=w_gla_gate.T.astype(BF16), b_gate_col=b_gla_gate.reshape(GLA_QK, 1),
        b_f_col=b_fox_forget.reshape(FOX_HEADS, 1),
        gn_gla=jnp.tile(g_gla_norm.reshape(1, GLA_DV), (1, GLA_HEADS)),
        w_bg=w_branch_gla.astype(BF16), w_bf=w_branch_fox.astype(BF16), w_out=w_out.astype(BF16),
        gn_ffn=norm_ffn.reshape(1, D_MODEL), w_up=w_up.astype(BF16), w_down=w_down.astype(BF16),
        gn_final=norm_final.reshape(1, D_MODEL),
    )


def _pad_lanes(x):
    n = x.shape[-1]
    npad = -(-n // LANES) * LANES
    return x if npad == n else jnp.pad(x, [(0, 0)] * (x.ndim - 1) + [(0, npad - n)])


def kernel(x_prompt, x_sample, cache_fox_k, cache_fox_v, cache_fox_logf, state_gla, meta_tokens,
           norm_mix, w_in, w_gla_gate, b_gla_gate, g_gla_norm, b_fox_forget, w_branch_gla,
           w_branch_fox, w_out, norm_ffn, w_up, w_down, norm_final):
    bsz, seq, _ = x_prompt.shape
    dbsz, dseq, _ = x_sample.shape
    past = cache_fox_k.shape[3]
    w = _prep_weights(norm_mix[0], w_in[0], w_gla_gate[0], b_gla_gate[0], g_gla_norm[0],
                      b_fox_forget[0], w_branch_gla[0], w_branch_fox[0], w_out[0], norm_ffn[0],
                      w_up[0], w_down[0], norm_final)
    xp = x_prompt.reshape(bsz * seq, D_MODEL)
    xs = x_sample.reshape(dbsz * dseq, D_MODEL)

    (m_qvr, m_kT, m_laT, m_logf, _, _, _, m_kT32, m_vT32, _) = _inproj(
        meta_tokens.astype(F32), 1, N_META, 1, N_META, N_META, w)
    zero_state = jnp.zeros((1, GLA_HEADS, GLA_DK, GLA_DV), F32)
    _, s_meta = _gla(m_qvr, m_kT, m_laT, zero_state, w["gn_gla"], 1, N_META, N_META, N_META, False)
    kp, vp = _pad_lanes(m_kT32), _pad_lanes(m_vT32)

    tm = 512
    lead = (kp.reshape(FOX_W, LANES), vp.reshape(FOX_W, LANES))
    (p_qvr, p_kT, p_laT, p_logf, p_fq, p_fkT, p_fvT, p_kT32, p_vT32, p_sg) = _inproj(
        xp, bsz, seq, 1, tm, LANES, w, lead=lead)
    y_gla, s_p = _gla(p_qvr, p_kT, p_laT, s_meta, w["gn_gla"], bsz, seq, LANES, CHUNK, True)
    lp = _pad_lanes(jnp.broadcast_to(m_logf, (bsz, FOX_HEADS, N_META)).reshape(bsz * FOX_HEADS, N_META))
    cp, cn = _decay(lp, p_logf.reshape(bsz * FOX_HEADS, seq), N_META)
    tq = 256
    c_row = cn.reshape(bsz, FOX_HEADS // 2, 2, seq)
    cp_row = cp[:FOX_HEADS].reshape(1, FOX_HEADS // 2, 2, LANES)
    o_fox = _fox(p_fq, p_fkT, p_fvT, kp, vp, c_row, cp_row, bsz, seq, tq, LANES, N_META, True, 2)
    y_prompt = _merge_ffn(xp, y_gla, o_fox, p_sg, w, tm).reshape(bsz, seq, D_MODEL)

    (s_qvr, s_kT, s_laT, s_logf, s_fq, s_fkT, s_fvT, s_kT32, s_vT32, s_sg) = _inproj(
        xs, dbsz, dseq, dbsz, dseq, dseq, w)
    ys_gla, s_s = _gla(s_qvr, s_kT, s_laT, state_gla[0].astype(F32), w["gn_gla"],
                       dbsz, dseq, dseq, dseq, True)
    cps, cns = _decay(cache_fox_logf[0].astype(F32).reshape(dbsz * FOX_HEADS, past),
                      _pad_lanes(s_logf.reshape(dbsz * FOX_HEADS, dseq)), past)
    cs_row = cns[:, :dseq].reshape(dbsz, FOX_HEADS // 2, 2, dseq)
    cps_row = cps.reshape(dbsz, FOX_HEADS // 2, 2, past)
    os_fox = _fox(s_fq, s_fkT, s_fvT, jnp.swapaxes(cache_fox_k[0], 2, 3).astype(F32),
                  jnp.swapaxes(cache_fox_v[0], 2, 3).astype(F32), cs_row, cps_row,
                  dbsz, dseq, dseq, 256, past, False, FOX_HEADS // 2)
    y_sample = _merge_ffn(xs, ys_gla, os_fox, s_sg, w, dbsz * dseq).reshape(dbsz, dseq, D_MODEL)

    to_state = lambda a: jnp.swapaxes(a, 2, 3)[None]
    new_fox_logf_prompt = jnp.concatenate(
        [jnp.broadcast_to(m_logf, (bsz, FOX_HEADS, N_META)), p_logf], axis=2)[None]
    return (y_prompt, y_sample, to_state(p_kT32), to_state(p_vT32), new_fox_logf_prompt,
            s_p[None], to_state(s_kT32), to_state(s_vT32), s_logf[None], s_s[None])
```

```python
import functools

import jax
import jax.numpy as jnp
from jax import lax
from jax.experimental import pallas as pl
from jax.experimental.pallas import tpu as pltpu

F32 = jnp.float32
BF16 = jnp.bfloat16

D_MODEL = 1024
N_META = 16
EPS = 1e-6
GLA_HEADS = 4
GLA_DK = 64
GLA_DV = 128
GLA_RANK = 16
GLA_TAU = 16.0
GLA_QK = GLA_HEADS * GLA_DK
GLA_V = GLA_HEADS * GLA_DV
FOX_HEADS = 8
FOX_DH = 64
FOX_W = FOX_HEADS * FOX_DH
D_FF = 4 * D_MODEL
NEG = -1e30
CHUNK = 64
LOG2E = 1.4426950408889634

LANES = 128
VMEM_LIMIT = 56 * 1024 * 1024

_C_GQ, _C_GV, _C_GR, _C_FQ, _C_GA, _C_GB, _C_END = 0, 256, 768, 1280, 1792, 2816, 3840
_R_K, _R_LR, _R_FF, _R_FK, _R_FV, _R_END = 0, 256, 272, 288, 800, 1312


def _dot(a, b):
    return jnp.dot(a, b, preferred_element_type=F32)


def _dot_nt(a, b):
    return lax.dot_general(a, b, (((1,), (1,)), ((), ())), preferred_element_type=F32)


def _split3(x):
    hi = x.astype(BF16)
    r = x - hi.astype(F32)
    mid = r.astype(BF16)
    lo = (r - mid.astype(F32)).astype(BF16)
    return hi, mid, lo


def _dot3(x, m):
    hi, mid, lo = _split3(x)
    return _dot(hi, m) + _dot(mid, m) + _dot(lo, m)


def _log_sigmoid(x):
    return jnp.minimum(x, 0.0) - jnp.log1p(jnp.exp(-jnp.abs(x)))


def _const_spec(shape):
    nd = len(shape)
    return pl.BlockSpec(shape, lambda *_: (0,) * nd, pipeline_mode=pl.Buffered(1))


def _inproj_kernel(x_ref, gn_ref, wnn_ref, wt_ref, wg_ref, bg_ref, bf_ref, *rest, nb, tmb, tw, nt, row_off):
    if row_off:
        lead_k_ref, lead_v_ref = rest[:2]
        rest = rest[2:]
    (qvr_ref, kT_ref, laT_ref, logf_ref, fq_ref, fkT_ref, fvT_ref, k32_ref, v32_ref, sg_ref) = rest[:10]
    carries = rest[10:]
    tm = nb * tmb

    def project():
        x = x_ref[...]
        ms = jnp.mean(x * x, axis=-1, keepdims=True)
        hn = ((x * lax.rsqrt(ms + EPS)) * gn_ref[...]).astype(BF16)

        zg = _dot_nt(wt_ref[_R_K:_R_FK, :], hn)
        kT = zg[_R_K:_R_LR].astype(BF16)
        glrT = zg[_R_LR:_R_FF].astype(BF16)
        laT = _log_sigmoid(_dot(wg_ref[...], glrT) + bg_ref[...]) * (1.0 / GLA_TAU)
        logf = _log_sigmoid(zg[_R_FF:_R_FF + FOX_HEADS] + bf_ref[...])
        zk = _dot_nt(wt_ref[_R_FK:_R_FV, :], hn)
        zv = _dot_nt(wt_ref[_R_FV:_R_END, :], hn)
        for bb in range(nb):
            r0 = bb * tmb
            logf_ref[bb] = logf[:, r0:r0 + tmb]
            fkT_ref[bb] = zk[:, r0:r0 + tmb].astype(BF16)
            fvT_ref[bb] = zv[:, r0:r0 + tmb].astype(BF16)
            for j in range(tmb // tw):
                kT_ref[bb, j] = kT[:, r0 + j * tw:r0 + (j + 1) * tw]
                laT_ref[bb, j] = laT[:, r0 + j * tw:r0 + (j + 1) * tw]
            if not row_off:
                for h in range(FOX_HEADS):
                    k32_ref[bb, h] = zk[h * FOX_DH:(h + 1) * FOX_DH, r0:r0 + tmb]
                    v32_ref[bb, h] = zv[h * FOX_DH:(h + 1) * FOX_DH, r0:r0 + tmb]
        if row_off:
            lane = lax.broadcasted_iota(jnp.int32, (FOX_W, LANES), 1)
            for z, carry, o_ref in ((zk, carries[0], k32_ref), (zv, carries[1], v32_ref)):
                rolled = pltpu.roll(z, row_off, axis=1)
                first = jnp.where(lane < row_off, carry[...], rolled[:, 0:LANES])
                carry[...] = rolled[:, 0:LANES]
                for h in range(FOX_HEADS):
                    hr = slice(h * FOX_DH, (h + 1) * FOX_DH)
                    o_ref[0, h, :, 0:LANES] = first[hr]
                    o_ref[0, h, :, LANES:tm] = rolled[hr, LANES:tm]

        for c0 in (_C_GA, _C_GB):
            sg_ref[:, c0 - _C_GA:c0 - _C_GA + D_MODEL] = jax.nn.sigmoid(
                _dot(hn, wnn_ref[:, c0:c0 + D_MODEL])).astype(BF16)
        zq = _dot(hn, wnn_ref[:, _C_GQ:_C_FQ])
        qvr_ref[:, _C_GQ:_C_GR] = zq[:, _C_GQ:_C_GR].astype(BF16)
        qvr_ref[:, _C_GR:_C_FQ] = jax.nn.silu(zq[:, _C_GR:_C_FQ]).astype(BF16)
        fq_ref[...] = _dot(hn, wnn_ref[:, _C_FQ:_C_GA]).astype(BF16)

    if not row_off:
        project()
        return

    i = pl.program_id(1)

    @pl.when(i == 0)
    def _():
        carries[0][...] = lead_k_ref[...]
        carries[1][...] = lead_v_ref[...]

    pl.when(i < nt)(project)

    @pl.when(i == nt)
    def _():
        for carry, o_ref in ((carries[0], k32_ref), (carries[1], v32_ref)):
            for h in range(FOX_HEADS):
                o_ref[0, h, :, 0:LANES] = carry[h * FOX_DH:(h + 1) * FOX_DH, :]


def _inproj(x2d, nbatch, n, nb, tmb, tw, w, lead=None):
    tm = nb * tmb
    ngrp, nt = nbatch // nb, n // tmb
    t_total = nbatch * n
    row_off = N_META if lead is not None else 0
    steps = nt + 1 if row_off else nt
    blk = (lambda i: jnp.minimum(i, nt - 1)) if row_off else (lambda i: i)
    tok = lambda width: pl.BlockSpec((tm, width), lambda g, i: (g * nt + blk(i), 0))
    out_shape = (
        jax.ShapeDtypeStruct((t_total, _C_FQ), BF16),
        jax.ShapeDtypeStruct((nbatch, n // tw, GLA_QK, tw), BF16),
        jax.ShapeDtypeStruct((nbatch, n // tw, GLA_QK, tw), F32),
        jax.ShapeDtypeStruct((nbatch, FOX_HEADS, n), F32),
        jax.ShapeDtypeStruct((t_total, FOX_W), BF16),
        jax.ShapeDtypeStruct((nbatch, FOX_W, n), BF16),
        jax.ShapeDtypeStruct((nbatch, FOX_W, n), BF16),
        jax.ShapeDtypeStruct((nbatch, FOX_HEADS, FOX_DH, row_off + n), F32),
        jax.ShapeDtypeStruct((nbatch, FOX_HEADS, FOX_DH, row_off + n), F32),
        jax.ShapeDtypeStruct((t_total, 2 * D_MODEL), BF16),
    )
    tblk = pl.BlockSpec((nb, tmb // tw, GLA_QK, tw), lambda g, i: (g, blk(i), 0, 0))
    lblk = pl.BlockSpec((nb, FOX_W, tmb), lambda g, i: (g, 0, blk(i)))
    sblk = pl.BlockSpec((nb, FOX_HEADS, FOX_DH, tmb), lambda g, i: (g, 0, 0, i))
    out_specs = (
        tok(_C_FQ), tblk, tblk,
        pl.BlockSpec((nb, FOX_HEADS, tmb), lambda g, i: (g, 0, blk(i))),
        tok(FOX_W), lblk, lblk, sblk, sblk, tok(2 * D_MODEL),
    )
    in_specs = [
        tok(D_MODEL),
        _const_spec((1, D_MODEL)),
        _const_spec((D_MODEL, _C_END)),
        _const_spec((_R_END, D_MODEL)),
        _const_spec((GLA_QK, GLA_RANK)),
        _const_spec((GLA_QK, 1)),
        _const_spec((FOX_HEADS, 1)),
    ]
    args = [x2d, w["gn_mix"], w["w_nn"], w["w_t"], w["w_gate_t"], w["b_gate_col"], w["b_f_col"]]
    scratch = []
    if row_off:
        assert nb == 1 and tmb % LANES == 0
        in_specs += [_const_spec((FOX_W, LANES))] * 2
        args += list(lead)
        scratch = [pltpu.VMEM((FOX_W, LANES), F32)] * 2
    return pl.pallas_call(
        functools.partial(_inproj_kernel, nb=nb, tmb=tmb, tw=tw, nt=nt, row_off=row_off),
        grid=(ngrp, steps), in_specs=in_specs, out_specs=out_specs, out_shape=out_shape,
        scratch_shapes=scratch,
        compiler_params=pltpu.CompilerParams(
            dimension_semantics=("arbitrary", "arbitrary"), vmem_limit_bytes=VMEM_LIMIT),
        name="inproj",
    )(*args)


def _cumsum_lanes(x, carry):
    n = x.shape[1]
    r = lax.broadcasted_iota(jnp.int32, (LANES, LANES), 0)
    c = lax.broadcasted_iota(jnp.int32, (LANES, LANES), 1)
    upper = jnp.where(r <= c, 1.0, 0.0).astype(BF16)
    outs = []
    for j in range(n // LANES):
        cj = _dot3(x[:, j * LANES:(j + 1) * LANES], upper) + carry
        outs.append(cj)
        carry = cj[:, LANES - 1:LANES]
    return jnp.concatenate(outs, axis=1), carry


def _decay_kernel(lp_ref, ln_ref, cp_ref, cn_ref, *, p_valid):
    lp = lp_ref[...]
    lane = lax.broadcasted_iota(jnp.int32, lp.shape, 1)
    lp = jnp.where(lane < p_valid, lp, 0.0)
    zero = jnp.zeros((lp.shape[0], 1), F32)
    cp, tot = _cumsum_lanes(lp, zero)
    cp_ref[...] = cp * LOG2E
    cn, _ = _cumsum_lanes(ln_ref[...], tot)
    cn_ref[...] = cn * LOG2E


def _decay(logf_p, logf_n, p_valid):
    r, pp = logf_p.shape
    npad = logf_n.shape[1]
    return pl.pallas_call(
        functools.partial(_decay_kernel, p_valid=p_valid),
        out_shape=(jax.ShapeDtypeStruct((r, pp), F32), jax.ShapeDtypeStruct((r, npad), F32)),
        compiler_params=pltpu.CompilerParams(vmem_limit_bytes=VMEM_LIMIT),
        name="decay",
    )(logf_p, logf_n)


def _gla_kernel(qvr_ref, kT_ref, laT_ref, s0_ref, y_ref, sout_ref, *, n, tw, chunk, need_y):
    nblk = n // tw
    ncb = tw // chunk

    sr = lax.broadcasted_iota(jnp.int32, (tw, LANES), 0)
    tr = lax.broadcasted_iota(jnp.int32, (tw, tw), 0)
    tc = lax.broadcasted_iota(jnp.int32, (tw, tw), 1)
    m_cum = ((tr // chunk) == (tc // chunk)) & (tr <= tc)
    m_all = jnp.concatenate(
        [jnp.where((sr // chunk) == ci, 1.0, 0.0) for ci in range(ncb)] + [jnp.where(m_cum, 1.0, 0.0)],
        axis=1).astype(BF16)
    stack_k = tw % LANES == 0
    m_rhs = jnp.concatenate([m_all] * 3, axis=0) if stack_k else m_all
    lane_chunk = lax.broadcasted_iota(jnp.int32, (GLA_QK, tw), 1) // chunk
    zero_blk = jnp.zeros((GLA_DK, GLA_DV), BF16)

    def group(jg, state):
        chunks = []
        for t in range(nbg):
            jb = jg * nbg + t
            hi, mid, lo = _split3(laT_ref[0, jb])
            if stack_k:
                g = _dot(jnp.concatenate([hi, mid, lo], axis=1), m_rhs)
            else:
                g = _dot(hi, m_rhs) + _dot(mid, m_rhs) + _dot(lo, m_rhs)
            tot = [g[:, ci * LANES:(ci + 1) * LANES] for ci in range(ncb)]
            g_cum = g[:, ncb * LANES:ncb * LANES + tw]
            g_tot = tot[0][:, 0:tw] if tw <= LANES else jnp.concatenate([tot[0]] * (tw // LANES), axis=1)
            for ci in range(1, ncb):
                g_tot = jnp.where(lane_chunk == ci, tot[ci][:, 0:tw], g_tot)
            kdec = (kT_ref[0, jb].astype(F32) * jnp.exp(g_tot - g_cum)).astype(BF16)
            for ci in range(ncb):
                rows = pl.ds(pl.multiple_of(jb * tw, tw) + ci * chunk, chunk)
                v = qvr_ref[rows, _C_GV:_C_GR]
                us = [_dot(kdec[h * GLA_DK:(h + 1) * GLA_DK, ci * chunk:(ci + 1) * chunk],
                           v[:, h * GLA_DV:(h + 1) * GLA_DV]) for h in range(GLA_HEADS)]
                chunks.append((rows, jnp.exp(tot[ci]), us))
        s_bds = []
        for rows, a, us in chunks:
            state = tuple(a[h * GLA_DK:(h + 1) * GLA_DK] * state[h] + us[h] for h in range(GLA_HEADS))
            if need_y:
                s_bds.append(jnp.concatenate(
                    [jnp.concatenate([state[h].astype(BF16) if hc == h else zero_blk
                                      for hc in range(GLA_HEADS)], axis=1) for h in range(GLA_HEADS)], axis=0))
        if need_y:
            for (rows, _, _), s_bd in zip(chunks, s_bds):
                o = _dot(qvr_ref[rows, _C_GQ:_C_GV], s_bd)
                gate = qvr_ref[rows, _C_GR:_C_FQ].astype(F32)
                parts = []
                for h in range(GLA_HEADS):
                    oh = o[:, h * GLA_DV:(h + 1) * GLA_DV]
                    msq = jnp.mean(oh * oh, axis=-1, keepdims=True)
                    parts.append(oh * lax.rsqrt(msq + EPS))
                y_ref[rows, :] = (jnp.concatenate(parts, axis=1) * gate).astype(BF16)
        return state

    nbg = 8 if nblk % 8 == 0 else 1
    state = tuple(s0_ref[0, h] for h in range(GLA_HEADS))
    state = lax.fori_loop(0, nblk // nbg, group, state)
    for h in range(GLA_HEADS):
        sout_ref[0, h] = state[h]


def _gla(qvr, kT, laT, s0, nbatch, n, tw, chunk, need_y):
    s0_map = (lambda b: (0, 0, 0, 0)) if s0.shape[0] == 1 else (lambda b: (b, 0, 0, 0))
    kern = functools.partial(_gla_kernel, n=n, tw=tw, chunk=chunk, need_y=need_y)
    if not need_y:
        def kern(qvr_ref, kT_ref, laT_ref, s0_ref, sout_ref):
            _gla_kernel(qvr_ref, kT_ref, laT_ref, s0_ref, None, sout_ref,
                        n=n, tw=tw, chunk=chunk, need_y=False)
    state_shape = (GLA_HEADS, GLA_DK, GLA_DV)
    out_shape = [jax.ShapeDtypeStruct((nbatch,) + state_shape, F32)]
    out_specs = [pl.BlockSpec((1,) + state_shape, lambda b: (b, 0, 0, 0))]
    if need_y:
        out_shape = [jax.ShapeDtypeStruct((nbatch * n, GLA_V), BF16)] + out_shape
        out_specs = [pl.BlockSpec((n, GLA_V), lambda b: (b, 0))] + out_specs
    res = pl.pallas_call(
        kern, grid=(nbatch,),
        in_specs=[
            pl.BlockSpec((n, _C_FQ), lambda b: (b, 0)),
            pl.BlockSpec((1, n // tw, GLA_QK, tw), lambda b: (b, 0, 0, 0)),
            pl.BlockSpec((1, n // tw, GLA_QK, tw), lambda b: (b, 0, 0, 0)),
            pl.BlockSpec((1,) + state_shape, s0_map),
        ],
        out_specs=out_specs, out_shape=out_shape,
        compiler_params=pltpu.CompilerParams(
            dimension_semantics=("arbitrary",), vmem_limit_bytes=VMEM_LIMIT),
        name="gla",
    )(qvr, kT, laT, s0)
    return res if need_y else (None, res[0])


def _fold_lanes(x, op):
    w = x.shape[1]
    if w % LANES == 0:
        f = x[:, 0:LANES]
        for g in range(1, w // LANES):
            f = op(f, x[:, g * LANES:(g + 1) * LANES])
        return f
    assert op is jnp.add
    lane = lax.broadcasted_iota(jnp.int32, (x.shape[0], LANES), 1)
    return jnp.where(lane == 0, jnp.sum(x, axis=-1, keepdims=True), 0.0)


def _tile_lanes(x, w):
    return x[:, 0:w] if w < LANES else jnp.concatenate([x] * (w // LANES), axis=1)


def _fox_kernel(q_ref, kn_ref, vn_ref, kp_ref, vp_ref, crow_ref, cprow_ref, o_ref, ka_scr, va_scr, *scr,
                n, tq, tkp, p_pad, p_valid, pv_t, nhp):
    nq = n // tq
    rb = min(tq, 64)
    nh = 2 * nhp
    nbuf = len(scr) // (3 * nh)
    s_scr, p_scr, sh_scr = (scr[i * nh * nbuf:(i + 1) * nh * nbuf] for i in range(3))
    nparts = 3
    ones_rows = 16

    def c_block(c_row):
        parts = _split3(-c_row)
        r = lax.broadcasted_iota(jnp.int32, (FOX_DH, c_row.shape[1]), 0)
        blk = jnp.zeros((FOX_DH, c_row.shape[1]), F32)
        for i, part in enumerate(parts):
            blk = jnp.where(r == i, part.astype(F32), blk)
        return blk.astype(BF16)

    for h in range(nh):
        hp, hh = divmod(h, 2)
        hrows = slice(h * FOX_DH, (h + 1) * FOX_DH)
        for k_src, c_src, c0, w in ((kp_ref[0, h].astype(BF16), cprow_ref[0, hp, hh:hh + 1, :], 0, p_pad),
                                    (kn_ref[0, hrows, :], crow_ref[0, hp, hh:hh + 1, :], p_pad, n)):
            halves = (k_src, c_block(c_src)) if hh == 0 else (c_block(c_src), k_src)
            ka_scr[h, :, c0:c0 + w] = jnp.concatenate(halves, axis=0)
        va_scr[h, 0:FOX_DH, 0:p_pad] = vp_ref[0, h].astype(BF16)
        va_scr[h, 0:FOX_DH, p_pad:p_pad + n] = vn_ref[0, hrows, :]
        va_scr[h, FOX_DH:FOX_DH + ones_rows, :] = jnp.ones((ones_rows, p_pad + n), BF16)

    lane_q = lax.broadcasted_iota(jnp.int32, (tq, LANES), 1)
    eye = jnp.where(lax.broadcasted_iota(jnp.int32, (tq, tq), 0) == lax.broadcasted_iota(jnp.int32, (tq, tq), 1),
                    1.0, 0.0).astype(BF16)

    def block_chunks(qi):
        return [(c0, tkp) for c0 in range(0, p_pad, tkp)] + [(p_pad + j * tq, tq) for j in range(qi + 1)]

    def pass1_items(qi):
        rows = slice(qi * tq, (qi + 1) * tq)
        chunks = block_chunks(qi)
        state = {}

        def start():
            c_parts = [part for h in range(nh) for part in _split3(crow_ref[0, h // 2, h % 2:h % 2 + 1, rows])]
            pad_rows = -len(c_parts) % ones_rows
            c_rows = jnp.concatenate(c_parts + [jnp.zeros((pad_rows, tq), BF16)], axis=0)
            state["c_cols"] = _dot_nt(eye, c_rows)
            for h in range(nh):
                hp, hh = divmod(h, 2)
                qp = q_ref[rows, hp * LANES:(hp + 1) * LANES]
                c_lane0 = FOX_DH if hh == 0 else 0
                own = (lane_q < FOX_DH) if hh == 0 else (lane_q >= FOX_DH)
                ones_pat = jnp.where((lane_q >= c_lane0) & (lane_q < c_lane0 + nparts), 1.0, 0.0).astype(BF16)
                state["q", h] = jnp.where(own, qp, ones_pat)
                state["mx", h] = None

        def chunk(h, ci):
            c0, cw = chunks[ci]
            s = _dot(state["q", h], ka_scr[h, :, c0:c0 + cw])
            if c0 < p_pad and c0 + cw > p_valid:
                col = lax.broadcasted_iota(jnp.int32, s.shape, 1) + c0
                s = jnp.where(col < p_valid, s, NEG)
            if ci == len(chunks) - 1:
                r = lax.broadcasted_iota(jnp.int32, s.shape, 0)
                c = lax.broadcasted_iota(jnp.int32, s.shape, 1)
                s = jnp.where(c <= r, s, NEG)
            s_scr[buf(qi, h)][:, c0:c0 + cw] = s
            f = _fold_lanes(s, jnp.maximum) if cw % LANES == 0 else jnp.max(s, axis=-1, keepdims=True)
            mx = state["mx", h]
            if mx is None or mx.shape == f.shape:
                mx = f if mx is None else jnp.maximum(mx, f)
            else:
                mx = jnp.maximum(jnp.max(mx, axis=-1, keepdims=True), jnp.max(f, axis=-1, keepdims=True))
            state["mx", h] = mx

        def finish(h):
            cq = sum(state["c_cols"][:, h * nparts + i:h * nparts + i + 1] for i in range(nparts))
            m = jnp.max(state["mx", h], axis=-1, keepdims=True) + cq
            sh_scr[buf(qi, h)][...] = jnp.broadcast_to(m - cq, (tq, LANES))

        items = [start]
        for ci in range(len(chunks)):
            items += [functools.partial(chunk, h, ci) for h in range(nh)]
        return items + [functools.partial(finish, h) for h in range(nh)]

    def pass2_items(qi):
        def item(r0, h, c0, cw):
            rr = slice(r0, r0 + rb)
            sh = sh_scr[buf(qi, h)][rr, :]
            p = jnp.exp2(s_scr[buf(qi, h)][rr, c0:c0 + cw] - _tile_lanes(sh, cw))
            p_scr[buf(qi, h)][rr, c0:c0 + cw] = p.astype(BF16)
        return [functools.partial(item, r0, h, c0, cw)
                for r0 in range(0, tq, rb) for h in range(nh) for c0, cw in block_chunks(qi)]

    def pv_items(qi):
        w_tot = p_pad + (qi + 1) * tq
        acc = {}

        def matmul(h):
            p_all, v_aug = p_scr[buf(qi, h)][:, 0:w_tot], va_scr[h, :, 0:w_tot]
            acc[h] = _dot_nt(v_aug, p_all) if pv_t else _dot_nt(p_all, v_aug)

        def store():
            if pv_t:
                outs = [acc[h][0:FOX_DH] * (1.0 / acc[h][FOX_DH:FOX_DH + 1]) for h in range(nh)]
                o = jnp.concatenate(outs, axis=0).T
            else:
                outs = [acc[h][:, 0:FOX_DH] * (1.0 / acc[h][:, FOX_DH:FOX_DH + 1]) for h in range(nh)]
                o = jnp.concatenate(outs, axis=1)
            o_ref[qi * tq:(qi + 1) * tq, :] = o.astype(BF16)

        return [functools.partial(matmul, h) for h in range(nh)] + [store]

    def emit_merged(*lists):
        total = max(len(l) for l in lists)
        pos = [0] * len(lists)
        for t in range(total):
            for k, l in enumerate(lists):
                while pos[k] < len(l) and pos[k] * total <= t * len(l):
                    l[pos[k]]()
                    pos[k] += 1
        for k, l in enumerate(lists):
            for item in l[pos[k]:]:
                item()

    buf = lambda qi, h: (qi % nbuf) * nh + h
    emit_merged(pass1_items(0))
    for qi in range(nq):
        emit_merged(pass2_items(qi),
                    pass1_items(qi + 1) if qi + 1 < nq else [],
                    pv_items(qi - 1) if qi > 0 else [])
    emit_merged(pv_items(nq - 1))


def _fox(q, kn, vn, kp, vp, c_row, cp_row, nbatch, n, tq, tkp, p_valid, pv_t, nhp):
    nq = n // tq
    p_pad = kp.shape[3]
    pmap4 = (lambda b, hp: (0, hp, 0, 0)) if kp.shape[0] == 1 else (lambda b, hp: (b, hp, 0, 0))
    cmap4 = (lambda b, hp: (0, hp, 0, 0)) if cp_row.shape[0] == 1 else (lambda b, hp: (b, hp, 0, 0))
    nh = 2 * nhp
    tok = pl.BlockSpec((n, nhp * LANES), lambda b, hp: (b, hp))
    tlanes = pl.BlockSpec((1, nhp * LANES, n), lambda b, hp: (b, hp, 0))
    nbuf = min(nq, 2)
    return pl.pallas_call(
        functools.partial(_fox_kernel, n=n, tq=tq, tkp=tkp, p_pad=p_pad, p_valid=p_valid, pv_t=pv_t, nhp=nhp),
        grid=(nbatch, FOX_HEADS // nh),
        in_specs=[
            tok, tlanes, tlanes,
            pl.BlockSpec((1, nh, FOX_DH, p_pad), pmap4),
            pl.BlockSpec((1, nh, FOX_DH, p_pad), pmap4),
            pl.BlockSpec((1, nhp, 2, n), lambda b, hp: (b, hp, 0, 0)),
            pl.BlockSpec((1, nhp, 2, p_pad), cmap4),
        ],
        out_specs=tok,
        out_shape=jax.ShapeDtypeStruct((nbatch * n, FOX_W), BF16),
        scratch_shapes=([pltpu.VMEM((nh, LANES, p_pad + n), BF16), pltpu.VMEM((nh, FOX_DH + 16, p_pad + n), BF16)]
                        + [pltpu.VMEM((tq, p_pad + n), F32)] * (nh * nbuf)
                        + [pltpu.VMEM((tq, p_pad + n), BF16)] * (nh * nbuf)
                        + [pltpu.VMEM((tq, LANES), F32)] * (nh * nbuf)),
        compiler_params=pltpu.CompilerParams(
            dimension_semantics=("arbitrary", "arbitrary"), vmem_limit_bytes=VMEM_LIMIT),
        name="fox",
    )(q, kn, vn, kp, vp, c_row, cp_row)


def _merge_ffn_kernel(x_ref, ygla_ref, ofox_ref, sg_ref, wbg_ref, wbf_ref, wout_ref, gffn_ref,
                      wup_ref, wdown_ref, gfin_ref, y_ref, *, ff_chunk):
    ya = _dot(ygla_ref[...], wbg_ref[...])
    yb = _dot(ofox_ref[...], wbf_ref[...])
    m = sg_ref[:, 0:D_MODEL].astype(F32) * ya + sg_ref[:, D_MODEL:2 * D_MODEL].astype(F32) * yb
    h = x_ref[...] + _dot(m.astype(BF16), wout_ref[...])
    ms = jnp.mean(h * h, axis=-1, keepdims=True)
    hn = ((h * lax.rsqrt(ms + EPS)) * gffn_ref[...]).astype(BF16)
    for c in range(D_FF // ff_chunk):
        u = _dot(hn, wup_ref[:, c * ff_chunk:(c + 1) * ff_chunk])
        r = jnp.square(jnp.maximum(u, 0.0)).astype(BF16)
        h = h + _dot(r, wdown_ref[c * ff_chunk:(c + 1) * ff_chunk, :])
    ms = jnp.mean(h * h, axis=-1, keepdims=True)
    y_ref[...] = (h * lax.rsqrt(ms + EPS)) * gfin_ref[...]


def _merge_ffn(x2d, ygla, ofox, sg, w, tm):
    t_total = x2d.shape[0]
    tok = lambda width: pl.BlockSpec((tm, width), lambda i: (i, 0))
    return pl.pallas_call(
        functools.partial(_merge_ffn_kernel, ff_chunk=1024),
        grid=(t_total // tm,),
        in_specs=[
            tok(D_MODEL), tok(GLA_V), tok(FOX_W), tok(2 * D_MODEL),
            _const_spec((GLA_V, D_MODEL)), _const_spec((FOX_W, D_MODEL)),
            _const_spec((D_MODEL, D_MODEL)), _const_spec((1, D_MODEL)),
            _const_spec((D_MODEL, D_FF)), _const_spec((D_FF, D_MODEL)), _const_spec((1, D_MODEL)),
        ],
        out_specs=tok(D_MODEL),
        out_shape=jax.ShapeDtypeStruct((t_total, D_MODEL), F32),
        compiler_params=pltpu.CompilerParams(
            dimension_semantics=("arbitrary",), vmem_limit_bytes=VMEM_LIMIT),
        name="merge_ffn",
    )(x2d, ygla, ofox, sg, w["w_bg"], w["w_bf"], w["w_out"], w["gn_ffn"], w["w_up"], w["w_down"],
      w["gn_final"])


def _prep_weights(norm_mix, w_in, w_gla_gate, b_gla_gate, g_gla_norm, b_fox_forget, w_branch_gla,
                  w_branch_fox, w_out, norm_ffn, w_up, w_down, norm_final):
    sizes = (GLA_QK, GLA_QK, GLA_V, GLA_V, GLA_RANK, FOX_W, FOX_W, FOX_W, FOX_HEADS, D_MODEL, D_MODEL)
    offs = [0]
    for s in sizes:
        offs.append(offs[-1] + s)
    col = lambda i: w_in[:, offs[i]:offs[i + 1]]
    gq, gk, gv, gr, glr, fq, fk, fv, ff, ga, gb = (col(i) for i in range(11))
    w_nn = jnp.concatenate([gq * (GLA_DK ** -0.5), gv, gr, fq * (FOX_DH ** -0.5 * LOG2E), ga, gb],
                           axis=1).astype(BF16)
    pad = jnp.zeros((D_MODEL, _R_FK - _R_FF - FOX_HEADS), F32)
    w_t = jnp.concatenate([gk, glr, ff, pad, fk, fv], axis=1).T.astype(BF16)
    return dict(
        gn_mix=norm_mix.reshape(1, D_MODEL), w_nn=w_nn, w_t=w_t,
        w_gate_t=w_gla_gate.T.astype(BF16), b_gate_col=b_gla_gate.reshape(GLA_QK, 1),
        b_f_col=b_fox_forget.reshape(FOX_HEADS, 1),
        w_bg=(jnp.tile(g_gla_norm, GLA_HEADS)[:, None] * w_branch_gla).astype(BF16), w_bf=w_branch_fox.astype(BF16), w_out=w_out.astype(BF16),
        gn_ffn=norm_ffn.reshape(1, D_MODEL), w_up=w_up.astype(BF16), w_down=w_down.astype(BF16),
        gn_final=norm_final.reshape(1, D_MODEL),
    )


def _pad_lanes(x):
    n = x.shape[-1]
    npad = -(-n // LANES) * LANES
    return x if npad == n else jnp.pad(x, [(0, 0)] * (x.ndim - 1) + [(0, npad - n)])


def kernel(x_prompt, x_sample, cache_fox_k, cache_fox_v, cache_fox_logf, state_gla, meta_tokens,
           norm_mix, w_in, w_gla_gate, b_gla_gate, g_gla_norm, b_fox_forget, w_branch_gla,
           w_branch_fox, w_out, norm_ffn, w_up, w_down, norm_final):
    bsz, seq, _ = x_prompt.shape
    dbsz, dseq, _ = x_sample.shape
    past = cache_fox_k.shape[3]
    w = _prep_weights(norm_mix[0], w_in[0], w_gla_gate[0], b_gla_gate[0], g_gla_norm[0],
                      b_fox_forget[0], w_branch_gla[0], w_branch_fox[0], w_out[0], norm_ffn[0],
                      w_up[0], w_down[0], norm_final)
    xp = x_prompt.reshape(bsz * seq, D_MODEL)
    xs = x_sample.reshape(dbsz * dseq, D_MODEL)

    (m_qvr, m_kT, m_laT, m_logf, _, _, _, m_kT32, m_vT32, _) = _inproj(
        meta_tokens.astype(F32), 1, N_META, 1, N_META, N_META, w)
    zero_state = jnp.zeros((1, GLA_HEADS, GLA_DK, GLA_DV), F32)
    _, s_meta = _gla(m_qvr, m_kT, m_laT, zero_state, 1, N_META, N_META, N_META, False)
    kp, vp = _pad_lanes(m_kT32), _pad_lanes(m_vT32)

    tm = 512
    lead = (kp.reshape(FOX_W, LANES), vp.reshape(FOX_W, LANES))
    (p_qvr, p_kT, p_laT, p_logf, p_fq, p_fkT, p_fvT, p_kT32, p_vT32, p_sg) = _inproj(
        xp, bsz, seq, 1, 2 * tm, LANES, w, lead=lead)
    y_gla, s_p = _gla(p_qvr, p_kT, p_laT, s_meta, bsz, seq, LANES, CHUNK, True)
    lp = _pad_lanes(jnp.broadcast_to(m_logf, (bsz, FOX_HEADS, N_META)).reshape(bsz * FOX_HEADS, N_META))
    cp, cn = _decay(lp, p_logf.reshape(bsz * FOX_HEADS, seq), N_META)
    tq = 256
    c_row = cn.reshape(bsz, FOX_HEADS // 2, 2, seq)
    cp_row = cp[:FOX_HEADS].reshape(1, FOX_HEADS // 2, 2, LANES)
    o_fox = _fox(p_fq, p_fkT, p_fvT, kp, vp, c_row, cp_row, bsz, seq, tq, LANES, N_META, True, 2)
    y_prompt = _merge_ffn(xp, y_gla, o_fox, p_sg, w, tm).reshape(bsz, seq, D_MODEL)

    (s_qvr, s_kT, s_laT, s_logf, s_fq, s_fkT, s_fvT, s_kT32, s_vT32, s_sg) = _inproj(
        xs, dbsz, dseq, dbsz, dseq, dseq, w)
    ys_gla, s_s = _gla(s_qvr, s_kT, s_laT, state_gla[0].astype(F32), dbsz, dseq, dseq, dseq, True)
    cps, cns = _decay(cache_fox_logf[0].astype(F32).reshape(dbsz * FOX_HEADS, past),
                      _pad_lanes(s_logf.reshape(dbsz * FOX_HEADS, dseq)), past)
    cs_row = cns[:, :dseq].reshape(dbsz, FOX_HEADS // 2, 2, dseq)
    cps_row = cps.reshape(dbsz, FOX_HEADS // 2, 2, past)
    os_fox = _fox(s_fq, s_fkT, s_fvT, jnp.swapaxes(cache_fox_k[0], 2, 3).astype(F32),
                  jnp.swapaxes(cache_fox_v[0], 2, 3).astype(F32), cs_row, cps_row,
                  dbsz, dseq, dseq, 256, past, False, FOX_HEADS // 2)
    y_sample = _merge_ffn(xs, ys_gla, os_fox, s_sg, w, dbsz * dseq).reshape(dbsz, dseq, D_MODEL)

    to_state = lambda a: jnp.swapaxes(a, 2, 3)[None]
    new_fox_logf_prompt = jnp.concatenate(
        [jnp.broadcast_to(m_logf, (bsz, FOX_HEADS, N_META)), p_logf], axis=2)[None]
    return (y_prompt, y_sample, to_state(p_kT32), to_state(p_vT32), new_fox_logf_prompt,
            s_p[None], to_state(s_kT32), to_state(s_vT32), s_logf[None], s_s[None])
```

```python
import functools

import jax
import jax.numpy as jnp
from jax import lax
from jax.experimental import pallas as pl
from jax.experimental.pallas import tpu as pltpu

F32 = jnp.float32
BF16 = jnp.bfloat16

D_MODEL = 1024
N_META = 16
EPS = 1e-6
GLA_HEADS = 4
GLA_DK = 64
GLA_DV = 128
GLA_RANK = 16
GLA_TAU = 16.0
GLA_QK = GLA_HEADS * GLA_DK
GLA_V = GLA_HEADS * GLA_DV
FOX_HEADS = 8
FOX_DH = 64
FOX_W = FOX_HEADS * FOX_DH
D_FF = 4 * D_MODEL
NEG = -1e30
CHUNK = 64
LOG2E = 1.4426950408889634

LANES = 128
VMEM_LIMIT = 56 * 1024 * 1024

_C_GQ, _C_GV, _C_GR, _C_FQ, _C_GA, _C_GB, _C_END = 0, 256, 768, 1280, 1792, 2816, 3840
_R_K, _R_LR, _R_FF, _R_FK, _R_FV, _R_END = 0, 256, 272, 288, 800, 1312


def _dot(a, b):
    return jnp.dot(a, b, preferred_element_type=F32)


def _dot_nt(a, b):
    return lax.dot_general(a, b, (((1,), (1,)), ((), ())), preferred_element_type=F32)


def _split3(x):
    hi = x.astype(BF16)
    r = x - hi.astype(F32)
    mid = r.astype(BF16)
    lo = (r - mid.astype(F32)).astype(BF16)
    return hi, mid, lo


def _dot3(x, m):
    hi, mid, lo = _split3(x)
    return _dot(hi, m) + _dot(mid, m) + _dot(lo, m)


def _log_sigmoid(x):
    return jnp.minimum(x, 0.0) - jnp.log1p(jnp.exp(-jnp.abs(x)))


def _const_spec(shape):
    nd = len(shape)
    return pl.BlockSpec(shape, lambda *_: (0,) * nd, pipeline_mode=pl.Buffered(1))


def _inproj_kernel(x_ref, gn_ref, wnn_ref, wt_ref, wg_ref, bg_ref, bf_ref, *rest, nb, tmb, tw, nt, row_off):
    if row_off:
        lead_k_ref, lead_v_ref = rest[:2]
        rest = rest[2:]
    (qvr_ref, kT_ref, laT_ref, logf_ref, fq_ref, fkT_ref, fvT_ref, k32_ref, v32_ref, sg_ref) = rest[:10]
    carries = rest[10:]
    tm = nb * tmb

    def project():
        x = x_ref[...]
        ms = jnp.mean(x * x, axis=-1, keepdims=True)
        hn = ((x * lax.rsqrt(ms + EPS)) * gn_ref[...]).astype(BF16)

        zT = _dot_nt(wt_ref[...], hn)
        kT = zT[_R_K:_R_LR].astype(BF16)
        glrT = zT[_R_LR:_R_FF].astype(BF16)
        laT = _log_sigmoid(_dot(wg_ref[...], glrT) + bg_ref[...]) * (1.0 / GLA_TAU)
        logf = _log_sigmoid(zT[_R_FF:_R_FF + FOX_HEADS] + bf_ref[...])
        zk, zv = zT[_R_FK:_R_FV], zT[_R_FV:_R_END]
        for bb in range(nb):
            r0 = bb * tmb
            logf_ref[bb] = logf[:, r0:r0 + tmb]
            fkT_ref[bb] = zk[:, r0:r0 + tmb].astype(BF16)
            fvT_ref[bb] = zv[:, r0:r0 + tmb].astype(BF16)
            for j in range(tmb // tw):
                kT_ref[bb, j] = kT[:, r0 + j * tw:r0 + (j + 1) * tw]
                laT_ref[bb, j] = laT[:, r0 + j * tw:r0 + (j + 1) * tw]
            if not row_off:
                for h in range(FOX_HEADS):
                    k32_ref[bb, h] = zk[h * FOX_DH:(h + 1) * FOX_DH, r0:r0 + tmb]
                    v32_ref[bb, h] = zv[h * FOX_DH:(h + 1) * FOX_DH, r0:r0 + tmb]
        if row_off:
            lane = lax.broadcasted_iota(jnp.int32, (FOX_W, LANES), 1)
            for z, carry, o_ref in ((zk, carries[0], k32_ref), (zv, carries[1], v32_ref)):
                rolled = pltpu.roll(z, row_off, axis=1)
                first = jnp.where(lane < row_off, carry[...], rolled[:, 0:LANES])
                carry[...] = rolled[:, 0:LANES]
                for h in range(FOX_HEADS):
                    hr = slice(h * FOX_DH, (h + 1) * FOX_DH)
                    o_ref[0, h, :, 0:LANES] = first[hr]
                    o_ref[0, h, :, LANES:tm] = rolled[hr, LANES:tm]

        sg_ref[...] = jax.nn.sigmoid(_dot(hn, wnn_ref[:, _C_GA:_C_END])).astype(BF16)
        zq = _dot(hn, wnn_ref[:, _C_GQ:_C_FQ])
        qvr_ref[:, _C_GQ:_C_GR] = zq[:, _C_GQ:_C_GR].astype(BF16)
        qvr_ref[:, _C_GR:_C_FQ] = jax.nn.silu(zq[:, _C_GR:_C_FQ]).astype(BF16)
        fq_ref[...] = _dot(hn, wnn_ref[:, _C_FQ:_C_GA]).astype(BF16)

    if not row_off:
        project()
        return

    i = pl.program_id(1)

    @pl.when(i == 0)
    def _():
        carries[0][...] = lead_k_ref[...]
        carries[1][...] = lead_v_ref[...]

    pl.when(i < nt)(project)

    @pl.when(i == nt)
    def _():
        for carry, o_ref in ((carries[0], k32_ref), (carries[1], v32_ref)):
            for h in range(FOX_HEADS):
                o_ref[0, h, :, 0:LANES] = carry[h * FOX_DH:(h + 1) * FOX_DH, :]


def _inproj(x2d, nbatch, n, nb, tmb, tw, w, lead=None):
    tm = nb * tmb
    ngrp, nt = nbatch // nb, n // tmb
    t_total = nbatch * n
    row_off = N_META if lead is not None else 0
    steps = nt + 1 if row_off else nt
    blk = (lambda i: jnp.minimum(i, nt - 1)) if row_off else (lambda i: i)
    tok = lambda width: pl.BlockSpec((tm, width), lambda g, i: (g * nt + blk(i), 0))
    out_shape = (
        jax.ShapeDtypeStruct((t_total, _C_FQ), BF16),
        jax.ShapeDtypeStruct((nbatch, n // tw, GLA_QK, tw), BF16),
        jax.ShapeDtypeStruct((nbatch, n // tw, GLA_QK, tw), F32),
        jax.ShapeDtypeStruct((nbatch, FOX_HEADS, n), F32),
        jax.ShapeDtypeStruct((t_total, FOX_W), BF16),
        jax.ShapeDtypeStruct((nbatch, FOX_W, n), BF16),
        jax.ShapeDtypeStruct((nbatch, FOX_W, n), BF16),
        jax.ShapeDtypeStruct((nbatch, FOX_HEADS, FOX_DH, row_off + n), F32),
        jax.ShapeDtypeStruct((nbatch, FOX_HEADS, FOX_DH, row_off + n), F32),
        jax.ShapeDtypeStruct((t_total, 2 * D_MODEL), BF16),
    )
    tblk = pl.BlockSpec((nb, tmb // tw, GLA_QK, tw), lambda g, i: (g, blk(i), 0, 0))
    lblk = pl.BlockSpec((nb, FOX_W, tmb), lambda g, i: (g, 0, blk(i)))
    sblk = pl.BlockSpec((nb, FOX_HEADS, FOX_DH, tmb), lambda g, i: (g, 0, 0, i))
    out_specs = (
        tok(_C_FQ), tblk, tblk,
        pl.BlockSpec((nb, FOX_HEADS, tmb), lambda g, i: (g, 0, blk(i))),
        tok(FOX_W), lblk, lblk, sblk, sblk, tok(2 * D_MODEL),
    )
    in_specs = [
        tok(D_MODEL),
        _const_spec((1, D_MODEL)),
        _const_spec((D_MODEL, _C_END)),
        _const_spec((_R_END, D_MODEL)),
        _const_spec((GLA_QK, GLA_RANK)),
        _const_spec((GLA_QK, 1)),
        _const_spec((FOX_HEADS, 1)),
    ]
    args = [x2d, w["gn_mix"], w["w_nn"], w["w_t"], w["w_gate_t"], w["b_gate_col"], w["b_f_col"]]
    scratch = []
    if row_off:
        assert nb == 1 and tmb % LANES == 0
        in_specs += [_const_spec((FOX_W, LANES))] * 2
        args += list(lead)
        scratch = [pltpu.VMEM((FOX_W, LANES), F32)] * 2
    return pl.pallas_call(
        functools.partial(_inproj_kernel, nb=nb, tmb=tmb, tw=tw, nt=nt, row_off=row_off),
        grid=(ngrp, steps), in_specs=in_specs, out_specs=out_specs, out_shape=out_shape,
        scratch_shapes=scratch,
        compiler_params=pltpu.CompilerParams(
            dimension_semantics=("arbitrary", "arbitrary"), vmem_limit_bytes=VMEM_LIMIT),
        name="inproj",
    )(*args)


def _cumsum_lanes(x, carry):
    n = x.shape[1]
    r = lax.broadcasted_iota(jnp.int32, (LANES, LANES), 0)
    c = lax.broadcasted_iota(jnp.int32, (LANES, LANES), 1)
    upper = jnp.where(r <= c, 1.0, 0.0).astype(BF16)
    outs = []
    for j in range(n // LANES):
        cj = _dot3(x[:, j * LANES:(j + 1) * LANES], upper) + carry
        outs.append(cj)
        carry = cj[:, LANES - 1:LANES]
    return jnp.concatenate(outs, axis=1), carry


def _decay_kernel(lp_ref, ln_ref, cp_ref, cn_ref, *, p_valid):
    lp = lp_ref[...]
    lane = lax.broadcasted_iota(jnp.int32, lp.shape, 1)
    lp = jnp.where(lane < p_valid, lp, 0.0)
    zero = jnp.zeros((lp.shape[0], 1), F32)
    cp, tot = _cumsum_lanes(lp, zero)
    cp_ref[...] = cp * LOG2E
    cn, _ = _cumsum_lanes(ln_ref[...], tot)
    cn_ref[...] = cn * LOG2E


def _decay(logf_p, logf_n, p_valid):
    r, pp = logf_p.shape
    npad = logf_n.shape[1]
    return pl.pallas_call(
        functools.partial(_decay_kernel, p_valid=p_valid),
        out_shape=(jax.ShapeDtypeStruct((r, pp), F32), jax.ShapeDtypeStruct((r, npad), F32)),
        compiler_params=pltpu.CompilerParams(vmem_limit_bytes=VMEM_LIMIT),
        name="decay",
    )(logf_p, logf_n)


def _gla_kernel(qvr_ref, kT_ref, laT_ref, s0_ref, y_ref, sout_ref, *, n, tw, chunk, need_y):
    nblk = n // tw
    ncb = tw // chunk

    sr = lax.broadcasted_iota(jnp.int32, (tw, LANES), 0)
    tr = lax.broadcasted_iota(jnp.int32, (tw, tw), 0)
    tc = lax.broadcasted_iota(jnp.int32, (tw, tw), 1)
    m_cum = ((tr // chunk) == (tc // chunk)) & (tr <= tc)
    m_all = jnp.concatenate(
        [jnp.where((sr // chunk) == ci, 1.0, 0.0) for ci in range(ncb)] + [jnp.where(m_cum, 1.0, 0.0)],
        axis=1).astype(BF16)
    stack_k = tw % LANES == 0
    m_rhs = jnp.concatenate([m_all] * 3, axis=0) if stack_k else m_all
    lane_chunk = lax.broadcasted_iota(jnp.int32, (GLA_QK, tw), 1) // chunk
    zero_blk = jnp.zeros((GLA_DK, GLA_DV), BF16)

    def group(jg, state):
        chunks = []
        for t in range(nbg):
            jb = jg * nbg + t
            hi, mid, lo = _split3(laT_ref[0, jb])
            if stack_k:
                g = _dot(jnp.concatenate([hi, mid, lo], axis=1), m_rhs)
            else:
                g = _dot(hi, m_rhs) + _dot(mid, m_rhs) + _dot(lo, m_rhs)
            tot = [g[:, ci * LANES:(ci + 1) * LANES] for ci in range(ncb)]
            g_cum = g[:, ncb * LANES:ncb * LANES + tw]
            g_tot = tot[0][:, 0:tw] if tw <= LANES else jnp.concatenate([tot[0]] * (tw // LANES), axis=1)
            for ci in range(1, ncb):
                g_tot = jnp.where(lane_chunk == ci, tot[ci][:, 0:tw], g_tot)
            kdec = (kT_ref[0, jb].astype(F32) * jnp.exp(g_tot - g_cum)).astype(BF16)
            for ci in range(ncb):
                rows = pl.ds(pl.multiple_of(jb * tw, tw) + ci * chunk, chunk)
                v = qvr_ref[rows, _C_GV:_C_GR]
                us = [_dot(kdec[h * GLA_DK:(h + 1) * GLA_DK, ci * chunk:(ci + 1) * chunk],
                           v[:, h * GLA_DV:(h + 1) * GLA_DV]) for h in range(GLA_HEADS)]
                chunks.append((rows, jnp.exp(tot[ci]), us))
        s_bds = []
        for rows, a, us in chunks:
            state = tuple(a[h * GLA_DK:(h + 1) * GLA_DK] * state[h] + us[h] for h in range(GLA_HEADS))
            if need_y:
                s_bds.append(jnp.concatenate(
                    [jnp.concatenate([state[h].astype(BF16) if hc == h else zero_blk
                                      for hc in range(GLA_HEADS)], axis=1) for h in range(GLA_HEADS)], axis=0))
        if need_y:
            for (rows, _, _), s_bd in zip(chunks, s_bds):
                o = _dot(qvr_ref[rows, _C_GQ:_C_GV], s_bd)
                gate = qvr_ref[rows, _C_GR:_C_FQ].astype(F32)
                parts = []
                for h in range(GLA_HEADS):
                    oh = o[:, h * GLA_DV:(h + 1) * GLA_DV]
                    msq = jnp.mean(oh * oh, axis=-1, keepdims=True)
                    parts.append(oh * lax.rsqrt(msq + EPS))
                y_ref[rows, :] = (jnp.concatenate(parts, axis=1) * gate).astype(BF16)
        return state

    nbg = 8 if nblk % 8 == 0 else 1
    state = tuple(s0_ref[0, h] for h in range(GLA_HEADS))
    state = lax.fori_loop(0, nblk // nbg, group, state)
    for h in range(GLA_HEADS):
        sout_ref[0, h] = state[h]


def _gla(qvr, kT, laT, s0, nbatch, n, tw, chunk, need_y):
    s0_map = (lambda b: (0, 0, 0, 0)) if s0.shape[0] == 1 else (lambda b: (b, 0, 0, 0))
    kern = functools.partial(_gla_kernel, n=n, tw=tw, chunk=chunk, need_y=need_y)
    if not need_y:
        def kern(qvr_ref, kT_ref, laT_ref, s0_ref, sout_ref):
            _gla_kernel(qvr_ref, kT_ref, laT_ref, s0_ref, None, sout_ref,
                        n=n, tw=tw, chunk=chunk, need_y=False)
    state_shape = (GLA_HEADS, GLA_DK, GLA_DV)
    out_shape = [jax.ShapeDtypeStruct((nbatch,) + state_shape, F32)]
    out_specs = [pl.BlockSpec((1,) + state_shape, lambda b: (b, 0, 0, 0))]
    if need_y:
        out_shape = [jax.ShapeDtypeStruct((nbatch * n, GLA_V), BF16)] + out_shape
        out_specs = [pl.BlockSpec((n, GLA_V), lambda b: (b, 0))] + out_specs
    res = pl.pallas_call(
        kern, grid=(nbatch,),
        in_specs=[
            pl.BlockSpec((n, _C_FQ), lambda b: (b, 0)),
            pl.BlockSpec((1, n // tw, GLA_QK, tw), lambda b: (b, 0, 0, 0)),
            pl.BlockSpec((1, n // tw, GLA_QK, tw), lambda b: (b, 0, 0, 0)),
            pl.BlockSpec((1,) + state_shape, s0_map),
        ],
        out_specs=out_specs, out_shape=out_shape,
        compiler_params=pltpu.CompilerParams(
            dimension_semantics=("arbitrary",), vmem_limit_bytes=VMEM_LIMIT),
        name="gla",
    )(qvr, kT, laT, s0)
    return res if need_y else (None, res[0])


def _fold_lanes(x, op):
    w = x.shape[1]
    if w % LANES == 0:
        f = x[:, 0:LANES]
        for g in range(1, w // LANES):
            f = op(f, x[:, g * LANES:(g + 1) * LANES])
        return f
    assert op is jnp.add
    lane = lax.broadcasted_iota(jnp.int32, (x.shape[0], LANES), 1)
    return jnp.where(lane == 0, jnp.sum(x, axis=-1, keepdims=True), 0.0)


def _tile_lanes(x, w):
    return x[:, 0:w] if w < LANES else jnp.concatenate([x] * (w // LANES), axis=1)


def _fox_kernel(q_ref, kn_ref, vn_ref, kp_ref, vp_ref, crow_ref, cprow_ref, o_ref, ka_scr, va_scr, *scr,
                n, tq, tkp, p_pad, p_valid, pv_t, nhp):
    nq = n // tq
    rb = min(tq, 64)
    nh = 2 * nhp
    nbuf = len(scr) // (3 * nh)
    s_scr, p_scr, sh_scr = (scr[i * nh * nbuf:(i + 1) * nh * nbuf] for i in range(3))
    nparts = 3
    ones_rows = 16

    def c_block(c_row):
        parts = _split3(-c_row)
        r = lax.broadcasted_iota(jnp.int32, (FOX_DH, c_row.shape[1]), 0)
        blk = jnp.zeros((FOX_DH, c_row.shape[1]), F32)
        for i, part in enumerate(parts):
            blk = jnp.where(r == i, part.astype(F32), blk)
        return blk.astype(BF16)

    for h in range(nh):
        hp, hh = divmod(h, 2)
        hrows = slice(h * FOX_DH, (h + 1) * FOX_DH)
        for k_src, c_src, c0, w in ((kp_ref[0, h].astype(BF16), cprow_ref[0, hp, hh:hh + 1, :], 0, p_pad),
                                    (kn_ref[0, hrows, :], crow_ref[0, hp, hh:hh + 1, :], p_pad, n)):
            halves = (k_src, c_block(c_src)) if hh == 0 else (c_block(c_src), k_src)
            ka_scr[h, :, c0:c0 + w] = jnp.concatenate(halves, axis=0)
        va_scr[h, 0:FOX_DH, 0:p_pad] = vp_ref[0, h].astype(BF16)
        va_scr[h, 0:FOX_DH, p_pad:p_pad + n] = vn_ref[0, hrows, :]
        va_scr[h, FOX_DH:FOX_DH + ones_rows, :] = jnp.ones((ones_rows, p_pad + n), BF16)

    lane_q = lax.broadcasted_iota(jnp.int32, (tq, LANES), 1)
    eye = jnp.where(lax.broadcasted_iota(jnp.int32, (tq, tq), 0) == lax.broadcasted_iota(jnp.int32, (tq, tq), 1),
                    1.0, 0.0).astype(BF16)

    def block_chunks(qi):
        return [(c0, tkp) for c0 in range(0, p_pad, tkp)] + [(p_pad + j * tq, tq) for j in range(qi + 1)]

    def pass1_items(qi):
        rows = slice(qi * tq, (qi + 1) * tq)
        chunks = block_chunks(qi)
        state = {}

        def start():
            c_parts = [part for h in range(nh) for part in _split3(crow_ref[0, h // 2, h % 2:h % 2 + 1, rows])]
            pad_rows = -len(c_parts) % ones_rows
            c_rows = jnp.concatenate(c_parts + [jnp.zeros((pad_rows, tq), BF16)], axis=0)
            state["c_cols"] = _dot_nt(eye, c_rows)
            for h in range(nh):
                hp, hh = divmod(h, 2)
                qp = q_ref[rows, hp * LANES:(hp + 1) * LANES]
                c_lane0 = FOX_DH if hh == 0 else 0
                own = (lane_q < FOX_DH) if hh == 0 else (lane_q >= FOX_DH)
                ones_pat = jnp.where((lane_q >= c_lane0) & (lane_q < c_lane0 + nparts), 1.0, 0.0).astype(BF16)
                state["q", h] = jnp.where(own, qp, ones_pat)
                state["mx", h] = None

        def chunk(h, ci):
            c0, cw = chunks[ci]
            s = _dot(state["q", h], ka_scr[h, :, c0:c0 + cw])
            if c0 < p_pad and c0 + cw > p_valid:
                col = lax.broadcasted_iota(jnp.int32, s.shape, 1) + c0
                s = jnp.where(col < p_valid, s, NEG)
            if ci == len(chunks) - 1:
                r = lax.broadcasted_iota(jnp.int32, s.shape, 0)
                c = lax.broadcasted_iota(jnp.int32, s.shape, 1)
                s = jnp.where(c <= r, s, NEG)
            s_scr[buf(qi, h)][:, c0:c0 + cw] = s
            f = _fold_lanes(s, jnp.maximum) if cw % LANES == 0 else jnp.max(s, axis=-1, keepdims=True)
            mx = state["mx", h]
            if mx is None or mx.shape == f.shape:
                mx = f if mx is None else jnp.maximum(mx, f)
            else:
                mx = jnp.maximum(jnp.max(mx, axis=-1, keepdims=True), jnp.max(f, axis=-1, keepdims=True))
            state["mx", h] = mx

        def finish(h):
            cq = sum(state["c_cols"][:, h * nparts + i:h * nparts + i + 1] for i in range(nparts))
            m = jnp.max(state["mx", h], axis=-1, keepdims=True) + cq
            sh_scr[buf(qi, h)][...] = jnp.broadcast_to(m - cq, (tq, LANES))

        items = [start]
        for ci in range(len(chunks)):
            items += [functools.partial(chunk, h, ci) for h in range(nh)]
        return items + [functools.partial(finish, h) for h in range(nh)]

    def pass2_items(qi):
        def item(r0, h, c0, cw):
            rr = slice(r0, r0 + rb)
            sh = sh_scr[buf(qi, h)][rr, :]
            p = jnp.exp2(s_scr[buf(qi, h)][rr, c0:c0 + cw] - _tile_lanes(sh, cw))
            p_scr[buf(qi, h)][rr, c0:c0 + cw] = p.astype(BF16)
        return [functools.partial(item, r0, h, c0, cw)
                for r0 in range(0, tq, rb) for h in range(nh) for c0, cw in block_chunks(qi)]

    def pv_items(qi):
        w_tot = p_pad + (qi + 1) * tq
        acc = {}

        def matmul(h):
            p_all, v_aug = p_scr[buf(qi, h)][:, 0:w_tot], va_scr[h, :, 0:w_tot]
            acc[h] = _dot_nt(v_aug, p_all) if pv_t else _dot_nt(p_all, v_aug)

        def store():
            if pv_t:
                outs = [acc[h][0:FOX_DH] * (1.0 / acc[h][FOX_DH:FOX_DH + 1]) for h in range(nh)]
                o = jnp.concatenate(outs, axis=0).T
            else:
                outs = [acc[h][:, 0:FOX_DH] * (1.0 / acc[h][:, FOX_DH:FOX_DH + 1]) for h in range(nh)]
                o = jnp.concatenate(outs, axis=1)
            o_ref[qi * tq:(qi + 1) * tq, :] = o.astype(BF16)

        return [functools.partial(matmul, h) for h in range(nh)] + [store]

    def emit_merged(*lists):
        total = max(len(l) for l in lists)
        pos = [0] * len(lists)
        for t in range(total):
            for k, l in enumerate(lists):
                while pos[k] < len(l) and pos[k] * total <= t * len(l):
                    l[pos[k]]()
                    pos[k] += 1
        for k, l in enumerate(lists):
            for item in l[pos[k]:]:
                item()

    buf = lambda qi, h: (qi % nbuf) * nh + h
    emit_merged(pass1_items(0))
    for qi in range(nq):
        emit_merged(pass2_items(qi),
                    pass1_items(qi + 1) if qi + 1 < nq else [],
                    pv_items(qi - 1) if qi > 0 else [])
    emit_merged(pv_items(nq - 1))


def _fox(q, kn, vn, kp, vp, c_row, cp_row, nbatch, n, tq, tkp, p_valid, pv_t, nhp):
    nq = n // tq
    p_pad = kp.shape[3]
    pmap4 = (lambda b, hp: (0, hp, 0, 0)) if kp.shape[0] == 1 else (lambda b, hp: (b, hp, 0, 0))
    cmap4 = (lambda b, hp: (0, hp, 0, 0)) if cp_row.shape[0] == 1 else (lambda b, hp: (b, hp, 0, 0))
    nh = 2 * nhp
    tok = pl.BlockSpec((n, nhp * LANES), lambda b, hp: (b, hp))
    tlanes = pl.BlockSpec((1, nhp * LANES, n), lambda b, hp: (b, hp, 0))
    nbuf = min(nq, 2)
    return pl.pallas_call(
        functools.partial(_fox_kernel, n=n, tq=tq, tkp=tkp, p_pad=p_pad, p_valid=p_valid, pv_t=pv_t, nhp=nhp),
        grid=(nbatch, FOX_HEADS // nh),
        in_specs=[
            tok, tlanes, tlanes,
            pl.BlockSpec((1, nh, FOX_DH, p_pad), pmap4),
            pl.BlockSpec((1, nh, FOX_DH, p_pad), pmap4),
            pl.BlockSpec((1, nhp, 2, n), lambda b, hp: (b, hp, 0, 0)),
            pl.BlockSpec((1, nhp, 2, p_pad), cmap4),
        ],
        out_specs=tok,
        out_shape=jax.ShapeDtypeStruct((nbatch * n, FOX_W), BF16),
        scratch_shapes=([pltpu.VMEM((nh, LANES, p_pad + n), BF16), pltpu.VMEM((nh, FOX_DH + 16, p_pad + n), BF16)]
                        + [pltpu.VMEM((tq, p_pad + n), F32)] * (nh * nbuf)
                        + [pltpu.VMEM((tq, p_pad + n), BF16)] * (nh * nbuf)
                        + [pltpu.VMEM((tq, LANES), F32)] * (nh * nbuf)),
        compiler_params=pltpu.CompilerParams(
            dimension_semantics=("arbitrary", "arbitrary"), vmem_limit_bytes=VMEM_LIMIT),
        name="fox",
    )(q, kn, vn, kp, vp, c_row, cp_row)


def _merge_ffn_kernel(x_ref, ygla_ref, ofox_ref, sg_ref, wbg_ref, wbf_ref, wout_ref, gffn_ref,
                      wup_ref, wdown_ref, gfin_ref, y_ref, *, ff_chunk):
    ya = _dot(ygla_ref[...], wbg_ref[...])
    yb = _dot(ofox_ref[...], wbf_ref[...])
    m = sg_ref[:, 0:D_MODEL].astype(F32) * ya + sg_ref[:, D_MODEL:2 * D_MODEL].astype(F32) * yb
    h = x_ref[...] + _dot(m.astype(BF16), wout_ref[...])
    ms = jnp.mean(h * h, axis=-1, keepdims=True)
    hn = ((h * lax.rsqrt(ms + EPS)) * gffn_ref[...]).astype(BF16)
    for c in range(D_FF // ff_chunk):
        u = _dot(hn, wup_ref[:, c * ff_chunk:(c + 1) * ff_chunk])
        r = jnp.square(jnp.maximum(u, 0.0)).astype(BF16)
        h = h + _dot(r, wdown_ref[c * ff_chunk:(c + 1) * ff_chunk, :])
    ms = jnp.mean(h * h, axis=-1, keepdims=True)
    y_ref[...] = (h * lax.rsqrt(ms + EPS)) * gfin_ref[...]


def _merge_ffn(x2d, ygla, ofox, sg, w, tm):
    t_total = x2d.shape[0]
    tok = lambda width: pl.BlockSpec((tm, width), lambda i: (i, 0))
    return pl.pallas_call(
        functools.partial(_merge_ffn_kernel, ff_chunk=1024),
        grid=(t_total // tm,),
        in_specs=[
            tok(D_MODEL), tok(GLA_V), tok(FOX_W), tok(2 * D_MODEL),
            _const_spec((GLA_V, D_MODEL)), _const_spec((FOX_W, D_MODEL)),
            _const_spec((D_MODEL, D_MODEL)), _const_spec((1, D_MODEL)),
            _const_spec((D_MODEL, D_FF)), _const_spec((D_FF, D_MODEL)), _const_spec((1, D_MODEL)),
        ],
        out_specs=tok(D_MODEL),
        out_shape=jax.ShapeDtypeStruct((t_total, D_MODEL), F32),
        compiler_params=pltpu.CompilerParams(
            dimension_semantics=("arbitrary",), vmem_limit_bytes=VMEM_LIMIT),
        name="merge_ffn",
    )(x2d, ygla, ofox, sg, w["w_bg"], w["w_bf"], w["w_out"], w["gn_ffn"], w["w_up"], w["w_down"],
      w["gn_final"])


def _prep_weights(norm_mix, w_in, w_gla_gate, b_gla_gate, g_gla_norm, b_fox_forget, w_branch_gla,
                  w_branch_fox, w_out, norm_ffn, w_up, w_down, norm_final):
    sizes = (GLA_QK, GLA_QK, GLA_V, GLA_V, GLA_RANK, FOX_W, FOX_W, FOX_W, FOX_HEADS, D_MODEL, D_MODEL)
    offs = [0]
    for s in sizes:
        offs.append(offs[-1] + s)
    col = lambda i: w_in[:, offs[i]:offs[i + 1]]
    gq, gk, gv, gr, glr, fq, fk, fv, ff, ga, gb = (col(i) for i in range(11))
    w_nn = jnp.concatenate([gq * (GLA_DK ** -0.5), gv, gr, fq * (FOX_DH ** -0.5 * LOG2E), ga, gb],
                           axis=1).astype(BF16)
    pad = jnp.zeros((D_MODEL, _R_FK - _R_FF - FOX_HEADS), F32)
    w_t = jnp.concatenate([gk, glr, ff, pad, fk, fv], axis=1).T.astype(BF16)
    return dict(
        gn_mix=norm_mix.reshape(1, D_MODEL), w_nn=w_nn, w_t=w_t,
        w_gate_t=w_gla_gate.T.astype(BF16), b_gate_col=b_gla_gate.reshape(GLA_QK, 1),
        b_f_col=b_fox_forget.reshape(FOX_HEADS, 1),
        w_bg=(jnp.tile(g_gla_norm, GLA_HEADS)[:, None] * w_branch_gla).astype(BF16), w_bf=w_branch_fox.astype(BF16), w_out=w_out.astype(BF16),
        gn_ffn=norm_ffn.reshape(1, D_MODEL), w_up=w_up.astype(BF16), w_down=w_down.astype(BF16),
        gn_final=norm_final.reshape(1, D_MODEL),
    )


def _pad_lanes(x):
    n = x.shape[-1]
    npad = -(-n // LANES) * LANES
    return x if npad == n else jnp.pad(x, [(0, 0)] * (x.ndim - 1) + [(0, npad - n)])


def kernel(x_prompt, x_sample, cache_fox_k, cache_fox_v, cache_fox_logf, state_gla, meta_tokens,
           norm_mix, w_in, w_gla_gate, b_gla_gate, g_gla_norm, b_fox_forget, w_branch_gla,
           w_branch_fox, w_out, norm_ffn, w_up, w_down, norm_final):
    bsz, seq, _ = x_prompt.shape
    dbsz, dseq, _ = x_sample.shape
    past = cache_fox_k.shape[3]
    w = _prep_weights(norm_mix[0], w_in[0], w_gla_gate[0], b_gla_gate[0], g_gla_norm[0],
                      b_fox_forget[0], w_branch_gla[0], w_branch_fox[0], w_out[0], norm_ffn[0],
                      w_up[0], w_down[0], norm_final)
    xp = x_prompt.reshape(bsz * seq, D_MODEL)
    xs = x_sample.reshape(dbsz * dseq, D_MODEL)

    (m_qvr, m_kT, m_laT, m_logf, _, _, _, m_kT32, m_vT32, _) = _inproj(
        meta_tokens.astype(F32), 1, N_META, 1, N_META, N_META, w)
    zero_state = jnp.zeros((1, GLA_HEADS, GLA_DK, GLA_DV), F32)
    _, s_meta = _gla(m_qvr, m_kT, m_laT, zero_state, 1, N_META, N_META, N_META, False)
    kp, vp = _pad_lanes(m_kT32), _pad_lanes(m_vT32)

    tm = 512
    lead = (kp.reshape(FOX_W, LANES), vp.reshape(FOX_W, LANES))
    (p_qvr, p_kT, p_laT, p_logf, p_fq, p_fkT, p_fvT, p_kT32, p_vT32, p_sg) = _inproj(
        xp, bsz, seq, 1, tm, LANES, w, lead=lead)
    y_gla, s_p = _gla(p_qvr, p_kT, p_laT, s_meta, bsz, seq, LANES, CHUNK, True)
    lp = _pad_lanes(jnp.broadcast_to(m_logf, (bsz, FOX_HEADS, N_META)).reshape(bsz * FOX_HEADS, N_META))
    cp, cn = _decay(lp, p_logf.reshape(bsz * FOX_HEADS, seq), N_META)
    tq = 256
    c_row = cn.reshape(bsz, FOX_HEADS // 2, 2, seq)
    cp_row = cp[:FOX_HEADS].reshape(1, FOX_HEADS // 2, 2, LANES)
    o_fox = _fox(p_fq, p_fkT, p_fvT, kp, vp, c_row, cp_row, bsz, seq, tq, LANES, N_META, True, 2)
    y_prompt = _merge_ffn(xp, y_gla, o_fox, p_sg, w, tm).reshape(bsz, seq, D_MODEL)

    (s_qvr, s_kT, s_laT, s_logf, s_fq, s_fkT, s_fvT, s_kT32, s_vT32, s_sg) = _inproj(
        xs, dbsz, dseq, dbsz, dseq, dseq, w)
    ys_gla, s_s = _gla(s_qvr, s_kT, s_laT, state_gla[0].astype(F32), dbsz, dseq, dseq, dseq, True)
    cps, cns = _decay(cache_fox_logf[0].astype(F32).reshape(dbsz * FOX_HEADS, past),
                      _pad_lanes(s_logf.reshape(dbsz * FOX_HEADS, dseq)), past)
    cs_row = cns[:, :dseq].reshape(dbsz, FOX_HEADS // 2, 2, dseq)
    cps_row = cps.reshape(dbsz, FOX_HEADS // 2, 2, past)
    os_fox = _fox(s_fq, s_fkT, s_fvT, jnp.swapaxes(cache_fox_k[0], 2, 3).astype(F32),
                  jnp.swapaxes(cache_fox_v[0], 2, 3).astype(F32), cs_row, cps_row,
                  dbsz, dseq, dseq, 256, past, False, FOX_HEADS // 2)
    y_sample = _merge_ffn(xs, ys_gla, os_fox, s_sg, w, dbsz * dseq).reshape(dbsz, dseq, D_MODEL)

    to_state = lambda a: jnp.swapaxes(a, 2, 3)[None]
    new_fox_logf_prompt = jnp.concatenate(
        [jnp.broadcast_to(m_logf, (bsz, FOX_HEADS, N_META)), p_logf], axis=2)[None]
    return (y_prompt, y_sample, to_state(p_kT32), to_state(p_vT32), new_fox_logf_prompt,
            s_p[None], to_state(s_kT32), to_state(s_vT32), s_logf[None], s_s[None])
```

```python
import functools

import jax
import jax.numpy as jnp
from jax import lax
from jax.experimental import pallas as pl
from jax.experimental.pallas import tpu as pltpu

F32 = jnp.float32
BF16 = jnp.bfloat16

D_MODEL = 1024
N_META = 16
EPS = 1e-6
GLA_HEADS = 4
GLA_DK = 64
GLA_DV = 128
GLA_RANK = 16
GLA_TAU = 16.0
GLA_QK = GLA_HEADS * GLA_DK
GLA_V = GLA_HEADS * GLA_DV
FOX_HEADS = 8
FOX_DH = 64
FOX_W = FOX_HEADS * FOX_DH
D_FF = 4 * D_MODEL
NEG = -1e30
CHUNK = 64
LOG2E = 1.4426950408889634

LANES = 128
VMEM_LIMIT = 56 * 1024 * 1024

_C_GQ, _C_GV, _C_GR, _C_FQ, _C_GA, _C_GB, _C_END = 0, 256, 768, 1280, 1792, 2816, 3840
_R_K, _R_LR, _R_FF, _R_FK, _R_FV, _R_END = 0, 256, 272, 288, 800, 1312


def _dot(a, b):
    return jnp.dot(a, b, preferred_element_type=F32)


def _dot_nt(a, b):
    return lax.dot_general(a, b, (((1,), (1,)), ((), ())), preferred_element_type=F32)


def _split3(x):
    hi = x.astype(BF16)
    r = x - hi.astype(F32)
    mid = r.astype(BF16)
    lo = (r - mid.astype(F32)).astype(BF16)
    return hi, mid, lo


def _dot3(x, m):
    hi, mid, lo = _split3(x)
    return _dot(hi, m) + _dot(mid, m) + _dot(lo, m)


def _log_sigmoid(x):
    return jnp.minimum(x, 0.0) - jnp.log1p(jnp.exp(-jnp.abs(x)))


def _const_spec(shape):
    nd = len(shape)
    return pl.BlockSpec(shape, lambda *_: (0,) * nd, pipeline_mode=pl.Buffered(1))


def _inproj_kernel(x_ref, gn_ref, wnn_ref, wt_ref, wg_ref, bg_ref, bf_ref, *rest, nb, tmb, tw, nt, row_off):
    if row_off:
        lead_k_ref, lead_v_ref = rest[:2]
        rest = rest[2:]
    (qvr_ref, kT_ref, laT_ref, logf_ref, fq_ref, fkT_ref, fvT_ref, k32_ref, v32_ref, sg_ref) = rest[:10]
    carries = rest[10:]
    tm = nb * tmb

    def project():
        x = x_ref[...]
        ms = jnp.mean(x * x, axis=-1, keepdims=True)
        hn = ((x * lax.rsqrt(ms + EPS)) * gn_ref[...]).astype(BF16)

        zT = _dot_nt(wt_ref[...], hn)
        kT = zT[_R_K:_R_LR].astype(BF16)
        glrT = zT[_R_LR:_R_FF].astype(BF16)
        laT = _log_sigmoid(_dot(wg_ref[...], glrT) + bg_ref[...]) * (1.0 / GLA_TAU)
        logf = _log_sigmoid(zT[_R_FF:_R_FF + FOX_HEADS] + bf_ref[...])
        zk, zv = zT[_R_FK:_R_FV], zT[_R_FV:_R_END]
        for bb in range(nb):
            r0 = bb * tmb
            logf_ref[bb] = logf[:, r0:r0 + tmb]
            fkT_ref[bb] = zk[:, r0:r0 + tmb].astype(BF16)
            fvT_ref[bb] = zv[:, r0:r0 + tmb].astype(BF16)
            for j in range(tmb // tw):
                kT_ref[bb, j] = kT[:, r0 + j * tw:r0 + (j + 1) * tw]
                laT_ref[bb, j] = laT[:, r0 + j * tw:r0 + (j + 1) * tw]
            if not row_off:
                for h in range(FOX_HEADS):
                    k32_ref[bb, h] = zk[h * FOX_DH:(h + 1) * FOX_DH, r0:r0 + tmb]
                    v32_ref[bb, h] = zv[h * FOX_DH:(h + 1) * FOX_DH, r0:r0 + tmb]
        if row_off:
            lane = lax.broadcasted_iota(jnp.int32, (FOX_W, LANES), 1)
            for z, carry, o_ref in ((zk, carries[0], k32_ref), (zv, carries[1], v32_ref)):
                rolled = pltpu.roll(z, row_off, axis=1)
                first = jnp.where(lane < row_off, carry[...], rolled[:, 0:LANES])
                carry[...] = rolled[:, 0:LANES]
                for h in range(FOX_HEADS):
                    hr = slice(h * FOX_DH, (h + 1) * FOX_DH)
                    o_ref[0, h, :, 0:LANES] = first[hr]
                    o_ref[0, h, :, LANES:tm] = rolled[hr, LANES:tm]

        sg_ref[...] = jax.nn.sigmoid(_dot(hn, wnn_ref[:, _C_GA:_C_END])).astype(BF16)
        zq = _dot(hn, wnn_ref[:, _C_GQ:_C_FQ])
        qvr_ref[:, _C_GQ:_C_GR] = zq[:, _C_GQ:_C_GR].astype(BF16)
        qvr_ref[:, _C_GR:_C_FQ] = jax.nn.silu(zq[:, _C_GR:_C_FQ]).astype(BF16)
        fq_ref[...] = _dot(hn, wnn_ref[:, _C_FQ:_C_GA]).astype(BF16)

    if not row_off:
        project()
        return

    i = pl.program_id(1)

    @pl.when(i == 0)
    def _():
        carries[0][...] = lead_k_ref[...]
        carries[1][...] = lead_v_ref[...]

    pl.when(i < nt)(project)

    @pl.when(i == nt)
    def _():
        for carry, o_ref in ((carries[0], k32_ref), (carries[1], v32_ref)):
            for h in range(FOX_HEADS):
                o_ref[0, h, :, 0:LANES] = carry[h * FOX_DH:(h + 1) * FOX_DH, :]


def _inproj(x2d, nbatch, n, nb, tmb, tw, w, lead=None):
    tm = nb * tmb
    ngrp, nt = nbatch // nb, n // tmb
    t_total = nbatch * n
    row_off = N_META if lead is not None else 0
    steps = nt + 1 if row_off else nt
    blk = (lambda i: jnp.minimum(i, nt - 1)) if row_off else (lambda i: i)
    tok = lambda width: pl.BlockSpec((tm, width), lambda g, i: (g * nt + blk(i), 0))
    out_shape = (
        jax.ShapeDtypeStruct((t_total, _C_FQ), BF16),
        jax.ShapeDtypeStruct((nbatch, n // tw, GLA_QK, tw), BF16),
        jax.ShapeDtypeStruct((nbatch, n // tw, GLA_QK, tw), F32),
        jax.ShapeDtypeStruct((nbatch, FOX_HEADS, n), F32),
        jax.ShapeDtypeStruct((t_total, FOX_W), BF16),
        jax.ShapeDtypeStruct((nbatch, FOX_W, n), BF16),
        jax.ShapeDtypeStruct((nbatch, FOX_W, n), BF16),
        jax.ShapeDtypeStruct((nbatch, FOX_HEADS, FOX_DH, row_off + n), F32),
        jax.ShapeDtypeStruct((nbatch, FOX_HEADS, FOX_DH, row_off + n), F32),
        jax.ShapeDtypeStruct((t_total, 2 * D_MODEL), BF16),
    )
    tblk = pl.BlockSpec((nb, tmb // tw, GLA_QK, tw), lambda g, i: (g, blk(i), 0, 0))
    lblk = pl.BlockSpec((nb, FOX_W, tmb), lambda g, i: (g, 0, blk(i)))
    sblk = pl.BlockSpec((nb, FOX_HEADS, FOX_DH, tmb), lambda g, i: (g, 0, 0, i))
    out_specs = (
        tok(_C_FQ), tblk, tblk,
        pl.BlockSpec((nb, FOX_HEADS, tmb), lambda g, i: (g, 0, blk(i))),
        tok(FOX_W), lblk, lblk, sblk, sblk, tok(2 * D_MODEL),
    )
    in_specs = [
        tok(D_MODEL),
        _const_spec((1, D_MODEL)),
        _const_spec((D_MODEL, _C_END)),
        _const_spec((_R_END, D_MODEL)),
        _const_spec((GLA_QK, GLA_RANK)),
        _const_spec((GLA_QK, 1)),
        _const_spec((FOX_HEADS, 1)),
    ]
    args = [x2d, w["gn_mix"], w["w_nn"], w["w_t"], w["w_gate_t"], w["b_gate_col"], w["b_f_col"]]
    scratch = []
    if row_off:
        assert nb == 1 and tmb % LANES == 0
        in_specs += [_const_spec((FOX_W, LANES))] * 2
        args += list(lead)
        scratch = [pltpu.VMEM((FOX_W, LANES), F32)] * 2
    return pl.pallas_call(
        functools.partial(_inproj_kernel, nb=nb, tmb=tmb, tw=tw, nt=nt, row_off=row_off),
        grid=(ngrp, steps), in_specs=in_specs, out_specs=out_specs, out_shape=out_shape,
        scratch_shapes=scratch,
        compiler_params=pltpu.CompilerParams(
            dimension_semantics=("arbitrary", "arbitrary"), vmem_limit_bytes=VMEM_LIMIT),
        name="inproj",
    )(*args)


def _cumsum_lanes(x, carry):
    n = x.shape[1]
    r = lax.broadcasted_iota(jnp.int32, (LANES, LANES), 0)
    c = lax.broadcasted_iota(jnp.int32, (LANES, LANES), 1)
    upper = jnp.where(r <= c, 1.0, 0.0).astype(BF16)
    outs = []
    for j in range(n // LANES):
        cj = _dot3(x[:, j * LANES:(j + 1) * LANES], upper) + carry
        outs.append(cj)
        carry = cj[:, LANES - 1:LANES]
    return jnp.concatenate(outs, axis=1), carry


def _decay_kernel(lp_ref, ln_ref, cp_ref, cn_ref, *, p_valid):
    lp = lp_ref[...]
    lane = lax.broadcasted_iota(jnp.int32, lp.shape, 1)
    lp = jnp.where(lane < p_valid, lp, 0.0)
    zero = jnp.zeros((lp.shape[0], 1), F32)
    cp, tot = _cumsum_lanes(lp, zero)
    cp_ref[...] = cp * LOG2E
    cn, _ = _cumsum_lanes(ln_ref[...], tot)
    cn_ref[...] = cn * LOG2E


def _decay(logf_p, logf_n, p_valid):
    r, pp = logf_p.shape
    npad = logf_n.shape[1]
    return pl.pallas_call(
        functools.partial(_decay_kernel, p_valid=p_valid),
        out_shape=(jax.ShapeDtypeStruct((r, pp), F32), jax.ShapeDtypeStruct((r, npad), F32)),
        compiler_params=pltpu.CompilerParams(vmem_limit_bytes=VMEM_LIMIT),
        name="decay",
    )(logf_p, logf_n)


def _gla_kernel(qvr_ref, kT_ref, laT_ref, s0_ref, y_ref, sout_ref, *, n, tw, chunk, need_y):
    nblk = n // tw
    ncb = tw // chunk

    sr = lax.broadcasted_iota(jnp.int32, (tw, LANES), 0)
    tr = lax.broadcasted_iota(jnp.int32, (tw, tw), 0)
    tc = lax.broadcasted_iota(jnp.int32, (tw, tw), 1)
    m_cum = ((tr // chunk) == (tc // chunk)) & (tr <= tc)
    m_all = jnp.concatenate(
        [jnp.where((sr // chunk) == ci, 1.0, 0.0) for ci in range(ncb)] + [jnp.where(m_cum, 1.0, 0.0)],
        axis=1).astype(BF16)
    stack_k = tw % LANES == 0
    m_rhs = jnp.concatenate([m_all] * 2, axis=0) if stack_k else m_all
    lane_chunk = lax.broadcasted_iota(jnp.int32, (GLA_QK, tw), 1) // chunk
    zero_blk = jnp.zeros((GLA_DK, GLA_DV), BF16)

    def group(jg, state):
        chunks = []
        for t in range(nbg):
            jb = jg * nbg + t
            la = laT_ref[0, jb]
            hi = la.astype(BF16)
            lo = (la - hi.astype(F32)).astype(BF16)
            if stack_k:
                g = _dot(jnp.concatenate([hi, lo], axis=1), m_rhs)
            else:
                g = _dot(hi, m_rhs) + _dot(lo, m_rhs)
            tot = [g[:, ci * LANES:(ci + 1) * LANES] for ci in range(ncb)]
            g_cum = g[:, ncb * LANES:ncb * LANES + tw]
            g_tot = tot[0][:, 0:tw] if tw <= LANES else jnp.concatenate([tot[0]] * (tw // LANES), axis=1)
            for ci in range(1, ncb):
                g_tot = jnp.where(lane_chunk == ci, tot[ci][:, 0:tw], g_tot)
            kdec = (kT_ref[0, jb].astype(F32) * jnp.exp(g_tot - g_cum)).astype(BF16)
            for ci in range(ncb):
                rows = pl.ds(pl.multiple_of(jb * tw, tw) + ci * chunk, chunk)
                v = qvr_ref[rows, _C_GV:_C_GR]
                us = [_dot(kdec[h * GLA_DK:(h + 1) * GLA_DK, ci * chunk:(ci + 1) * chunk],
                           v[:, h * GLA_DV:(h + 1) * GLA_DV]) for h in range(GLA_HEADS)]
                chunks.append((rows, jnp.exp(tot[ci]), us))
        s_bds = []
        for rows, a, us in chunks:
            state = tuple(a[h * GLA_DK:(h + 1) * GLA_DK] * state[h] + us[h] for h in range(GLA_HEADS))
            if need_y:
                s_bds.append(jnp.concatenate(
                    [jnp.concatenate([state[h].astype(BF16) if hc == h else zero_blk
                                      for hc in range(GLA_HEADS)], axis=1) for h in range(GLA_HEADS)], axis=0))
        if need_y:
            for (rows, _, _), s_bd in zip(chunks, s_bds):
                o = _dot(qvr_ref[rows, _C_GQ:_C_GV], s_bd)
                gate = qvr_ref[rows, _C_GR:_C_FQ].astype(F32)
                parts = []
                for h in range(GLA_HEADS):
                    oh = o[:, h * GLA_DV:(h + 1) * GLA_DV]
                    msq = jnp.mean(oh * oh, axis=-1, keepdims=True)
                    parts.append(oh * lax.rsqrt(msq + EPS))
                y_ref[rows, :] = (jnp.concatenate(parts, axis=1) * gate).astype(BF16)
        return state

    nbg = 8 if nblk % 8 == 0 else 1
    state = tuple(s0_ref[0, h] for h in range(GLA_HEADS))
    state = lax.fori_loop(0, nblk // nbg, group, state)
    for h in range(GLA_HEADS):
        sout_ref[0, h] = state[h]


def _gla(qvr, kT, laT, s0, nbatch, n, tw, chunk, need_y):
    s0_map = (lambda b: (0, 0, 0, 0)) if s0.shape[0] == 1 else (lambda b: (b, 0, 0, 0))
    kern = functools.partial(_gla_kernel, n=n, tw=tw, chunk=chunk, need_y=need_y)
    if not need_y:
        def kern(qvr_ref, kT_ref, laT_ref, s0_ref, sout_ref):
            _gla_kernel(qvr_ref, kT_ref, laT_ref, s0_ref, None, sout_ref,
                        n=n, tw=tw, chunk=chunk, need_y=False)
    state_shape = (GLA_HEADS, GLA_DK, GLA_DV)
    out_shape = [jax.ShapeDtypeStruct((nbatch,) + state_shape, F32)]
    out_specs = [pl.BlockSpec((1,) + state_shape, lambda b: (b, 0, 0, 0))]
    if need_y:
        out_shape = [jax.ShapeDtypeStruct((nbatch * n, GLA_V), BF16)] + out_shape
        out_specs = [pl.BlockSpec((n, GLA_V), lambda b: (b, 0))] + out_specs
    res = pl.pallas_call(
        kern, grid=(nbatch,),
        in_specs=[
            pl.BlockSpec((n, _C_FQ), lambda b: (b, 0)),
            pl.BlockSpec((1, n // tw, GLA_QK, tw), lambda b: (b, 0, 0, 0)),
            pl.BlockSpec((1, n // tw, GLA_QK, tw), lambda b: (b, 0, 0, 0)),
            pl.BlockSpec((1,) + state_shape, s0_map),
        ],
        out_specs=out_specs, out_shape=out_shape,
        compiler_params=pltpu.CompilerParams(
            dimension_semantics=("arbitrary",), vmem_limit_bytes=VMEM_LIMIT),
        name="gla",
    )(qvr, kT, laT, s0)
    return res if need_y else (None, res[0])


def _fold_lanes(x, op):
    w = x.shape[1]
    if w % LANES == 0:
        f = x[:, 0:LANES]
        for g in range(1, w // LANES):
            f = op(f, x[:, g * LANES:(g + 1) * LANES])
        return f
    assert op is jnp.add
    lane = lax.broadcasted_iota(jnp.int32, (x.shape[0], LANES), 1)
    return jnp.where(lane == 0, jnp.sum(x, axis=-1, keepdims=True), 0.0)


def _tile_lanes(x, w):
    return x[:, 0:w] if w < LANES else jnp.concatenate([x] * (w // LANES), axis=1)


def _fox_kernel(q_ref, kn_ref, vn_ref, kp_ref, vp_ref, crow_ref, cprow_ref, o_ref, ka_scr, va_scr, *scr,
                n, tq, tkp, p_pad, p_valid, pv_t, nhp):
    nq = n // tq
    rb = min(tq, 64)
    nh = 2 * nhp
    nbuf = len(scr) // (3 * nh)
    s_scr, p_scr, sh_scr = (scr[i * nh * nbuf:(i + 1) * nh * nbuf] for i in range(3))
    nparts = 3
    ones_rows = 16

    def c_block(c_row):
        parts = _split3(-c_row)
        r = lax.broadcasted_iota(jnp.int32, (FOX_DH, c_row.shape[1]), 0)
        blk = jnp.zeros((FOX_DH, c_row.shape[1]), F32)
        for i, part in enumerate(parts):
            blk = jnp.where(r == i, part.astype(F32), blk)
        return blk.astype(BF16)

    for h in range(nh):
        hp, hh = divmod(h, 2)
        hrows = slice(h * FOX_DH, (h + 1) * FOX_DH)
        for k_src, c_src, c0, w in ((kp_ref[0, h].astype(BF16), cprow_ref[0, hp, hh:hh + 1, :], 0, p_pad),
                                    (kn_ref[0, hrows, :], crow_ref[0, hp, hh:hh + 1, :], p_pad, n)):
            halves = (k_src, c_block(c_src)) if hh == 0 else (c_block(c_src), k_src)
            ka_scr[h, :, c0:c0 + w] = jnp.concatenate(halves, axis=0)
        va_scr[h, 0:FOX_DH, 0:p_pad] = vp_ref[0, h].astype(BF16)
        va_scr[h, 0:FOX_DH, p_pad:p_pad + n] = vn_ref[0, hrows, :]
        va_scr[h, FOX_DH:FOX_DH + ones_rows, :] = jnp.ones((ones_rows, p_pad + n), BF16)

    lane_q = lax.broadcasted_iota(jnp.int32, (tq, LANES), 1)
    eye = jnp.where(lax.broadcasted_iota(jnp.int32, (tq, tq), 0) == lax.broadcasted_iota(jnp.int32, (tq, tq), 1),
                    1.0, 0.0).astype(BF16)

    def block_chunks(qi):
        return [(c0, tkp) for c0 in range(0, p_pad, tkp)] + [(p_pad + j * tq, tq) for j in range(qi + 1)]

    def pass1_items(qi):
        rows = slice(qi * tq, (qi + 1) * tq)
        chunks = block_chunks(qi)
        state = {}

        def start():
            c_parts = [part for h in range(nh) for part in _split3(crow_ref[0, h // 2, h % 2:h % 2 + 1, rows])]
            pad_rows = -len(c_parts) % ones_rows
            c_rows = jnp.concatenate(c_parts + [jnp.zeros((pad_rows, tq), BF16)], axis=0)
            state["c_cols"] = _dot_nt(eye, c_rows)
            for h in range(nh):
                hp, hh = divmod(h, 2)
                qp = q_ref[rows, hp * LANES:(hp + 1) * LANES]
                c_lane0 = FOX_DH if hh == 0 else 0
                own = (lane_q < FOX_DH) if hh == 0 else (lane_q >= FOX_DH)
                ones_pat = jnp.where((lane_q >= c_lane0) & (lane_q < c_lane0 + nparts), 1.0, 0.0).astype(BF16)
                state["q", h] = jnp.where(own, qp, ones_pat)
                state["mx", h] = None

        def chunk(h, ci):
            c0, cw = chunks[ci]
            s = _dot(state["q", h], ka_scr[h, :, c0:c0 + cw])
            if c0 < p_pad and c0 + cw > p_valid:
                col = lax.broadcasted_iota(jnp.int32, s.shape, 1) + c0
                s = jnp.where(col < p_valid, s, NEG)
            if ci == len(chunks) - 1:
                r = lax.broadcasted_iota(jnp.int32, s.shape, 0)
                c = lax.broadcasted_iota(jnp.int32, s.shape, 1)
                s = jnp.where(c <= r, s, NEG)
            s_scr[buf(qi, h)][:, c0:c0 + cw] = s
            f = _fold_lanes(s, jnp.maximum) if cw % LANES == 0 else jnp.max(s, axis=-1, keepdims=True)
            mx = state["mx", h]
            if mx is None or mx.shape == f.shape:
                mx = f if mx is None else jnp.maximum(mx, f)
            else:
                mx = jnp.maximum(jnp.max(mx, axis=-1, keepdims=True), jnp.max(f, axis=-1, keepdims=True))
            state["mx", h] = mx

        def finish(h):
            cq = sum(state["c_cols"][:, h * nparts + i:h * nparts + i + 1] for i in range(nparts))
            m = jnp.max(state["mx", h], axis=-1, keepdims=True) + cq
            sh_scr[buf(qi, h)][...] = jnp.broadcast_to(m - cq, (tq, LANES))

        items = [start]
        for ci in range(len(chunks)):
            items += [functools.partial(chunk, h, ci) for h in range(nh)]
        return items + [functools.partial(finish, h) for h in range(nh)]

    def pass2_items(qi):
        def item(r0, h, c0, cw):
            rr = slice(r0, r0 + rb)
            sh = sh_scr[buf(qi, h)][rr, :]
            p = jnp.exp2(s_scr[buf(qi, h)][rr, c0:c0 + cw] - _tile_lanes(sh, cw))
            p_scr[buf(qi, h)][rr, c0:c0 + cw] = p.astype(BF16)
        return [functools.partial(item, r0, h, c0, cw)
                for r0 in range(0, tq, rb) for h in range(nh) for c0, cw in block_chunks(qi)]

    def pv_items(qi):
        w_tot = p_pad + (qi + 1) * tq
        acc = {}

        def matmul(h):
            p_all, v_aug = p_scr[buf(qi, h)][:, 0:w_tot], va_scr[h, :, 0:w_tot]
            acc[h] = _dot_nt(v_aug, p_all) if pv_t else _dot_nt(p_all, v_aug)

        def store():
            if pv_t:
                outs = [acc[h][0:FOX_DH] * (1.0 / acc[h][FOX_DH:FOX_DH + 1]) for h in range(nh)]
                o = jnp.concatenate(outs, axis=0).T
            else:
                outs = [acc[h][:, 0:FOX_DH] * (1.0 / acc[h][:, FOX_DH:FOX_DH + 1]) for h in range(nh)]
                o = jnp.concatenate(outs, axis=1)
            o_ref[qi * tq:(qi + 1) * tq, :] = o.astype(BF16)

        return [functools.partial(matmul, h) for h in range(nh)] + [store]

    def emit_merged(*lists):
        total = max(len(l) for l in lists)
        pos = [0] * len(lists)
        for t in range(total):
            for k, l in enumerate(lists):
                while pos[k] < len(l) and pos[k] * total <= t * len(l):
                    l[pos[k]]()
                    pos[k] += 1
        for k, l in enumerate(lists):
            for item in l[pos[k]:]:
                item()

    buf = lambda qi, h: (qi % nbuf) * nh + h
    emit_merged(pass1_items(0))
    for qi in range(nq):
        emit_merged(pass2_items(qi),
                    pass1_items(qi + 1) if qi + 1 < nq else [],
                    pv_items(qi - 1) if qi > 0 else [])
    emit_merged(pv_items(nq - 1))


def _fox(q, kn, vn, kp, vp, c_row, cp_row, nbatch, n, tq, tkp, p_valid, pv_t, nhp):
    nq = n // tq
    p_pad = kp.shape[3]
    pmap4 = (lambda b, hp: (0, hp, 0, 0)) if kp.shape[0] == 1 else (lambda b, hp: (b, hp, 0, 0))
    cmap4 = (lambda b, hp: (0, hp, 0, 0)) if cp_row.shape[0] == 1 else (lambda b, hp: (b, hp, 0, 0))
    nh = 2 * nhp
    tok = pl.BlockSpec((n, nhp * LANES), lambda b, hp: (b, hp))
    tlanes = pl.BlockSpec((1, nhp * LANES, n), lambda b, hp: (b, hp, 0))
    nbuf = min(nq, 2)
    return pl.pallas_call(
        functools.partial(_fox_kernel, n=n, tq=tq, tkp=tkp, p_pad=p_pad, p_valid=p_valid, pv_t=pv_t, nhp=nhp),
        grid=(nbatch, FOX_HEADS // nh),
        in_specs=[
            tok, tlanes, tlanes,
            pl.BlockSpec((1, nh, FOX_DH, p_pad), pmap4),
            pl.BlockSpec((1, nh, FOX_DH, p_pad), pmap4),
            pl.BlockSpec((1, nhp, 2, n), lambda b, hp: (b, hp, 0, 0)),
            pl.BlockSpec((1, nhp, 2, p_pad), cmap4),
        ],
        out_specs=tok,
        out_shape=jax.ShapeDtypeStruct((nbatch * n, FOX_W), BF16),
        scratch_shapes=([pltpu.VMEM((nh, LANES, p_pad + n), BF16), pltpu.VMEM((nh, FOX_DH + 16, p_pad + n), BF16)]
                        + [pltpu.VMEM((tq, p_pad + n), F32)] * (nh * nbuf)
                        + [pltpu.VMEM((tq, p_pad + n), BF16)] * (nh * nbuf)
                        + [pltpu.VMEM((tq, LANES), F32)] * (nh * nbuf)),
        compiler_params=pltpu.CompilerParams(
            dimension_semantics=("arbitrary", "arbitrary"), vmem_limit_bytes=VMEM_LIMIT),
        name="fox",
    )(q, kn, vn, kp, vp, c_row, cp_row)


def _merge_ffn_kernel(x_ref, ygla_ref, ofox_ref, sg_ref, wbg_ref, wbf_ref, wout_ref, gffn_ref,
                      wup_ref, wdown_ref, gfin_ref, y_ref, *, ff_chunk):
    ya = _dot(ygla_ref[...], wbg_ref[...])
    yb = _dot(ofox_ref[...], wbf_ref[...])
    m = sg_ref[:, 0:D_MODEL].astype(F32) * ya + sg_ref[:, D_MODEL:2 * D_MODEL].astype(F32) * yb
    h = x_ref[...] + _dot(m.astype(BF16), wout_ref[...])
    ms = jnp.mean(h * h, axis=-1, keepdims=True)
    hn = ((h * lax.rsqrt(ms + EPS)) * gffn_ref[...]).astype(BF16)
    for c in range(D_FF // ff_chunk):
        u = _dot(hn, wup_ref[:, c * ff_chunk:(c + 1) * ff_chunk])
        r = jnp.square(jnp.maximum(u, 0.0)).astype(BF16)
        h = h + _dot(r, wdown_ref[c * ff_chunk:(c + 1) * ff_chunk, :])
    ms = jnp.mean(h * h, axis=-1, keepdims=True)
    y_ref[...] = (h * lax.rsqrt(ms + EPS)) * gfin_ref[...]


def _merge_ffn(x2d, ygla, ofox, sg, w, tm):
    t_total = x2d.shape[0]
    tok = lambda width: pl.BlockSpec((tm, width), lambda i: (i, 0))
    return pl.pallas_call(
        functools.partial(_merge_ffn_kernel, ff_chunk=1024),
        grid=(t_total // tm,),
        in_specs=[
            tok(D_MODEL), tok(GLA_V), tok(FOX_W), tok(2 * D_MODEL),
            _const_spec((GLA_V, D_MODEL)), _const_spec((FOX_W, D_MODEL)),
            _const_spec((D_MODEL, D_MODEL)), _const_spec((1, D_MODEL)),
            _const_spec((D_MODEL, D_FF)), _const_spec((D_FF, D_MODEL)), _const_spec((1, D_MODEL)),
        ],
        out_specs=tok(D_MODEL),
        out_shape=jax.ShapeDtypeStruct((t_total, D_MODEL), F32),
        compiler_params=pltpu.CompilerParams(
            dimension_semantics=("arbitrary",), vmem_limit_bytes=VMEM_LIMIT),
        name="merge_ffn",
    )(x2d, ygla, ofox, sg, w["w_bg"], w["w_bf"], w["w_out"], w["gn_ffn"], w["w_up"], w["w_down"],
      w["gn_final"])


def _prep_weights(norm_mix, w_in, w_gla_gate, b_gla_gate, g_gla_norm, b_fox_forget, w_branch_gla,
                  w_branch_fox, w_out, norm_ffn, w_up, w_down, norm_final):
    sizes = (GLA_QK, GLA_QK, GLA_V, GLA_V, GLA_RANK, FOX_W, FOX_W, FOX_W, FOX_HEADS, D_MODEL, D_MODEL)
    offs = [0]
    for s in sizes:
        offs.append(offs[-1] + s)
    col = lambda i: w_in[:, offs[i]:offs[i + 1]]
    gq, gk, gv, gr, glr, fq, fk, fv, ff, ga, gb = (col(i) for i in range(11))
    w_nn = jnp.concatenate([p.astype(BF16) for p in (
        gq * (GLA_DK ** -0.5), gv, gr, fq * (FOX_DH ** -0.5 * LOG2E), ga, gb)], axis=1)
    pad = jnp.zeros((D_MODEL, _R_FK - _R_FF - FOX_HEADS), F32)
    w_t = jnp.concatenate([gk, glr, ff, pad, fk, fv], axis=1).T.astype(BF16)
    return dict(
        gn_mix=norm_mix.reshape(1, D_MODEL), w_nn=w_nn, w_t=w_t,
        w_gate_t=w_gla_gate.T.astype(BF16), b_gate_col=b_gla_gate.reshape(GLA_QK, 1),
        b_f_col=b_fox_forget.reshape(FOX_HEADS, 1),
        w_bg=(jnp.tile(g_gla_norm, GLA_HEADS)[:, None] * w_branch_gla).astype(BF16), w_bf=w_branch_fox.astype(BF16), w_out=w_out.astype(BF16),
        gn_ffn=norm_ffn.reshape(1, D_MODEL), w_up=w_up.astype(BF16), w_down=w_down.astype(BF16),
        gn_final=norm_final.reshape(1, D_MODEL),
    )


def _pad_lanes(x):
    n = x.shape[-1]
    npad = -(-n // LANES) * LANES
    return x if npad == n else jnp.pad(x, [(0, 0)] * (x.ndim - 1) + [(0, npad - n)])


def kernel(x_prompt, x_sample, cache_fox_k, cache_fox_v, cache_fox_logf, state_gla, meta_tokens,
           norm_mix, w_in, w_gla_gate, b_gla_gate, g_gla_norm, b_fox_forget, w_branch_gla,
           w_branch_fox, w_out, norm_ffn, w_up, w_down, norm_final):
    bsz, seq, _ = x_prompt.shape
    dbsz, dseq, _ = x_sample.shape
    past = cache_fox_k.shape[3]
    w = _prep_weights(norm_mix[0], w_in[0], w_gla_gate[0], b_gla_gate[0], g_gla_norm[0],
                      b_fox_forget[0], w_branch_gla[0], w_branch_fox[0], w_out[0], norm_ffn[0],
                      w_up[0], w_down[0], norm_final)
    xp = x_prompt.reshape(bsz * seq, D_MODEL)
    xs = x_sample.reshape(dbsz * dseq, D_MODEL)

    (m_qvr, m_kT, m_laT, m_logf, _, _, _, m_kT32, m_vT32, _) = _inproj(
        meta_tokens.astype(F32), 1, N_META, 1, N_META, N_META, w)
    zero_state = jnp.zeros((1, GLA_HEADS, GLA_DK, GLA_DV), F32)
    _, s_meta = _gla(m_qvr, m_kT, m_laT, zero_state, 1, N_META, N_META, N_META, False)
    kp, vp = _pad_lanes(m_kT32), _pad_lanes(m_vT32)

    tm = 512
    lead = (kp.reshape(FOX_W, LANES), vp.reshape(FOX_W, LANES))
    (p_qvr, p_kT, p_laT, p_logf, p_fq, p_fkT, p_fvT, p_kT32, p_vT32, p_sg) = _inproj(
        xp, bsz, seq, 1, tm, LANES, w, lead=lead)
    y_gla, s_p = _gla(p_qvr, p_kT, p_laT, s_meta, bsz, seq, LANES, CHUNK, True)
    lp = _pad_lanes(jnp.broadcast_to(m_logf, (bsz, FOX_HEADS, N_META)).reshape(bsz * FOX_HEADS, N_META))
    cp, cn = _decay(lp, p_logf.reshape(bsz * FOX_HEADS, seq), N_META)
    tq = 256
    c_row = cn.reshape(bsz, FOX_HEADS // 2, 2, seq)
    cp_row = cp[:FOX_HEADS].reshape(1, FOX_HEADS // 2, 2, LANES)
    o_fox = _fox(p_fq, p_fkT, p_fvT, kp, vp, c_row, cp_row, bsz, seq, tq, LANES, N_META, True, 2)
    y_prompt = _merge_ffn(xp, y_gla, o_fox, p_sg, w, tm).reshape(bsz, seq, D_MODEL)

    (s_qvr, s_kT, s_laT, s_logf, s_fq, s_fkT, s_fvT, s_kT32, s_vT32, s_sg) = _inproj(
        xs, dbsz, dseq, dbsz, dseq, dseq, w)
    ys_gla, s_s = _gla(s_qvr, s_kT, s_laT, state_gla[0].astype(F32), dbsz, dseq, dseq, dseq, True)
    cps, cns = _decay(cache_fox_logf[0].astype(F32).reshape(dbsz * FOX_HEADS, past),
                      _pad_lanes(s_logf.reshape(dbsz * FOX_HEADS, dseq)), past)
    cs_row = cns[:, :dseq].reshape(dbsz, FOX_HEADS // 2, 2, dseq)
    cps_row = cps.reshape(dbsz, FOX_HEADS // 2, 2, past)
    os_fox = _fox(s_fq, s_fkT, s_fvT, jnp.swapaxes(cache_fox_k[0], 2, 3).astype(F32),
                  jnp.swapaxes(cache_fox_v[0], 2, 3).astype(F32), cs_row, cps_row,
                  dbsz, dseq, dseq, 256, past, False, FOX_HEADS // 2)
    y_sample = _merge_ffn(xs, ys_gla, os_fox, s_sg, w, dbsz * dseq).reshape(dbsz, dseq, D_MODEL)

    to_state = lambda a: jnp.swapaxes(a, 2, 3)[None]
    new_fox_logf_prompt = jnp.concatenate(
        [jnp.broadcast_to(m_logf, (bsz, FOX_HEADS, N_META)), p_logf], axis=2)[None]
    return (y_prompt, y_sample, to_state(p_kT32), to_state(p_vT32), new_fox_logf_prompt,
            s_p[None], to_state(s_kT32), to_state(s_vT32), s_logf[None], s_s[None])
```

```python
import functools

import jax
import jax.numpy as jnp
from jax import lax
from jax.experimental import pallas as pl
from jax.experimental.pallas import tpu as pltpu

F32 = jnp.float32
BF16 = jnp.bfloat16

D_MODEL = 1024
N_META = 16
EPS = 1e-6
GLA_HEADS = 4
GLA_DK = 64
GLA_DV = 128
GLA_RANK = 16
GLA_TAU = 16.0
GLA_QK = GLA_HEADS * GLA_DK
GLA_V = GLA_HEADS * GLA_DV
FOX_HEADS = 8
FOX_DH = 64
FOX_W = FOX_HEADS * FOX_DH
D_FF = 4 * D_MODEL
NEG = -1e30
CHUNK = 64
LOG2E = 1.4426950408889634

LANES = 128
VMEM_LIMIT = 56 * 1024 * 1024

_C_GQ, _C_GV, _C_GR, _C_FQ, _C_GA, _C_GB, _C_END = 0, 256, 768, 1280, 1792, 2816, 3840
_R_K, _R_LR, _R_FF, _R_FK, _R_FV, _R_END = 0, 256, 272, 288, 800, 1312


def _dot(a, b):
    return jnp.dot(a, b, preferred_element_type=F32)


def _dot_nt(a, b):
    return lax.dot_general(a, b, (((1,), (1,)), ((), ())), preferred_element_type=F32)


def _split3(x):
    hi = x.astype(BF16)
    r = x - hi.astype(F32)
    mid = r.astype(BF16)
    lo = (r - mid.astype(F32)).astype(BF16)
    return hi, mid, lo


def _dot3(x, m):
    hi, mid, lo = _split3(x)
    return _dot(hi, m) + _dot(mid, m) + _dot(lo, m)


def _log_sigmoid(x):
    return jnp.minimum(x, 0.0) - jnp.log1p(jnp.exp(-jnp.abs(x)))


def _const_spec(shape):
    nd = len(shape)
    return pl.BlockSpec(shape, lambda *_: (0,) * nd, pipeline_mode=pl.Buffered(1))


def _inproj_kernel(x_ref, gn_ref, wnn_ref, wt_ref, wg_ref, bg_ref, bf_ref, *rest, nb, tmb, tw, nt, row_off):
    if row_off:
        lead_k_ref, lead_v_ref, x_first_ref = rest[:3]
        rest = rest[3:]
    (qvr_ref, kT_ref, laT_ref, logf_ref, fq_ref, fkT_ref, fvT_ref, k32_ref, v32_ref, sg_ref) = rest[:10]
    carries, hn_scr = rest[10:12], (rest[12] if row_off else None)
    tm = nb * tmb

    def normed(xr):
        x = xr[...]
        ms = jnp.mean(x * x, axis=-1, keepdims=True)
        return ((x * lax.rsqrt(ms + EPS)) * gn_ref[...]).astype(BF16)

    def project():
        if row_off:
            step = pl.program_id(0) * nt + pl.program_id(1)
            hn = hn_scr[lax.rem(step, 2)]
        else:
            hn = normed(x_ref)

        zT = _dot_nt(wt_ref[...], hn)
        if row_off:
            hn_scr[lax.rem(step + 1, 2)] = normed(x_ref)
        kT = zT[_R_K:_R_LR].astype(BF16)
        glrT = zT[_R_LR:_R_FF].astype(BF16)
        laT = _log_sigmoid(_dot(wg_ref[...], glrT) + bg_ref[...]) * (1.0 / GLA_TAU)
        logf = _log_sigmoid(zT[_R_FF:_R_FF + FOX_HEADS] + bf_ref[...])
        zk, zv = zT[_R_FK:_R_FV], zT[_R_FV:_R_END]
        for bb in range(nb):
            r0 = bb * tmb
            logf_ref[bb] = logf[:, r0:r0 + tmb]
            fkT_ref[bb] = zk[:, r0:r0 + tmb].astype(BF16)
            fvT_ref[bb] = zv[:, r0:r0 + tmb].astype(BF16)
            for j in range(tmb // tw):
                kT_ref[bb, j] = kT[:, r0 + j * tw:r0 + (j + 1) * tw]
                laT_ref[bb, j] = laT[:, r0 + j * tw:r0 + (j + 1) * tw]
            if not row_off:
                for h in range(FOX_HEADS):
                    k32_ref[bb, h] = zk[h * FOX_DH:(h + 1) * FOX_DH, r0:r0 + tmb]
                    v32_ref[bb, h] = zv[h * FOX_DH:(h + 1) * FOX_DH, r0:r0 + tmb]
        if row_off:
            lane = lax.broadcasted_iota(jnp.int32, (FOX_W, LANES), 1)
            for z, carry, o_ref in ((zk, carries[0], k32_ref), (zv, carries[1], v32_ref)):
                rolled = pltpu.roll(z, row_off, axis=1)
                first = jnp.where(lane < row_off, carry[...], rolled[:, 0:LANES])
                carry[...] = rolled[:, 0:LANES]
                for h in range(FOX_HEADS):
                    hr = slice(h * FOX_DH, (h + 1) * FOX_DH)
                    o_ref[0, h, :, 0:LANES] = first[hr]
                    o_ref[0, h, :, LANES:tm] = rolled[hr, LANES:tm]

        sg_ref[...] = jax.nn.sigmoid(_dot(hn, wnn_ref[:, _C_GA:_C_END])).astype(BF16)
        zq = _dot(hn, wnn_ref[:, _C_GQ:_C_FQ])
        qvr_ref[:, _C_GQ:_C_GR] = zq[:, _C_GQ:_C_GR].astype(BF16)
        qvr_ref[:, _C_GR:_C_FQ] = jax.nn.silu(zq[:, _C_GR:_C_FQ]).astype(BF16)
        fq_ref[...] = _dot(hn, wnn_ref[:, _C_FQ:_C_GA]).astype(BF16)

    if not row_off:
        project()
        return

    i = pl.program_id(1)

    @pl.when(i == 0)
    def _():
        carries[0][...] = lead_k_ref[...]
        carries[1][...] = lead_v_ref[...]

    @pl.when((pl.program_id(0) == 0) & (i == 0))
    def _():
        hn_scr[0] = normed(x_first_ref)

    pl.when(i < nt)(project)

    @pl.when(i == nt)
    def _():
        for carry, o_ref in ((carries[0], k32_ref), (carries[1], v32_ref)):
            for h in range(FOX_HEADS):
                o_ref[0, h, :, 0:LANES] = carry[h * FOX_DH:(h + 1) * FOX_DH, :]


def _inproj(x2d, nbatch, n, nb, tmb, tw, w, lead=None):
    tm = nb * tmb
    ngrp, nt = nbatch // nb, n // tmb
    t_total = nbatch * n
    row_off = N_META if lead is not None else 0
    steps = nt + 1 if row_off else nt
    blk = (lambda i: jnp.minimum(i, nt - 1)) if row_off else (lambda i: i)
    tok = lambda width: pl.BlockSpec((tm, width), lambda g, i: (g * nt + blk(i), 0))
    out_shape = (
        jax.ShapeDtypeStruct((t_total, _C_FQ), BF16),
        jax.ShapeDtypeStruct((nbatch, n // tw, GLA_QK, tw), BF16),
        jax.ShapeDtypeStruct((nbatch, n // tw, GLA_QK, tw), F32),
        jax.ShapeDtypeStruct((nbatch, FOX_HEADS, n), F32),
        jax.ShapeDtypeStruct((t_total, FOX_W), BF16),
        jax.ShapeDtypeStruct((nbatch, FOX_W, n), BF16),
        jax.ShapeDtypeStruct((nbatch, FOX_W, n), BF16),
        jax.ShapeDtypeStruct((nbatch, FOX_HEADS, FOX_DH, row_off + n), F32),
        jax.ShapeDtypeStruct((nbatch, FOX_HEADS, FOX_DH, row_off + n), F32),
        jax.ShapeDtypeStruct((t_total, 2 * D_MODEL), BF16),
    )
    tblk = pl.BlockSpec((nb, tmb // tw, GLA_QK, tw), lambda g, i: (g, blk(i), 0, 0))
    lblk = pl.BlockSpec((nb, FOX_W, tmb), lambda g, i: (g, 0, blk(i)))
    sblk = pl.BlockSpec((nb, FOX_HEADS, FOX_DH, tmb), lambda g, i: (g, 0, 0, i))
    out_specs = (
        tok(_C_FQ), tblk, tblk,
        pl.BlockSpec((nb, FOX_HEADS, tmb), lambda g, i: (g, 0, blk(i))),
        tok(FOX_W), lblk, lblk, sblk, sblk, tok(2 * D_MODEL),
    )
    if row_off:
        x_spec = pl.BlockSpec((tm, D_MODEL), lambda g, i: (jnp.minimum(g * nt + blk(i) + 1, ngrp * nt - 1), 0))
    else:
        x_spec = tok(D_MODEL)
    in_specs = [
        x_spec,
        _const_spec((1, D_MODEL)),
        _const_spec((D_MODEL, _C_END)),
        _const_spec((_R_END, D_MODEL)),
        _const_spec((GLA_QK, GLA_RANK)),
        _const_spec((GLA_QK, 1)),
        _const_spec((FOX_HEADS, 1)),
    ]
    args = [x2d, w["gn_mix"], w["w_nn"], w["w_t"], w["w_gate_t"], w["b_gate_col"], w["b_f_col"]]
    scratch = []
    if row_off:
        assert nb == 1 and tmb % LANES == 0
        in_specs += [_const_spec((FOX_W, LANES))] * 2 + [_const_spec((tm, D_MODEL))]
        args += list(lead) + [x2d]
        scratch = [pltpu.VMEM((FOX_W, LANES), F32)] * 2 + [pltpu.VMEM((2, tm, D_MODEL), BF16)]
    return pl.pallas_call(
        functools.partial(_inproj_kernel, nb=nb, tmb=tmb, tw=tw, nt=nt, row_off=row_off),
        grid=(ngrp, steps), in_specs=in_specs, out_specs=out_specs, out_shape=out_shape,
        scratch_shapes=scratch,
        compiler_params=pltpu.CompilerParams(
            dimension_semantics=("arbitrary", "arbitrary"), vmem_limit_bytes=VMEM_LIMIT),
        name="inproj",
    )(*args)


def _cumsum_lanes(x, carry):
    n = x.shape[1]
    r = lax.broadcasted_iota(jnp.int32, (LANES, LANES), 0)
    c = lax.broadcasted_iota(jnp.int32, (LANES, LANES), 1)
    upper = jnp.where(r <= c, 1.0, 0.0).astype(BF16)
    outs = []
    for j in range(n // LANES):
        cj = _dot3(x[:, j * LANES:(j + 1) * LANES], upper) + carry
        outs.append(cj)
        carry = cj[:, LANES - 1:LANES]
    return jnp.concatenate(outs, axis=1), carry


def _decay_kernel(lp_ref, ln_ref, cp_ref, cn_ref, *, p_valid):
    lp = lp_ref[...]
    lane = lax.broadcasted_iota(jnp.int32, lp.shape, 1)
    lp = jnp.where(lane < p_valid, lp, 0.0)
    zero = jnp.zeros((lp.shape[0], 1), F32)
    cp, tot = _cumsum_lanes(lp, zero)
    cp_ref[...] = cp * LOG2E
    cn, _ = _cumsum_lanes(ln_ref[...], tot)
    cn_ref[...] = cn * LOG2E


def _decay(logf_p, logf_n, p_valid):
    r, pp = logf_p.shape
    npad = logf_n.shape[1]
    return pl.pallas_call(
        functools.partial(_decay_kernel, p_valid=p_valid),
        out_shape=(jax.ShapeDtypeStruct((r, pp), F32), jax.ShapeDtypeStruct((r, npad), F32)),
        compiler_params=pltpu.CompilerParams(vmem_limit_bytes=VMEM_LIMIT),
        name="decay",
    )(logf_p, logf_n)


def _gla_kernel(qvr_ref, kT_ref, laT_ref, s0_ref, y_ref, sout_ref, *, n, tw, chunk, need_y):
    nblk = n // tw
    ncb = tw // chunk

    sr = lax.broadcasted_iota(jnp.int32, (tw, LANES), 0)
    tr = lax.broadcasted_iota(jnp.int32, (tw, tw), 0)
    tc = lax.broadcasted_iota(jnp.int32, (tw, tw), 1)
    m_cum = ((tr // chunk) == (tc // chunk)) & (tr <= tc)
    m_all = jnp.concatenate(
        [jnp.where((sr // chunk) == ci, 1.0, 0.0) for ci in range(ncb)] + [jnp.where(m_cum, 1.0, 0.0)],
        axis=1).astype(BF16)
    stack_k = tw % LANES == 0
    m_rhs = jnp.concatenate([m_all] * 2, axis=0) if stack_k else m_all
    lane_chunk = lax.broadcasted_iota(jnp.int32, (GLA_QK, tw), 1) // chunk
    zero_blk = jnp.zeros((GLA_DK, GLA_DV), BF16)

    def group(jg, state):
        chunks = []
        for t in range(nbg):
            jb = jg * nbg + t
            la = laT_ref[0, jb]
            hi = la.astype(BF16)
            lo = (la - hi.astype(F32)).astype(BF16)
            if stack_k:
                g = _dot(jnp.concatenate([hi, lo], axis=1), m_rhs)
            else:
                g = _dot(hi, m_rhs) + _dot(lo, m_rhs)
            tot = [g[:, ci * LANES:(ci + 1) * LANES] for ci in range(ncb)]
            g_cum = g[:, ncb * LANES:ncb * LANES + tw]
            g_tot = tot[0][:, 0:tw] if tw <= LANES else jnp.concatenate([tot[0]] * (tw // LANES), axis=1)
            for ci in range(1, ncb):
                g_tot = jnp.where(lane_chunk == ci, tot[ci][:, 0:tw], g_tot)
            kdec = (kT_ref[0, jb].astype(F32) * jnp.exp(g_tot - g_cum)).astype(BF16)
            for ci in range(ncb):
                rows = pl.ds(pl.multiple_of(jb * tw, tw) + ci * chunk, chunk)
                v = qvr_ref[rows, _C_GV:_C_GR]
                us = [_dot(kdec[h * GLA_DK:(h + 1) * GLA_DK, ci * chunk:(ci + 1) * chunk],
                           v[:, h * GLA_DV:(h + 1) * GLA_DV]) for h in range(GLA_HEADS)]
                chunks.append((rows, jnp.exp(tot[ci]), us))
        s_bds = []
        for rows, a, us in chunks:
            state = tuple(a[h * GLA_DK:(h + 1) * GLA_DK] * state[h] + us[h] for h in range(GLA_HEADS))
            if need_y:
                s_bds.append(jnp.concatenate(
                    [jnp.concatenate([state[h].astype(BF16) if hc == h else zero_blk
                                      for hc in range(GLA_HEADS)], axis=1) for h in range(GLA_HEADS)], axis=0))
        if need_y:
            for (rows, _, _), s_bd in zip(chunks, s_bds):
                o = _dot(qvr_ref[rows, _C_GQ:_C_GV], s_bd)
                gate = qvr_ref[rows, _C_GR:_C_FQ].astype(F32)
                parts = []
                for h in range(GLA_HEADS):
                    oh = o[:, h * GLA_DV:(h + 1) * GLA_DV]
                    msq = jnp.mean(oh * oh, axis=-1, keepdims=True)
                    parts.append(oh * lax.rsqrt(msq + EPS))
                y_ref[rows, :] = (jnp.concatenate(parts, axis=1) * gate).astype(BF16)
        return state

    nbg = 8 if nblk % 8 == 0 else 1
    state = tuple(s0_ref[0, h] for h in range(GLA_HEADS))
    state = lax.fori_loop(0, nblk // nbg, group, state)
    for h in range(GLA_HEADS):
        sout_ref[0, h] = state[h]


def _gla(qvr, kT, laT, s0, nbatch, n, tw, chunk, need_y):
    s0_map = (lambda b: (0, 0, 0, 0)) if s0.shape[0] == 1 else (lambda b: (b, 0, 0, 0))
    kern = functools.partial(_gla_kernel, n=n, tw=tw, chunk=chunk, need_y=need_y)
    if not need_y:
        def kern(qvr_ref, kT_ref, laT_ref, s0_ref, sout_ref):
            _gla_kernel(qvr_ref, kT_ref, laT_ref, s0_ref, None, sout_ref,
                        n=n, tw=tw, chunk=chunk, need_y=False)
    state_shape = (GLA_HEADS, GLA_DK, GLA_DV)
    out_shape = [jax.ShapeDtypeStruct((nbatch,) + state_shape, F32)]
    out_specs = [pl.BlockSpec((1,) + state_shape, lambda b: (b, 0, 0, 0))]
    if need_y:
        out_shape = [jax.ShapeDtypeStruct((nbatch * n, GLA_V), BF16)] + out_shape
        out_specs = [pl.BlockSpec((n, GLA_V), lambda b: (b, 0))] + out_specs
    res = pl.pallas_call(
        kern, grid=(nbatch,),
        in_specs=[
            pl.BlockSpec((n, _C_FQ), lambda b: (b, 0)),
            pl.BlockSpec((1, n // tw, GLA_QK, tw), lambda b: (b, 0, 0, 0)),
            pl.BlockSpec((1, n // tw, GLA_QK, tw), lambda b: (b, 0, 0, 0)),
            pl.BlockSpec((1,) + state_shape, s0_map),
        ],
        out_specs=out_specs, out_shape=out_shape,
        compiler_params=pltpu.CompilerParams(
            dimension_semantics=("arbitrary",), vmem_limit_bytes=VMEM_LIMIT),
        name="gla",
    )(qvr, kT, laT, s0)
    return res if need_y else (None, res[0])


def _fold_lanes(x, op):
    w = x.shape[1]
    if w % LANES == 0:
        f = x[:, 0:LANES]
        for g in range(1, w // LANES):
            f = op(f, x[:, g * LANES:(g + 1) * LANES])
        return f
    assert op is jnp.add
    lane = lax.broadcasted_iota(jnp.int32, (x.shape[0], LANES), 1)
    return jnp.where(lane == 0, jnp.sum(x, axis=-1, keepdims=True), 0.0)


def _tile_lanes(x, w):
    return x[:, 0:w] if w < LANES else jnp.concatenate([x] * (w // LANES), axis=1)


def _fox_kernel(q_ref, kn_ref, vn_ref, kp_ref, vp_ref, crow_ref, cprow_ref, o_ref, ka_scr, va_scr, *scr,
                n, tq, tkp, p_pad, p_valid, pv_t, nhp):
    nq = n // tq
    rb = min(tq, 64)
    nh = 2 * nhp
    nbuf = len(scr) // (3 * nh)
    s_scr, p_scr, sh_scr = (scr[i * nh * nbuf:(i + 1) * nh * nbuf] for i in range(3))
    nparts = 3
    ones_rows = 16

    def c_block(c_row):
        parts = _split3(-c_row)
        r = lax.broadcasted_iota(jnp.int32, (FOX_DH, c_row.shape[1]), 0)
        blk = jnp.zeros((FOX_DH, c_row.shape[1]), F32)
        for i, part in enumerate(parts):
            blk = jnp.where(r == i, part.astype(F32), blk)
        return blk.astype(BF16)

    for h in range(nh):
        hp, hh = divmod(h, 2)
        hrows = slice(h * FOX_DH, (h + 1) * FOX_DH)
        for k_src, c_src, c0, w in ((kp_ref[0, h].astype(BF16), cprow_ref[0, hp, hh:hh + 1, :], 0, p_pad),
                                    (kn_ref[0, hrows, :], crow_ref[0, hp, hh:hh + 1, :], p_pad, n)):
            halves = (k_src, c_block(c_src)) if hh == 0 else (c_block(c_src), k_src)
            ka_scr[h, :, c0:c0 + w] = jnp.concatenate(halves, axis=0)
        va_scr[h, 0:FOX_DH, 0:p_pad] = vp_ref[0, h].astype(BF16)
        va_scr[h, 0:FOX_DH, p_pad:p_pad + n] = vn_ref[0, hrows, :]
        va_scr[h, FOX_DH:FOX_DH + ones_rows, :] = jnp.ones((ones_rows, p_pad + n), BF16)

    lane_q = lax.broadcasted_iota(jnp.int32, (tq, LANES), 1)
    eye = jnp.where(lax.broadcasted_iota(jnp.int32, (tq, tq), 0) == lax.broadcasted_iota(jnp.int32, (tq, tq), 1),
                    1.0, 0.0).astype(BF16)

    def block_chunks(qi):
        return [(c0, tkp) for c0 in range(0, p_pad, tkp)] + [(p_pad + j * tq, tq) for j in range(qi + 1)]

    def pass1_items(qi):
        rows = slice(qi * tq, (qi + 1) * tq)
        chunks = block_chunks(qi)
        state = {}

        def start():
            c_parts = [part for h in range(nh) for part in _split3(crow_ref[0, h // 2, h % 2:h % 2 + 1, rows])]
            pad_rows = -len(c_parts) % ones_rows
            c_rows = jnp.concatenate(c_parts + [jnp.zeros((pad_rows, tq), BF16)], axis=0)
            state["c_cols"] = _dot_nt(eye, c_rows)
            for h in range(nh):
                hp, hh = divmod(h, 2)
                qp = q_ref[rows, hp * LANES:(hp + 1) * LANES]
                c_lane0 = FOX_DH if hh == 0 else 0
                own = (lane_q < FOX_DH) if hh == 0 else (lane_q >= FOX_DH)
                ones_pat = jnp.where((lane_q >= c_lane0) & (lane_q < c_lane0 + nparts), 1.0, 0.0).astype(BF16)
                state["q", h] = jnp.where(own, qp, ones_pat)
                state["mx", h] = None

        def chunk(h, ci):
            c0, cw = chunks[ci]
            s = _dot(state["q", h], ka_scr[h, :, c0:c0 + cw])
            if c0 < p_pad and c0 + cw > p_valid:
                col = lax.broadcasted_iota(jnp.int32, s.shape, 1) + c0
                s = jnp.where(col < p_valid, s, NEG)
            if ci == len(chunks) - 1:
                r = lax.broadcasted_iota(jnp.int32, s.shape, 0)
                c = lax.broadcasted_iota(jnp.int32, s.shape, 1)
                s = jnp.where(c <= r, s, NEG)
            s_scr[buf(qi, h)][:, c0:c0 + cw] = s
            f = _fold_lanes(s, jnp.maximum) if cw % LANES == 0 else jnp.max(s, axis=-1, keepdims=True)
            mx = state["mx", h]
            if mx is None or mx.shape == f.shape:
                mx = f if mx is None else jnp.maximum(mx, f)
            else:
                mx = jnp.maximum(jnp.max(mx, axis=-1, keepdims=True), jnp.max(f, axis=-1, keepdims=True))
            state["mx", h] = mx

        def finish(h):
            cq = sum(state["c_cols"][:, h * nparts + i:h * nparts + i + 1] for i in range(nparts))
            m = jnp.max(state["mx", h], axis=-1, keepdims=True) + cq
            sh_scr[buf(qi, h)][...] = jnp.broadcast_to(m - cq, (tq, LANES))

        items = [start]
        for ci in range(len(chunks)):
            items += [functools.partial(chunk, h, ci) for h in range(nh)]
        return items + [functools.partial(finish, h) for h in range(nh)]

    def pass2_items(qi):
        def item(r0, h, c0, cw):
            rr = slice(r0, r0 + rb)
            sh = sh_scr[buf(qi, h)][rr, :]
            p = jnp.exp2(s_scr[buf(qi, h)][rr, c0:c0 + cw] - _tile_lanes(sh, cw))
            p_scr[buf(qi, h)][rr, c0:c0 + cw] = p.astype(BF16)
        return [functools.partial(item, r0, h, c0, cw)
                for r0 in range(0, tq, rb) for h in range(nh) for c0, cw in block_chunks(qi)]

    def pv_items(qi):
        w_tot = p_pad + (qi + 1) * tq
        acc = {}

        def matmul(h):
            p_all, v_aug = p_scr[buf(qi, h)][:, 0:w_tot], va_scr[h, :, 0:w_tot]
            acc[h] = _dot_nt(v_aug, p_all) if pv_t else _dot_nt(p_all, v_aug)

        def store():
            if pv_t:
                outs = [acc[h][0:FOX_DH] * (1.0 / acc[h][FOX_DH:FOX_DH + 1]) for h in range(nh)]
                o = jnp.concatenate(outs, axis=0).T
            else:
                outs = [acc[h][:, 0:FOX_DH] * (1.0 / acc[h][:, FOX_DH:FOX_DH + 1]) for h in range(nh)]
                o = jnp.concatenate(outs, axis=1)
            o_ref[qi * tq:(qi + 1) * tq, :] = o.astype(BF16)

        return [functools.partial(matmul, h) for h in range(nh)] + [store]

    def emit_merged(*lists):
        total = max(len(l) for l in lists)
        pos = [0] * len(lists)
        for t in range(total):
            for k, l in enumerate(lists):
                while pos[k] < len(l) and pos[k] * total <= t * len(l):
                    l[pos[k]]()
                    pos[k] += 1
        for k, l in enumerate(lists):
            for item in l[pos[k]:]:
                item()

    buf = lambda qi, h: (qi % nbuf) * nh + h
    emit_merged(pass1_items(0))
    for qi in range(nq):
        emit_merged(pass2_items(qi),
                    pass1_items(qi + 1) if qi + 1 < nq else [],
                    pv_items(qi - 1) if qi > 0 else [])
    emit_merged(pv_items(nq - 1))


def _fox(q, kn, vn, kp, vp, c_row, cp_row, nbatch, n, tq, tkp, p_valid, pv_t, nhp):
    nq = n // tq
    p_pad = kp.shape[3]
    pmap4 = (lambda b, hp: (0, hp, 0, 0)) if kp.shape[0] == 1 else (lambda b, hp: (b, hp, 0, 0))
    cmap4 = (lambda b, hp: (0, hp, 0, 0)) if cp_row.shape[0] == 1 else (lambda b, hp: (b, hp, 0, 0))
    nh = 2 * nhp
    tok = pl.BlockSpec((n, nhp * LANES), lambda b, hp: (b, hp))
    tlanes = pl.BlockSpec((1, nhp * LANES, n), lambda b, hp: (b, hp, 0))
    nbuf = min(nq, 2)
    return pl.pallas_call(
        functools.partial(_fox_kernel, n=n, tq=tq, tkp=tkp, p_pad=p_pad, p_valid=p_valid, pv_t=pv_t, nhp=nhp),
        grid=(nbatch, FOX_HEADS // nh),
        in_specs=[
            tok, tlanes, tlanes,
            pl.BlockSpec((1, nh, FOX_DH, p_pad), pmap4),
            pl.BlockSpec((1, nh, FOX_DH, p_pad), pmap4),
            pl.BlockSpec((1, nhp, 2, n), lambda b, hp: (b, hp, 0, 0)),
            pl.BlockSpec((1, nhp, 2, p_pad), cmap4),
        ],
        out_specs=tok,
        out_shape=jax.ShapeDtypeStruct((nbatch * n, FOX_W), BF16),
        scratch_shapes=([pltpu.VMEM((nh, LANES, p_pad + n), BF16), pltpu.VMEM((nh, FOX_DH + 16, p_pad + n), BF16)]
                        + [pltpu.VMEM((tq, p_pad + n), F32)] * (nh * nbuf)
                        + [pltpu.VMEM((tq, p_pad + n), BF16)] * (nh * nbuf)
                        + [pltpu.VMEM((tq, LANES), F32)] * (nh * nbuf)),
        compiler_params=pltpu.CompilerParams(
            dimension_semantics=("arbitrary", "arbitrary"), vmem_limit_bytes=VMEM_LIMIT),
        name="fox",
    )(q, kn, vn, kp, vp, c_row, cp_row)


def _merge_ffn_kernel(x_ref, ygla_ref, ofox_ref, sg_ref, wbg_ref, wbf_ref, wout_ref, gffn_ref,
                      wup_ref, wdown_ref, gfin_ref, y_ref, *, ff_chunk):
    ya = _dot(ygla_ref[...], wbg_ref[...])
    yb = _dot(ofox_ref[...], wbf_ref[...])
    m = sg_ref[:, 0:D_MODEL].astype(F32) * ya + sg_ref[:, D_MODEL:2 * D_MODEL].astype(F32) * yb
    h = x_ref[...] + _dot(m.astype(BF16), wout_ref[...])
    ms = jnp.mean(h * h, axis=-1, keepdims=True)
    hn = ((h * lax.rsqrt(ms + EPS)) * gffn_ref[...]).astype(BF16)
    for c in range(D_FF // ff_chunk):
        u = _dot(hn, wup_ref[:, c * ff_chunk:(c + 1) * ff_chunk])
        r = jnp.square(jnp.maximum(u, 0.0)).astype(BF16)
        h = h + _dot(r, wdown_ref[c * ff_chunk:(c + 1) * ff_chunk, :])
    ms = jnp.mean(h * h, axis=-1, keepdims=True)
    y_ref[...] = (h * lax.rsqrt(ms + EPS)) * gfin_ref[...]


def _merge_ffn(x2d, ygla, ofox, sg, w, tm):
    t_total = x2d.shape[0]
    tok = lambda width: pl.BlockSpec((tm, width), lambda i: (i, 0))
    return pl.pallas_call(
        functools.partial(_merge_ffn_kernel, ff_chunk=1024),
        grid=(t_total // tm,),
        in_specs=[
            tok(D_MODEL), tok(GLA_V), tok(FOX_W), tok(2 * D_MODEL),
            _const_spec((GLA_V, D_MODEL)), _const_spec((FOX_W, D_MODEL)),
            _const_spec((D_MODEL, D_MODEL)), _const_spec((1, D_MODEL)),
            _const_spec((D_MODEL, D_FF)), _const_spec((D_FF, D_MODEL)), _const_spec((1, D_MODEL)),
        ],
        out_specs=tok(D_MODEL),
        out_shape=jax.ShapeDtypeStruct((t_total, D_MODEL), F32),
        compiler_params=pltpu.CompilerParams(
            dimension_semantics=("arbitrary",), vmem_limit_bytes=VMEM_LIMIT),
        name="merge_ffn",
    )(x2d, ygla, ofox, sg, w["w_bg"], w["w_bf"], w["w_out"], w["gn_ffn"], w["w_up"], w["w_down"],
      w["gn_final"])


def _prep_weights(norm_mix, w_in, w_gla_gate, b_gla_gate, g_gla_norm, b_fox_forget, w_branch_gla,
                  w_branch_fox, w_out, norm_ffn, w_up, w_down, norm_final):
    sizes = (GLA_QK, GLA_QK, GLA_V, GLA_V, GLA_RANK, FOX_W, FOX_W, FOX_W, FOX_HEADS, D_MODEL, D_MODEL)
    offs = [0]
    for s in sizes:
        offs.append(offs[-1] + s)
    col = lambda i: w_in[:, offs[i]:offs[i + 1]]
    gq, gk, gv, gr, glr, fq, fk, fv, ff, ga, gb = (col(i) for i in range(11))
    w_nn = jnp.concatenate([p.astype(BF16) for p in (
        gq * (GLA_DK ** -0.5), gv, gr, fq * (FOX_DH ** -0.5 * LOG2E), ga, gb)], axis=1)
    pad = jnp.zeros((D_MODEL, _R_FK - _R_FF - FOX_HEADS), F32)
    w_t = jnp.concatenate([gk, glr, ff, pad, fk, fv], axis=1).T.astype(BF16)
    return dict(
        gn_mix=norm_mix.reshape(1, D_MODEL), w_nn=w_nn, w_t=w_t,
        w_gate_t=w_gla_gate.T.astype(BF16), b_gate_col=b_gla_gate.reshape(GLA_QK, 1),
        b_f_col=b_fox_forget.reshape(FOX_HEADS, 1),
        w_bg=(jnp.tile(g_gla_norm, GLA_HEADS)[:, None] * w_branch_gla).astype(BF16), w_bf=w_branch_fox.astype(BF16), w_out=w_out.astype(BF16),
        gn_ffn=norm_ffn.reshape(1, D_MODEL), w_up=w_up.astype(BF16), w_down=w_down.astype(BF16),
        gn_final=norm_final.reshape(1, D_MODEL),
    )


def _pad_lanes(x):
    n = x.shape[-1]
    npad = -(-n // LANES) * LANES
    return x if npad == n else jnp.pad(x, [(0, 0)] * (x.ndim - 1) + [(0, npad - n)])


def kernel(x_prompt, x_sample, cache_fox_k, cache_fox_v, cache_fox_logf, state_gla, meta_tokens,
           norm_mix, w_in, w_gla_gate, b_gla_gate, g_gla_norm, b_fox_forget, w_branch_gla,
           w_branch_fox, w_out, norm_ffn, w_up, w_down, norm_final):
    bsz, seq, _ = x_prompt.shape
    dbsz, dseq, _ = x_sample.shape
    past = cache_fox_k.shape[3]
    w = _prep_weights(norm_mix[0], w_in[0], w_gla_gate[0], b_gla_gate[0], g_gla_norm[0],
                      b_fox_forget[0], w_branch_gla[0], w_branch_fox[0], w_out[0], norm_ffn[0],
                      w_up[0], w_down[0], norm_final)
    xp = x_prompt.reshape(bsz * seq, D_MODEL)
    xs = x_sample.reshape(dbsz * dseq, D_MODEL)

    (m_qvr, m_kT, m_laT, m_logf, _, _, _, m_kT32, m_vT32, _) = _inproj(
        meta_tokens.astype(F32), 1, N_META, 1, N_META, N_META, w)
    zero_state = jnp.zeros((1, GLA_HEADS, GLA_DK, GLA_DV), F32)
    _, s_meta = _gla(m_qvr, m_kT, m_laT, zero_state, 1, N_META, N_META, N_META, False)
    kp, vp = _pad_lanes(m_kT32), _pad_lanes(m_vT32)

    tm = 512
    lead = (kp.reshape(FOX_W, LANES), vp.reshape(FOX_W, LANES))
    (p_qvr, p_kT, p_laT, p_logf, p_fq, p_fkT, p_fvT, p_kT32, p_vT32, p_sg) = _inproj(
        xp, bsz, seq, 1, tm, LANES, w, lead=lead)
    y_gla, s_p = _gla(p_qvr, p_kT, p_laT, s_meta, bsz, seq, LANES, CHUNK, True)
    lp = _pad_lanes(jnp.broadcast_to(m_logf, (bsz, FOX_HEADS, N_META)).reshape(bsz * FOX_HEADS, N_META))
    cp, cn = _decay(lp, p_logf.reshape(bsz * FOX_HEADS, seq), N_META)
    tq = 256
    c_row = cn.reshape(bsz, FOX_HEADS // 2, 2, seq)
    cp_row = cp[:FOX_HEADS].reshape(1, FOX_HEADS // 2, 2, LANES)
    o_fox = _fox(p_fq, p_fkT, p_fvT, kp, vp, c_row, cp_row, bsz, seq, tq, LANES, N_META, True, 2)
    y_prompt = _merge_ffn(xp, y_gla, o_fox, p_sg, w, tm).reshape(bsz, seq, D_MODEL)

    (s_qvr, s_kT, s_laT, s_logf, s_fq, s_fkT, s_fvT, s_kT32, s_vT32, s_sg) = _inproj(
        xs, dbsz, dseq, dbsz, dseq, dseq, w)
    ys_gla, s_s = _gla(s_qvr, s_kT, s_laT, state_gla[0].astype(F32), dbsz, dseq, dseq, dseq, True)
    cps, cns = _decay(cache_fox_logf[0].astype(F32).reshape(dbsz * FOX_HEADS, past),
                      _pad_lanes(s_logf.reshape(dbsz * FOX_HEADS, dseq)), past)
    cs_row = cns[:, :dseq].reshape(dbsz, FOX_HEADS // 2, 2, dseq)
    cps_row = cps.reshape(dbsz, FOX_HEADS // 2, 2, past)
    os_fox = _fox(s_fq, s_fkT, s_fvT, jnp.swapaxes(cache_fox_k[0], 2, 3).astype(F32),
                  jnp.swapaxes(cache_fox_v[0], 2, 3).astype(F32), cs_row, cps_row,
                  dbsz, dseq, dseq, 256, past, False, FOX_HEADS // 2)
    y_sample = _merge_ffn(xs, ys_gla, os_fox, s_sg, w, dbsz * dseq).reshape(dbsz, dseq, D_MODEL)

    to_state = lambda a: jnp.swapaxes(a, 2, 3)[None]
    new_fox_logf_prompt = jnp.concatenate(
        [jnp.broadcast_to(m_logf, (bsz, FOX_HEADS, N_META)), p_logf], axis=2)[None]
    return (y_prompt, y_sample, to_state(p_kT32), to_state(p_vT32), new_fox_logf_prompt,
            s_p[None], to_state(s_kT32), to_state(s_vT32), s_logf[None], s_s[None])
```

```python
import functools

import jax
import jax.numpy as jnp
from jax import lax
from jax.experimental import pallas as pl
from jax.experimental.pallas import tpu as pltpu

F32 = jnp.float32
BF16 = jnp.bfloat16

D_MODEL = 1024
N_META = 16
EPS = 1e-6
GLA_HEADS = 4
GLA_DK = 64
GLA_DV = 128
GLA_RANK = 16
GLA_TAU = 16.0
GLA_QK = GLA_HEADS * GLA_DK
GLA_V = GLA_HEADS * GLA_DV
FOX_HEADS = 8
FOX_DH = 64
FOX_W = FOX_HEADS * FOX_DH
D_FF = 4 * D_MODEL
NEG = -1e30
CHUNK = 64
LOG2E = 1.4426950408889634

LANES = 128
VMEM_LIMIT = 56 * 1024 * 1024

ROW_TILE = 512
FFN_CHUNK = 1024
FOX_TQ = 256
FOX_PREFIX_CHUNK = 256
FOX_PAIRS_PROMPT = 2
FOX_PAIRS_SAMPLE = FOX_HEADS // 2
GLA_BLOCKS_PER_TRIP = 8

_C_GQ, _C_GV, _C_GR, _C_FQ, _C_GA, _C_GB, _C_END = 0, 256, 768, 1280, 1792, 2816, 3840
_R_K, _R_LR, _R_FF, _R_FK, _R_FV, _R_END = 0, 256, 272, 288, 800, 1312


def _dot(a, b):
    return jnp.dot(a, b, preferred_element_type=F32)


def _dot_nt(a, b):
    return lax.dot_general(a, b, (((1,), (1,)), ((), ())), preferred_element_type=F32)


def _split3(x):
    hi = x.astype(BF16)
    r = x - hi.astype(F32)
    mid = r.astype(BF16)
    lo = (r - mid.astype(F32)).astype(BF16)
    return hi, mid, lo


def _dot3(x, m):
    hi, mid, lo = _split3(x)
    return _dot(hi, m) + _dot(mid, m) + _dot(lo, m)


def _log_sigmoid(x):
    return jnp.minimum(x, 0.0) - jnp.log1p(jnp.exp(-jnp.abs(x)))


def _const_spec(shape):
    nd = len(shape)
    return pl.BlockSpec(shape, lambda *_: (0,) * nd, pipeline_mode=pl.Buffered(1))


def _inproj_kernel(x_ref, gn_ref, wnn_ref, wt_ref, wg_ref, bg_ref, bf_ref, *rest, nb, tmb, tw, nt, row_off):
    if row_off:
        lead_k_ref, lead_v_ref, x_first_ref = rest[:3]
        rest = rest[3:]
    (qvr_ref, kT_ref, laT_ref, logf_ref, fq_ref, fkT_ref, fvT_ref, k32_ref, v32_ref, sg_ref) = rest[:10]
    carries, hn_scr = rest[10:12], (rest[12] if row_off else None)
    tm = nb * tmb

    def normed(xr):
        x = xr[...]
        ms = jnp.mean(x * x, axis=-1, keepdims=True)
        return ((x * lax.rsqrt(ms + EPS)) * gn_ref[...]).astype(BF16)

    def project():
        if row_off:
            step = pl.program_id(0) * nt + pl.program_id(1)
            hn = hn_scr[lax.rem(step, 2)]
        else:
            hn = normed(x_ref)

        zT = _dot_nt(wt_ref[...], hn)
        if row_off:
            hn_scr[lax.rem(step + 1, 2)] = normed(x_ref)
        kT = zT[_R_K:_R_LR].astype(BF16)
        glrT = zT[_R_LR:_R_FF].astype(BF16)
        laT = _log_sigmoid(_dot(wg_ref[...], glrT) + bg_ref[...]) * (1.0 / GLA_TAU)
        logf = _log_sigmoid(zT[_R_FF:_R_FF + FOX_HEADS] + bf_ref[...])
        zk, zv = zT[_R_FK:_R_FV], zT[_R_FV:_R_END]
        for bb in range(nb):
            r0 = bb * tmb
            logf_ref[bb] = logf[:, r0:r0 + tmb]
            fkT_ref[bb] = zk[:, r0:r0 + tmb].astype(BF16)
            fvT_ref[bb] = zv[:, r0:r0 + tmb].astype(BF16)
            for j in range(tmb // tw):
                kT_ref[bb, j] = kT[:, r0 + j * tw:r0 + (j + 1) * tw]
                laT_ref[bb, j] = laT[:, r0 + j * tw:r0 + (j + 1) * tw]
            if not row_off:
                for h in range(FOX_HEADS):
                    k32_ref[bb, h] = zk[h * FOX_DH:(h + 1) * FOX_DH, r0:r0 + tmb]
                    v32_ref[bb, h] = zv[h * FOX_DH:(h + 1) * FOX_DH, r0:r0 + tmb]
        if row_off:
            lane = lax.broadcasted_iota(jnp.int32, (FOX_W, LANES), 1)
            for z, carry, o_ref in ((zk, carries[0], k32_ref), (zv, carries[1], v32_ref)):
                rolled = pltpu.roll(z, row_off, axis=1)
                first = jnp.where(lane < row_off, carry[...], rolled[:, 0:LANES])
                carry[...] = rolled[:, 0:LANES]
                for h in range(FOX_HEADS):
                    hr = slice(h * FOX_DH, (h + 1) * FOX_DH)
                    o_ref[0, h, :, 0:LANES] = first[hr]
                    o_ref[0, h, :, LANES:tm] = rolled[hr, LANES:tm]

        sg_ref[...] = jax.nn.sigmoid(_dot(hn, wnn_ref[:, _C_GA:_C_END])).astype(BF16)
        zq = _dot(hn, wnn_ref[:, _C_GQ:_C_FQ])
        qvr_ref[:, _C_GQ:_C_GR] = zq[:, _C_GQ:_C_GR].astype(BF16)
        qvr_ref[:, _C_GR:_C_FQ] = jax.nn.silu(zq[:, _C_GR:_C_FQ]).astype(BF16)
        fq_ref[...] = _dot(hn, wnn_ref[:, _C_FQ:_C_GA]).astype(BF16)

    if not row_off:
        project()
        return

    i = pl.program_id(1)

    @pl.when(i == 0)
    def _():
        carries[0][...] = lead_k_ref[...]
        carries[1][...] = lead_v_ref[...]

    @pl.when((pl.program_id(0) == 0) & (i == 0))
    def _():
        hn_scr[0] = normed(x_first_ref)

    pl.when(i < nt)(project)

    @pl.when(i == nt)
    def _():
        for carry, o_ref in ((carries[0], k32_ref), (carries[1], v32_ref)):
            for h in range(FOX_HEADS):
                o_ref[0, h, :, 0:LANES] = carry[h * FOX_DH:(h + 1) * FOX_DH, :]


def _inproj(x2d, nbatch, n, nb, tmb, tw, w, lead=None):
    tm = nb * tmb
    ngrp, nt = nbatch // nb, n // tmb
    t_total = nbatch * n
    row_off = N_META if lead is not None else 0
    steps = nt + 1 if row_off else nt
    blk = (lambda i: jnp.minimum(i, nt - 1)) if row_off else (lambda i: i)
    tok = lambda width: pl.BlockSpec((tm, width), lambda g, i: (g * nt + blk(i), 0))
    out_shape = (
        jax.ShapeDtypeStruct((t_total, _C_FQ), BF16),
        jax.ShapeDtypeStruct((nbatch, n // tw, GLA_QK, tw), BF16),
        jax.ShapeDtypeStruct((nbatch, n // tw, GLA_QK, tw), F32),
        jax.ShapeDtypeStruct((nbatch, FOX_HEADS, n), F32),
        jax.ShapeDtypeStruct((t_total, FOX_W), BF16),
        jax.ShapeDtypeStruct((nbatch, FOX_W, n), BF16),
        jax.ShapeDtypeStruct((nbatch, FOX_W, n), BF16),
        jax.ShapeDtypeStruct((nbatch, FOX_HEADS, FOX_DH, row_off + n), F32),
        jax.ShapeDtypeStruct((nbatch, FOX_HEADS, FOX_DH, row_off + n), F32),
        jax.ShapeDtypeStruct((t_total, 2 * D_MODEL), BF16),
    )
    tblk = pl.BlockSpec((nb, tmb // tw, GLA_QK, tw), lambda g, i: (g, blk(i), 0, 0))
    lblk = pl.BlockSpec((nb, FOX_W, tmb), lambda g, i: (g, 0, blk(i)))
    sblk = pl.BlockSpec((nb, FOX_HEADS, FOX_DH, tmb), lambda g, i: (g, 0, 0, i))
    out_specs = (
        tok(_C_FQ), tblk, tblk,
        pl.BlockSpec((nb, FOX_HEADS, tmb), lambda g, i: (g, 0, blk(i))),
        tok(FOX_W), lblk, lblk, sblk, sblk, tok(2 * D_MODEL),
    )
    if row_off:
        x_spec = pl.BlockSpec((tm, D_MODEL), lambda g, i: (jnp.minimum(g * nt + blk(i) + 1, ngrp * nt - 1), 0))
    else:
        x_spec = tok(D_MODEL)
    in_specs = [
        x_spec,
        _const_spec((1, D_MODEL)),
        _const_spec((D_MODEL, _C_END)),
        _const_spec((_R_END, D_MODEL)),
        _const_spec((GLA_QK, GLA_RANK)),
        _const_spec((GLA_QK, 1)),
        _const_spec((FOX_HEADS, 1)),
    ]
    args = [x2d, w["gn_mix"], w["w_nn"], w["w_t"], w["w_gate_t"], w["b_gate_col"], w["b_f_col"]]
    scratch = []
    if row_off:
        assert nb == 1 and tmb % LANES == 0
        in_specs += [_const_spec((FOX_W, LANES))] * 2 + [_const_spec((tm, D_MODEL))]
        args += list(lead) + [x2d]
        scratch = [pltpu.VMEM((FOX_W, LANES), F32)] * 2 + [pltpu.VMEM((2, tm, D_MODEL), BF16)]
    return pl.pallas_call(
        functools.partial(_inproj_kernel, nb=nb, tmb=tmb, tw=tw, nt=nt, row_off=row_off),
        grid=(ngrp, steps), in_specs=in_specs, out_specs=out_specs, out_shape=out_shape,
        scratch_shapes=scratch,
        compiler_params=pltpu.CompilerParams(
            dimension_semantics=("arbitrary", "arbitrary"), vmem_limit_bytes=VMEM_LIMIT),
        name="inproj",
    )(*args)


def _cumsum_lanes(x, carry):
    n = x.shape[1]
    r = lax.broadcasted_iota(jnp.int32, (LANES, LANES), 0)
    c = lax.broadcasted_iota(jnp.int32, (LANES, LANES), 1)
    upper = jnp.where(r <= c, 1.0, 0.0).astype(BF16)
    outs = []
    for j in range(n // LANES):
        cj = _dot3(x[:, j * LANES:(j + 1) * LANES], upper) + carry
        outs.append(cj)
        carry = cj[:, LANES - 1:LANES]
    return jnp.concatenate(outs, axis=1), carry


def _decay_kernel(lp_ref, ln_ref, cp_ref, cn_ref, *, p_valid):
    lp = lp_ref[...]
    lane = lax.broadcasted_iota(jnp.int32, lp.shape, 1)
    lp = jnp.where(lane < p_valid, lp, 0.0)
    zero = jnp.zeros((lp.shape[0], 1), F32)
    cp, tot = _cumsum_lanes(lp, zero)
    cp_ref[...] = cp * LOG2E
    cn, _ = _cumsum_lanes(ln_ref[...], tot)
    cn_ref[...] = cn * LOG2E


def _decay(logf_p, logf_n, p_valid):
    r, pp = logf_p.shape
    npad = logf_n.shape[1]
    return pl.pallas_call(
        functools.partial(_decay_kernel, p_valid=p_valid),
        out_shape=(jax.ShapeDtypeStruct((r, pp), F32), jax.ShapeDtypeStruct((r, npad), F32)),
        compiler_params=pltpu.CompilerParams(vmem_limit_bytes=VMEM_LIMIT),
        name="decay",
    )(logf_p, logf_n)


def _gla_kernel(qvr_ref, kT_ref, laT_ref, s0_ref, y_ref, sout_ref, *, n, tw, chunk, need_y):
    nblk = n // tw
    ncb = tw // chunk

    sr = lax.broadcasted_iota(jnp.int32, (tw, LANES), 0)
    tr = lax.broadcasted_iota(jnp.int32, (tw, tw), 0)
    tc = lax.broadcasted_iota(jnp.int32, (tw, tw), 1)
    m_cum = ((tr // chunk) == (tc // chunk)) & (tr <= tc)
    m_all = jnp.concatenate(
        [jnp.where((sr // chunk) == ci, 1.0, 0.0) for ci in range(ncb)] + [jnp.where(m_cum, 1.0, 0.0)],
        axis=1).astype(BF16)
    stack_k = tw % LANES == 0
    m_rhs = jnp.concatenate([m_all] * 2, axis=0) if stack_k else m_all
    lane_chunk = lax.broadcasted_iota(jnp.int32, (GLA_QK, tw), 1) // chunk
    zero_blk = jnp.zeros((GLA_DK, GLA_DV), BF16)

    def group(jg, state):
        chunks = []
        for t in range(nbg):
            jb = jg * nbg + t
            la = laT_ref[0, jb]
            hi = la.astype(BF16)
            lo = (la - hi.astype(F32)).astype(BF16)
            if stack_k:
                g = _dot(jnp.concatenate([hi, lo], axis=1), m_rhs)
            else:
                g = _dot(hi, m_rhs) + _dot(lo, m_rhs)
            tot = [g[:, ci * LANES:(ci + 1) * LANES] for ci in range(ncb)]
            g_cum = g[:, ncb * LANES:ncb * LANES + tw]
            g_tot = tot[0][:, 0:tw] if tw <= LANES else jnp.concatenate([tot[0]] * (tw // LANES), axis=1)
            for ci in range(1, ncb):
                g_tot = jnp.where(lane_chunk == ci, tot[ci][:, 0:tw], g_tot)
            kdec = (kT_ref[0, jb].astype(F32) * jnp.exp(g_tot - g_cum)).astype(BF16)
            for ci in range(ncb):
                rows = pl.ds(pl.multiple_of(jb * tw, tw) + ci * chunk, chunk)
                v = qvr_ref[rows, _C_GV:_C_GR]
                us = [_dot(kdec[h * GLA_DK:(h + 1) * GLA_DK, ci * chunk:(ci + 1) * chunk],
                           v[:, h * GLA_DV:(h + 1) * GLA_DV]) for h in range(GLA_HEADS)]
                chunks.append((rows, jnp.exp(tot[ci]), us))
        s_bds = []
        for rows, a, us in chunks:
            state = tuple(a[h * GLA_DK:(h + 1) * GLA_DK] * state[h] + us[h] for h in range(GLA_HEADS))
            if need_y:
                s_bds.append(jnp.concatenate(
                    [jnp.concatenate([state[h].astype(BF16) if hc == h else zero_blk
                                      for hc in range(GLA_HEADS)], axis=1) for h in range(GLA_HEADS)], axis=0))
        if need_y:
            for (rows, _, _), s_bd in zip(chunks, s_bds):
                o = _dot(qvr_ref[rows, _C_GQ:_C_GV], s_bd)
                gate = qvr_ref[rows, _C_GR:_C_FQ].astype(F32)
                parts = []
                for h in range(GLA_HEADS):
                    oh = o[:, h * GLA_DV:(h + 1) * GLA_DV]
                    msq = jnp.mean(oh * oh, axis=-1, keepdims=True)
                    parts.append(oh * lax.rsqrt(msq + EPS))
                y_ref[rows, :] = (jnp.concatenate(parts, axis=1) * gate).astype(BF16)
        return state

    nbg = GLA_BLOCKS_PER_TRIP if nblk % GLA_BLOCKS_PER_TRIP == 0 else 1
    state = tuple(s0_ref[0, h] for h in range(GLA_HEADS))
    state = lax.fori_loop(0, nblk // nbg, group, state)
    for h in range(GLA_HEADS):
        sout_ref[0, h] = state[h]


def _gla(qvr, kT, laT, s0, nbatch, n, tw, chunk, need_y):
    s0_map = (lambda b: (0, 0, 0, 0)) if s0.shape[0] == 1 else (lambda b: (b, 0, 0, 0))
    kern = functools.partial(_gla_kernel, n=n, tw=tw, chunk=chunk, need_y=need_y)
    if not need_y:
        def kern(qvr_ref, kT_ref, laT_ref, s0_ref, sout_ref):
            _gla_kernel(qvr_ref, kT_ref, laT_ref, s0_ref, None, sout_ref,
                        n=n, tw=tw, chunk=chunk, need_y=False)
    state_shape = (GLA_HEADS, GLA_DK, GLA_DV)
    out_shape = [jax.ShapeDtypeStruct((nbatch,) + state_shape, F32)]
    out_specs = [pl.BlockSpec((1,) + state_shape, lambda b: (b, 0, 0, 0))]
    if need_y:
        out_shape = [jax.ShapeDtypeStruct((nbatch * n, GLA_V), BF16)] + out_shape
        out_specs = [pl.BlockSpec((n, GLA_V), lambda b: (b, 0))] + out_specs
    res = pl.pallas_call(
        kern, grid=(nbatch,),
        in_specs=[
            pl.BlockSpec((n, _C_FQ), lambda b: (b, 0)),
            pl.BlockSpec((1, n // tw, GLA_QK, tw), lambda b: (b, 0, 0, 0)),
            pl.BlockSpec((1, n // tw, GLA_QK, tw), lambda b: (b, 0, 0, 0)),
            pl.BlockSpec((1,) + state_shape, s0_map),
        ],
        out_specs=out_specs, out_shape=out_shape,
        compiler_params=pltpu.CompilerParams(
            dimension_semantics=("arbitrary",), vmem_limit_bytes=VMEM_LIMIT),
        name="gla",
    )(qvr, kT, laT, s0)
    return res if need_y else (None, res[0])


def _fold_lanes(x, op):
    w = x.shape[1]
    if w % LANES == 0:
        f = x[:, 0:LANES]
        for g in range(1, w // LANES):
            f = op(f, x[:, g * LANES:(g + 1) * LANES])
        return f
    assert op is jnp.add
    lane = lax.broadcasted_iota(jnp.int32, (x.shape[0], LANES), 1)
    return jnp.where(lane == 0, jnp.sum(x, axis=-1, keepdims=True), 0.0)


def _tile_lanes(x, w):
    return x[:, 0:w] if w < LANES else jnp.concatenate([x] * (w // LANES), axis=1)


def _fox_kernel(q_ref, kn_ref, vn_ref, kp_ref, vp_ref, crow_ref, cprow_ref, o_ref, ka_scr, va_scr, *scr,
                n, tq, tkp, p_pad, p_valid, pv_t, nhp):
    nq = n // tq
    rb = min(tq, 64)
    nh = 2 * nhp
    nbuf = len(scr) // (3 * nh)
    s_scr, p_scr, sh_scr = (scr[i * nh * nbuf:(i + 1) * nh * nbuf] for i in range(3))
    nparts = 3
    ones_rows = 16

    def c_block(c_row):
        parts = _split3(-c_row)
        r = lax.broadcasted_iota(jnp.int32, (FOX_DH, c_row.shape[1]), 0)
        blk = jnp.zeros((FOX_DH, c_row.shape[1]), F32)
        for i, part in enumerate(parts):
            blk = jnp.where(r == i, part.astype(F32), blk)
        return blk.astype(BF16)

    for h in range(nh):
        hp, hh = divmod(h, 2)
        hrows = slice(h * FOX_DH, (h + 1) * FOX_DH)
        for k_src, c_src, c0, w in ((kp_ref[0, h].astype(BF16), cprow_ref[0, hp, hh:hh + 1, :], 0, p_pad),
                                    (kn_ref[0, hrows, :], crow_ref[0, hp, hh:hh + 1, :], p_pad, n)):
            halves = (k_src, c_block(c_src)) if hh == 0 else (c_block(c_src), k_src)
            ka_scr[h, :, c0:c0 + w] = jnp.concatenate(halves, axis=0)
        va_scr[h, 0:FOX_DH, 0:p_pad] = vp_ref[0, h].astype(BF16)
        va_scr[h, 0:FOX_DH, p_pad:p_pad + n] = vn_ref[0, hrows, :]
        va_scr[h, FOX_DH:FOX_DH + ones_rows, :] = jnp.ones((ones_rows, p_pad + n), BF16)

    lane_q = lax.broadcasted_iota(jnp.int32, (tq, LANES), 1)
    eye = jnp.where(lax.broadcasted_iota(jnp.int32, (tq, tq), 0) == lax.broadcasted_iota(jnp.int32, (tq, tq), 1),
                    1.0, 0.0).astype(BF16)

    def block_chunks(qi):
        return [(c0, tkp) for c0 in range(0, p_pad, tkp)] + [(p_pad + j * tq, tq) for j in range(qi + 1)]

    def pass1_items(qi):
        rows = slice(qi * tq, (qi + 1) * tq)
        chunks = block_chunks(qi)
        state = {}

        def start():
            c_parts = [part for h in range(nh) for part in _split3(crow_ref[0, h // 2, h % 2:h % 2 + 1, rows])]
            pad_rows = -len(c_parts) % ones_rows
            c_rows = jnp.concatenate(c_parts + [jnp.zeros((pad_rows, tq), BF16)], axis=0)
            state["c_cols"] = _dot_nt(eye, c_rows)
            for h in range(nh):
                hp, hh = divmod(h, 2)
                qp = q_ref[rows, hp * LANES:(hp + 1) * LANES]
                c_lane0 = FOX_DH if hh == 0 else 0
                own = (lane_q < FOX_DH) if hh == 0 else (lane_q >= FOX_DH)
                ones_pat = jnp.where((lane_q >= c_lane0) & (lane_q < c_lane0 + nparts), 1.0, 0.0).astype(BF16)
                state["q", h] = jnp.where(own, qp, ones_pat)
                state["mx", h] = None

        def chunk(h, ci):
            c0, cw = chunks[ci]
            s = _dot(state["q", h], ka_scr[h, :, c0:c0 + cw])
            if c0 < p_pad and c0 + cw > p_valid:
                col = lax.broadcasted_iota(jnp.int32, s.shape, 1) + c0
                s = jnp.where(col < p_valid, s, NEG)
            if ci == len(chunks) - 1:
                r = lax.broadcasted_iota(jnp.int32, s.shape, 0)
                c = lax.broadcasted_iota(jnp.int32, s.shape, 1)
                s = jnp.where(c <= r, s, NEG)
            s_scr[buf(qi, h)][:, c0:c0 + cw] = s
            f = _fold_lanes(s, jnp.maximum) if cw % LANES == 0 else jnp.max(s, axis=-1, keepdims=True)
            mx = state["mx", h]
            if mx is None or mx.shape == f.shape:
                mx = f if mx is None else jnp.maximum(mx, f)
            else:
                mx = jnp.maximum(jnp.max(mx, axis=-1, keepdims=True), jnp.max(f, axis=-1, keepdims=True))
            state["mx", h] = mx

        def finish(h):
            cq = sum(state["c_cols"][:, h * nparts + i:h * nparts + i + 1] for i in range(nparts))
            m = jnp.max(state["mx", h], axis=-1, keepdims=True) + cq
            sh_scr[buf(qi, h)][...] = jnp.broadcast_to(m - cq, (tq, LANES))

        items = [start]
        for ci in range(len(chunks)):
            items += [functools.partial(chunk, h, ci) for h in range(nh)]
        return items + [functools.partial(finish, h) for h in range(nh)]

    def pass2_items(qi):
        def item(r0, h, c0, cw):
            rr = slice(r0, r0 + rb)
            sh = sh_scr[buf(qi, h)][rr, :]
            p = jnp.exp2(s_scr[buf(qi, h)][rr, c0:c0 + cw] - _tile_lanes(sh, cw))
            p_scr[buf(qi, h)][rr, c0:c0 + cw] = p.astype(BF16)
        return [functools.partial(item, r0, h, c0, cw)
                for r0 in range(0, tq, rb) for h in range(nh) for c0, cw in block_chunks(qi)]

    def pv_items(qi):
        w_tot = p_pad + (qi + 1) * tq
        acc = {}

        def matmul(h):
            p_all, v_aug = p_scr[buf(qi, h)][:, 0:w_tot], va_scr[h, :, 0:w_tot]
            acc[h] = _dot_nt(v_aug, p_all) if pv_t else _dot_nt(p_all, v_aug)

        def store():
            if pv_t:
                outs = [acc[h][0:FOX_DH] * (1.0 / acc[h][FOX_DH:FOX_DH + 1]) for h in range(nh)]
                o = jnp.concatenate(outs, axis=0).T
            else:
                outs = [acc[h][:, 0:FOX_DH] * (1.0 / acc[h][:, FOX_DH:FOX_DH + 1]) for h in range(nh)]
                o = jnp.concatenate(outs, axis=1)
            o_ref[qi * tq:(qi + 1) * tq, :] = o.astype(BF16)

        return [functools.partial(matmul, h) for h in range(nh)] + [store]

    def emit_merged(*lists):
        total = max(len(l) for l in lists)
        pos = [0] * len(lists)
        for t in range(total):
            for k, l in enumerate(lists):
                while pos[k] < len(l) and pos[k] * total <= t * len(l):
                    l[pos[k]]()
                    pos[k] += 1
        for k, l in enumerate(lists):
            for item in l[pos[k]:]:
                item()

    buf = lambda qi, h: (qi % nbuf) * nh + h
    emit_merged(pass1_items(0))
    for qi in range(nq):
        emit_merged(pass2_items(qi),
                    pass1_items(qi + 1) if qi + 1 < nq else [],
                    pv_items(qi - 1) if qi > 0 else [])
    emit_merged(pv_items(nq - 1))


def _fox(q, kn, vn, kp, vp, c_row, cp_row, nbatch, n, tq, tkp, p_valid, pv_t, nhp):
    nq = n // tq
    p_pad = kp.shape[3]
    pmap4 = (lambda b, hp: (0, hp, 0, 0)) if kp.shape[0] == 1 else (lambda b, hp: (b, hp, 0, 0))
    cmap4 = (lambda b, hp: (0, hp, 0, 0)) if cp_row.shape[0] == 1 else (lambda b, hp: (b, hp, 0, 0))
    nh = 2 * nhp
    tok = pl.BlockSpec((n, nhp * LANES), lambda b, hp: (b, hp))
    tlanes = pl.BlockSpec((1, nhp * LANES, n), lambda b, hp: (b, hp, 0))
    nbuf = min(nq, 2)
    return pl.pallas_call(
        functools.partial(_fox_kernel, n=n, tq=tq, tkp=tkp, p_pad=p_pad, p_valid=p_valid, pv_t=pv_t, nhp=nhp),
        grid=(nbatch, FOX_HEADS // nh),
        in_specs=[
            tok, tlanes, tlanes,
            pl.BlockSpec((1, nh, FOX_DH, p_pad), pmap4),
            pl.BlockSpec((1, nh, FOX_DH, p_pad), pmap4),
            pl.BlockSpec((1, nhp, 2, n), lambda b, hp: (b, hp, 0, 0)),
            pl.BlockSpec((1, nhp, 2, p_pad), cmap4),
        ],
        out_specs=tok,
        out_shape=jax.ShapeDtypeStruct((nbatch * n, FOX_W), BF16),
        scratch_shapes=([pltpu.VMEM((nh, LANES, p_pad + n), BF16), pltpu.VMEM((nh, FOX_DH + 16, p_pad + n), BF16)]
                        + [pltpu.VMEM((tq, p_pad + n), F32)] * (nh * nbuf)
                        + [pltpu.VMEM((tq, p_pad + n), BF16)] * (nh * nbuf)
                        + [pltpu.VMEM((tq, LANES), F32)] * (nh * nbuf)),
        compiler_params=pltpu.CompilerParams(
            dimension_semantics=("arbitrary", "arbitrary"), vmem_limit_bytes=VMEM_LIMIT),
        name="fox",
    )(q, kn, vn, kp, vp, c_row, cp_row)


def _merge_ffn_kernel(x_ref, ygla_ref, ofox_ref, sg_ref, wbg_ref, wbf_ref, wout_ref, gffn_ref,
                      wup_ref, wdown_ref, gfin_ref, y_ref, *, ff_chunk):
    ya = _dot(ygla_ref[...], wbg_ref[...])
    yb = _dot(ofox_ref[...], wbf_ref[...])
    m = sg_ref[:, 0:D_MODEL].astype(F32) * ya + sg_ref[:, D_MODEL:2 * D_MODEL].astype(F32) * yb
    h = x_ref[...] + _dot(m.astype(BF16), wout_ref[...])
    ms = jnp.mean(h * h, axis=-1, keepdims=True)
    hn = ((h * lax.rsqrt(ms + EPS)) * gffn_ref[...]).astype(BF16)
    for c in range(D_FF // ff_chunk):
        u = _dot(hn, wup_ref[:, c * ff_chunk:(c + 1) * ff_chunk])
        r = jnp.square(jnp.maximum(u, 0.0)).astype(BF16)
        h = h + _dot(r, wdown_ref[c * ff_chunk:(c + 1) * ff_chunk, :])
    ms = jnp.mean(h * h, axis=-1, keepdims=True)
    y_ref[...] = (h * lax.rsqrt(ms + EPS)) * gfin_ref[...]


def _merge_ffn(x2d, ygla, ofox, sg, w, tm):
    t_total = x2d.shape[0]
    tok = lambda width: pl.BlockSpec((tm, width), lambda i: (i, 0))
    return pl.pallas_call(
        functools.partial(_merge_ffn_kernel, ff_chunk=FFN_CHUNK),
        grid=(t_total // tm,),
        in_specs=[
            tok(D_MODEL), tok(GLA_V), tok(FOX_W), tok(2 * D_MODEL),
            _const_spec((GLA_V, D_MODEL)), _const_spec((FOX_W, D_MODEL)),
            _const_spec((D_MODEL, D_MODEL)), _const_spec((1, D_MODEL)),
            _const_spec((D_MODEL, D_FF)), _const_spec((D_FF, D_MODEL)), _const_spec((1, D_MODEL)),
        ],
        out_specs=tok(D_MODEL),
        out_shape=jax.ShapeDtypeStruct((t_total, D_MODEL), F32),
        compiler_params=pltpu.CompilerParams(
            dimension_semantics=("arbitrary",), vmem_limit_bytes=VMEM_LIMIT),
        name="merge_ffn",
    )(x2d, ygla, ofox, sg, w["w_bg"], w["w_bf"], w["w_out"], w["gn_ffn"], w["w_up"], w["w_down"],
      w["gn_final"])


def _prep_weights(norm_mix, w_in, w_gla_gate, b_gla_gate, g_gla_norm, b_fox_forget, w_branch_gla,
                  w_branch_fox, w_out, norm_ffn, w_up, w_down, norm_final):
    sizes = (GLA_QK, GLA_QK, GLA_V, GLA_V, GLA_RANK, FOX_W, FOX_W, FOX_W, FOX_HEADS, D_MODEL, D_MODEL)
    offs = [0]
    for s in sizes:
        offs.append(offs[-1] + s)
    col = lambda i: w_in[:, offs[i]:offs[i + 1]]
    gq, gk, gv, gr, glr, fq, fk, fv, ff, ga, gb = (col(i) for i in range(11))
    w_nn = jnp.concatenate([p.astype(BF16) for p in (
        gq * (GLA_DK ** -0.5), gv, gr, fq * (FOX_DH ** -0.5 * LOG2E), ga, gb)], axis=1)
    pad = jnp.zeros((D_MODEL, _R_FK - _R_FF - FOX_HEADS), F32)
    w_t = jnp.concatenate([gk, glr, ff, pad, fk, fv], axis=1).T.astype(BF16)
    return dict(
        gn_mix=norm_mix.reshape(1, D_MODEL), w_nn=w_nn, w_t=w_t,
        w_gate_t=w_gla_gate.T.astype(BF16), b_gate_col=b_gla_gate.reshape(GLA_QK, 1),
        b_f_col=b_fox_forget.reshape(FOX_HEADS, 1),
        w_bg=(jnp.tile(g_gla_norm, GLA_HEADS)[:, None] * w_branch_gla).astype(BF16), w_bf=w_branch_fox.astype(BF16), w_out=w_out.astype(BF16),
        gn_ffn=norm_ffn.reshape(1, D_MODEL), w_up=w_up.astype(BF16), w_down=w_down.astype(BF16),
        gn_final=norm_final.reshape(1, D_MODEL),
    )


def _pad_lanes(x):
    n = x.shape[-1]
    npad = -(-n // LANES) * LANES
    return x if npad == n else jnp.pad(x, [(0, 0)] * (x.ndim - 1) + [(0, npad - n)])


def kernel(x_prompt, x_sample, cache_fox_k, cache_fox_v, cache_fox_logf, state_gla, meta_tokens,
           norm_mix, w_in, w_gla_gate, b_gla_gate, g_gla_norm, b_fox_forget, w_branch_gla,
           w_branch_fox, w_out, norm_ffn, w_up, w_down, norm_final):
    bsz, seq, _ = x_prompt.shape
    dbsz, dseq, _ = x_sample.shape
    past = cache_fox_k.shape[3]
    w = _prep_weights(norm_mix[0], w_in[0], w_gla_gate[0], b_gla_gate[0], g_gla_norm[0],
                      b_fox_forget[0], w_branch_gla[0], w_branch_fox[0], w_out[0], norm_ffn[0],
                      w_up[0], w_down[0], norm_final)
    xp = x_prompt.reshape(bsz * seq, D_MODEL)
    xs = x_sample.reshape(dbsz * dseq, D_MODEL)

    (m_qvr, m_kT, m_laT, m_logf, _, _, _, m_kT32, m_vT32, _) = _inproj(
        meta_tokens.astype(F32), 1, N_META, 1, N_META, N_META, w)
    zero_state = jnp.zeros((1, GLA_HEADS, GLA_DK, GLA_DV), F32)
    _, s_meta = _gla(m_qvr, m_kT, m_laT, zero_state, 1, N_META, N_META, N_META, False)
    kp, vp = _pad_lanes(m_kT32), _pad_lanes(m_vT32)

    tm = ROW_TILE
    lead = (kp.reshape(FOX_W, LANES), vp.reshape(FOX_W, LANES))
    (p_qvr, p_kT, p_laT, p_logf, p_fq, p_fkT, p_fvT, p_kT32, p_vT32, p_sg) = _inproj(
        xp, bsz, seq, 1, tm, LANES, w, lead=lead)
    y_gla, s_p = _gla(p_qvr, p_kT, p_laT, s_meta, bsz, seq, LANES, CHUNK, True)
    lp = _pad_lanes(jnp.broadcast_to(m_logf, (bsz, FOX_HEADS, N_META)).reshape(bsz * FOX_HEADS, N_META))
    cp, cn = _decay(lp, p_logf.reshape(bsz * FOX_HEADS, seq), N_META)
    tq = FOX_TQ
    c_row = cn.reshape(bsz, FOX_HEADS // 2, 2, seq)
    cp_row = cp[:FOX_HEADS].reshape(1, FOX_HEADS // 2, 2, LANES)
    o_fox = _fox(p_fq, p_fkT, p_fvT, kp, vp, c_row, cp_row, bsz, seq, tq, LANES, N_META, True,
                 FOX_PAIRS_PROMPT)
    y_prompt = _merge_ffn(xp, y_gla, o_fox, p_sg, w, tm).reshape(bsz, seq, D_MODEL)

    (s_qvr, s_kT, s_laT, s_logf, s_fq, s_fkT, s_fvT, s_kT32, s_vT32, s_sg) = _inproj(
        xs, dbsz, dseq, dbsz, dseq, dseq, w)
    ys_gla, s_s = _gla(s_qvr, s_kT, s_laT, state_gla[0].astype(F32), dbsz, dseq, dseq, dseq, True)
    cps, cns = _decay(cache_fox_logf[0].astype(F32).reshape(dbsz * FOX_HEADS, past),
                      _pad_lanes(s_logf.reshape(dbsz * FOX_HEADS, dseq)), past)
    cs_row = cns[:, :dseq].reshape(dbsz, FOX_HEADS // 2, 2, dseq)
    cps_row = cps.reshape(dbsz, FOX_HEADS // 2, 2, past)
    os_fox = _fox(s_fq, s_fkT, s_fvT, jnp.swapaxes(cache_fox_k[0], 2, 3).astype(F32),
                  jnp.swapaxes(cache_fox_v[0], 2, 3).astype(F32), cs_row, cps_row,
                  dbsz, dseq, dseq, FOX_PREFIX_CHUNK, past, False, FOX_PAIRS_SAMPLE)
    y_sample = _merge_ffn(xs, ys_gla, os_fox, s_sg, w, dbsz * dseq).reshape(dbsz, dseq, D_MODEL)

    to_state = lambda a: jnp.swapaxes(a, 2, 3)[None]
    new_fox_logf_prompt = jnp.concatenate(
        [jnp.broadcast_to(m_logf, (bsz, FOX_HEADS, N_META)), p_logf], axis=2)[None]
    return (y_prompt, y_sample, to_state(p_kT32), to_state(p_vT32), new_fox_logf_prompt,
            s_p[None], to_state(s_kT32), to_state(s_vT32), s_logf[None], s_s[None])
```

```python
import functools

import jax
import jax.numpy as jnp
from jax import lax
from jax.experimental import pallas as pl
from jax.experimental.pallas import tpu as pltpu

F32 = jnp.float32
BF16 = jnp.bfloat16

D_MODEL = 1024
N_META = 16
EPS = 1e-6
GLA_HEADS = 4
GLA_DK = 64
GLA_DV = 128
GLA_RANK = 16
GLA_TAU = 16.0
GLA_QK = GLA_HEADS * GLA_DK
GLA_V = GLA_HEADS * GLA_DV
FOX_HEADS = 8
FOX_DH = 64
FOX_W = FOX_HEADS * FOX_DH
D_FF = 4 * D_MODEL
NEG = -1e30
CHUNK = 64
LOG2E = 1.4426950408889634

LANES = 128
VMEM_LIMIT = 56 * 1024 * 1024

ROW_TILE = 512
FFN_CHUNK = 1024
FOX_TQ = 256
FOX_PREFIX_CHUNK = 256
FOX_PAIRS_PROMPT = 2
FOX_PAIRS_SAMPLE = FOX_HEADS // 2
GLA_BLOCKS_PER_TRIP = 8

_C_GQ, _C_GV, _C_GR, _C_FQ, _C_GA, _C_GB, _C_END = 0, 256, 768, 1280, 1792, 2816, 3840
_R_K, _R_LR, _R_FF, _R_FK, _R_FV, _R_END = 0, 256, 272, 288, 800, 1312


def _dot(a, b):
    return jnp.dot(a, b, preferred_element_type=F32)


def _dot_nt(a, b):
    return lax.dot_general(a, b, (((1,), (1,)), ((), ())), preferred_element_type=F32)


def _split3(x):
    hi = x.astype(BF16)
    r = x - hi.astype(F32)
    mid = r.astype(BF16)
    lo = (r - mid.astype(F32)).astype(BF16)
    return hi, mid, lo


def _dot3(x, m):
    hi, mid, lo = _split3(x)
    return _dot(hi, m) + _dot(mid, m) + _dot(lo, m)


def _log_sigmoid(x):
    return jnp.minimum(x, 0.0) - jnp.log1p(jnp.exp(-jnp.abs(x)))


def _const_spec(shape):
    nd = len(shape)
    return pl.BlockSpec(shape, lambda *_: (0,) * nd, pipeline_mode=pl.Buffered(1))


def _inproj_kernel(x_ref, gn_ref, wnn_ref, wt_ref, wg_ref, bg_ref, bf_ref, *rest, nb, tmb, tw, nt, row_off):
    if row_off:
        lead_k_ref, lead_v_ref, x_first_ref = rest[:3]
        rest = rest[3:]
    (qvr_ref, kT_ref, laT_ref, logf_ref, fq_ref, fkT_ref, fvT_ref, k32_ref, v32_ref, sg_ref) = rest[:10]
    carries, hn_scr = rest[10:12], (rest[12] if row_off else None)
    tm = nb * tmb

    def normed(xr):
        x = xr[...]
        ms = jnp.mean(x * x, axis=-1, keepdims=True)
        return ((x * lax.rsqrt(ms + EPS)) * gn_ref[...]).astype(BF16)

    def project():
        if row_off:
            step = pl.program_id(0) * nt + pl.program_id(1)
            hn = hn_scr[lax.rem(step, 2)]
        else:
            hn = normed(x_ref)

        zT = _dot_nt(wt_ref[...], hn)
        if row_off:
            hn_scr[lax.rem(step + 1, 2)] = normed(x_ref)
        kT = zT[_R_K:_R_LR].astype(BF16)
        glrT = zT[_R_LR:_R_FF].astype(BF16)
        laT = _log_sigmoid(_dot(wg_ref[...], glrT) + bg_ref[...]) * (1.0 / GLA_TAU)
        logf = _log_sigmoid(zT[_R_FF:_R_FF + FOX_HEADS] + bf_ref[...])
        zk, zv = zT[_R_FK:_R_FV], zT[_R_FV:_R_END]
        for bb in range(nb):
            r0 = bb * tmb
            logf_ref[bb] = logf[:, r0:r0 + tmb]
            fkT_ref[bb] = zk[:, r0:r0 + tmb].astype(BF16)
            fvT_ref[bb] = zv[:, r0:r0 + tmb].astype(BF16)
            kT_ref[bb] = kT[:, r0:r0 + tmb]
            laT_ref[bb] = laT[:, r0:r0 + tmb]
            if not row_off:
                for h in range(FOX_HEADS):
                    k32_ref[bb, h] = zk[h * FOX_DH:(h + 1) * FOX_DH, r0:r0 + tmb]
                    v32_ref[bb, h] = zv[h * FOX_DH:(h + 1) * FOX_DH, r0:r0 + tmb]
        if row_off:
            lane = lax.broadcasted_iota(jnp.int32, (FOX_W, LANES), 1)
            for z, carry, o_ref in ((zk, carries[0], k32_ref), (zv, carries[1], v32_ref)):
                rolled = pltpu.roll(z, row_off, axis=1)
                first = jnp.where(lane < row_off, carry[...], rolled[:, 0:LANES])
                carry[...] = rolled[:, 0:LANES]
                for h in range(FOX_HEADS):
                    hr = slice(h * FOX_DH, (h + 1) * FOX_DH)
                    o_ref[0, h, :, 0:LANES] = first[hr]
                    o_ref[0, h, :, LANES:tm] = rolled[hr, LANES:tm]

        sg_ref[...] = jax.nn.sigmoid(_dot(hn, wnn_ref[:, _C_GA:_C_END])).astype(BF16)
        zq = _dot(hn, wnn_ref[:, _C_GQ:_C_FQ])
        qvr_ref[:, _C_GQ:_C_GR] = zq[:, _C_GQ:_C_GR].astype(BF16)
        qvr_ref[:, _C_GR:_C_FQ] = jax.nn.silu(zq[:, _C_GR:_C_FQ]).astype(BF16)
        fq_ref[...] = _dot(hn, wnn_ref[:, _C_FQ:_C_GA]).astype(BF16)

    if not row_off:
        project()
        return

    i = pl.program_id(1)

    @pl.when(i == 0)
    def _():
        carries[0][...] = lead_k_ref[...]
        carries[1][...] = lead_v_ref[...]

    @pl.when((pl.program_id(0) == 0) & (i == 0))
    def _():
        hn_scr[0] = normed(x_first_ref)

    pl.when(i < nt)(project)

    @pl.when(i == nt)
    def _():
        for carry, o_ref in ((carries[0], k32_ref), (carries[1], v32_ref)):
            for h in range(FOX_HEADS):
                o_ref[0, h, :, 0:LANES] = carry[h * FOX_DH:(h + 1) * FOX_DH, :]


def _inproj(x2d, nbatch, n, nb, tmb, tw, w, lead=None):
    tm = nb * tmb
    ngrp, nt = nbatch // nb, n // tmb
    t_total = nbatch * n
    row_off = N_META if lead is not None else 0
    steps = nt + 1 if row_off else nt
    blk = (lambda i: jnp.minimum(i, nt - 1)) if row_off else (lambda i: i)
    tok = lambda width: pl.BlockSpec((tm, width), lambda g, i: (g * nt + blk(i), 0))
    out_shape = (
        jax.ShapeDtypeStruct((t_total, _C_FQ), BF16),
        jax.ShapeDtypeStruct((nbatch, GLA_QK, n), BF16),
        jax.ShapeDtypeStruct((nbatch, GLA_QK, n), F32),
        jax.ShapeDtypeStruct((nbatch, FOX_HEADS, n), F32),
        jax.ShapeDtypeStruct((t_total, FOX_W), BF16),
        jax.ShapeDtypeStruct((nbatch, FOX_W, n), BF16),
        jax.ShapeDtypeStruct((nbatch, FOX_W, n), BF16),
        jax.ShapeDtypeStruct((nbatch, FOX_HEADS, FOX_DH, row_off + n), F32),
        jax.ShapeDtypeStruct((nbatch, FOX_HEADS, FOX_DH, row_off + n), F32),
        jax.ShapeDtypeStruct((t_total, 2 * D_MODEL), BF16),
    )
    tblk = pl.BlockSpec((nb, GLA_QK, tmb), lambda g, i: (g, 0, blk(i)))
    lblk = pl.BlockSpec((nb, FOX_W, tmb), lambda g, i: (g, 0, blk(i)))
    sblk = pl.BlockSpec((nb, FOX_HEADS, FOX_DH, tmb), lambda g, i: (g, 0, 0, i))
    out_specs = (
        tok(_C_FQ), tblk, tblk,
        pl.BlockSpec((nb, FOX_HEADS, tmb), lambda g, i: (g, 0, blk(i))),
        tok(FOX_W), lblk, lblk, sblk, sblk, tok(2 * D_MODEL),
    )
    if row_off:
        x_spec = pl.BlockSpec((tm, D_MODEL), lambda g, i: (jnp.minimum(g * nt + blk(i) + 1, ngrp * nt - 1), 0))
    else:
        x_spec = tok(D_MODEL)
    in_specs = [
        x_spec,
        _const_spec((1, D_MODEL)),
        _const_spec((D_MODEL, _C_END)),
        _const_spec((_R_END, D_MODEL)),
        _const_spec((GLA_QK, GLA_RANK)),
        _const_spec((GLA_QK, 1)),
        _const_spec((FOX_HEADS, 1)),
    ]
    args = [x2d, w["gn_mix"], w["w_nn"], w["w_t"], w["w_gate_t"], w["b_gate_col"], w["b_f_col"]]
    scratch = []
    if row_off:
        assert nb == 1 and tmb % LANES == 0
        in_specs += [_const_spec((FOX_W, LANES))] * 2 + [_const_spec((tm, D_MODEL))]
        args += list(lead) + [x2d]
        scratch = [pltpu.VMEM((FOX_W, LANES), F32)] * 2 + [pltpu.VMEM((2, tm, D_MODEL), BF16)]
    return pl.pallas_call(
        functools.partial(_inproj_kernel, nb=nb, tmb=tmb, tw=tw, nt=nt, row_off=row_off),
        grid=(ngrp, steps), in_specs=in_specs, out_specs=out_specs, out_shape=out_shape,
        scratch_shapes=scratch,
        compiler_params=pltpu.CompilerParams(
            dimension_semantics=("arbitrary", "arbitrary"), vmem_limit_bytes=VMEM_LIMIT),
        name="inproj",
    )(*args)


def _cumsum_lanes(x, carry):
    n = x.shape[1]
    r = lax.broadcasted_iota(jnp.int32, (LANES, LANES), 0)
    c = lax.broadcasted_iota(jnp.int32, (LANES, LANES), 1)
    upper = jnp.where(r <= c, 1.0, 0.0).astype(BF16)
    outs = []
    for j in range(n // LANES):
        cj = _dot3(x[:, j * LANES:(j + 1) * LANES], upper) + carry
        outs.append(cj)
        carry = cj[:, LANES - 1:LANES]
    return jnp.concatenate(outs, axis=1), carry


def _decay_kernel(lp_ref, ln_ref, cp_ref, cn_ref, *, p_valid):
    lp = lp_ref[...]
    lane = lax.broadcasted_iota(jnp.int32, lp.shape, 1)
    lp = jnp.where(lane < p_valid, lp, 0.0)
    zero = jnp.zeros((lp.shape[0], 1), F32)
    cp, tot = _cumsum_lanes(lp, zero)
    cp_ref[...] = cp * LOG2E
    cn, _ = _cumsum_lanes(ln_ref[...], tot)
    cn_ref[...] = cn * LOG2E


def _decay(logf_p, logf_n, p_valid):
    r, pp = logf_p.shape
    npad = logf_n.shape[1]
    return pl.pallas_call(
        functools.partial(_decay_kernel, p_valid=p_valid),
        out_shape=(jax.ShapeDtypeStruct((r, pp), F32), jax.ShapeDtypeStruct((r, npad), F32)),
        compiler_params=pltpu.CompilerParams(vmem_limit_bytes=VMEM_LIMIT),
        name="decay",
    )(logf_p, logf_n)


def _gla_kernel(qvr_ref, kT_ref, laT_ref, s0_ref, y_ref, sout_ref, *, n, tw, chunk, need_y):
    nblk = n // tw
    ncb = tw // chunk

    sr = lax.broadcasted_iota(jnp.int32, (tw, LANES), 0)
    tr = lax.broadcasted_iota(jnp.int32, (tw, tw), 0)
    tc = lax.broadcasted_iota(jnp.int32, (tw, tw), 1)
    m_cum = ((tr // chunk) == (tc // chunk)) & (tr <= tc)
    m_all = jnp.concatenate(
        [jnp.where((sr // chunk) == ci, 1.0, 0.0) for ci in range(ncb)] + [jnp.where(m_cum, 1.0, 0.0)],
        axis=1).astype(BF16)
    stack_k = tw % LANES == 0
    m_rhs = jnp.concatenate([m_all] * 2, axis=0) if stack_k else m_all
    lane_chunk = lax.broadcasted_iota(jnp.int32, (GLA_QK, tw), 1) // chunk
    zero_blk = jnp.zeros((GLA_DK, GLA_DV), BF16)

    def group(jg, state):
        chunks = []
        for t in range(nbg):
            jb = jg * nbg + t
            cols = pl.ds(pl.multiple_of(jb * tw, tw), tw) if nblk > 1 else slice(0, tw)
            la = laT_ref[0, :, cols]
            hi = la.astype(BF16)
            lo = (la - hi.astype(F32)).astype(BF16)
            if stack_k:
                g = _dot(jnp.concatenate([hi, lo], axis=1), m_rhs)
            else:
                g = _dot(hi, m_rhs) + _dot(lo, m_rhs)
            tot = [g[:, ci * LANES:(ci + 1) * LANES] for ci in range(ncb)]
            g_cum = g[:, ncb * LANES:ncb * LANES + tw]
            g_tot = tot[0][:, 0:tw] if tw <= LANES else jnp.concatenate([tot[0]] * (tw // LANES), axis=1)
            for ci in range(1, ncb):
                g_tot = jnp.where(lane_chunk == ci, tot[ci][:, 0:tw], g_tot)
            kdec = (kT_ref[0, :, cols].astype(F32) * jnp.exp(g_tot - g_cum)).astype(BF16)
            for ci in range(ncb):
                rows = pl.ds(pl.multiple_of(jb * tw, tw) + ci * chunk, chunk)
                v = qvr_ref[rows, _C_GV:_C_GR]
                us = [_dot(kdec[h * GLA_DK:(h + 1) * GLA_DK, ci * chunk:(ci + 1) * chunk],
                           v[:, h * GLA_DV:(h + 1) * GLA_DV]) for h in range(GLA_HEADS)]
                chunks.append((rows, jnp.exp(tot[ci]), us))
        s_bds = []
        for rows, a, us in chunks:
            state = tuple(a[h * GLA_DK:(h + 1) * GLA_DK] * state[h] + us[h] for h in range(GLA_HEADS))
            if need_y:
                s_bds.append(jnp.concatenate(
                    [jnp.concatenate([state[h].astype(BF16) if hc == h else zero_blk
                                      for hc in range(GLA_HEADS)], axis=1) for h in range(GLA_HEADS)], axis=0))
        if need_y:
            for (rows, _, _), s_bd in zip(chunks, s_bds):
                o = _dot(qvr_ref[rows, _C_GQ:_C_GV], s_bd)
                gate = qvr_ref[rows, _C_GR:_C_FQ].astype(F32)
                parts = []
                for h in range(GLA_HEADS):
                    oh = o[:, h * GLA_DV:(h + 1) * GLA_DV]
                    msq = jnp.mean(oh * oh, axis=-1, keepdims=True)
                    parts.append(oh * lax.rsqrt(msq + EPS))
                y_ref[rows, :] = (jnp.concatenate(parts, axis=1) * gate).astype(BF16)
        return state

    nbg = GLA_BLOCKS_PER_TRIP if nblk % GLA_BLOCKS_PER_TRIP == 0 else 1
    state = tuple(s0_ref[0, h] for h in range(GLA_HEADS))
    state = lax.fori_loop(0, nblk // nbg, group, state) if nblk > 1 else group(0, state)
    for h in range(GLA_HEADS):
        sout_ref[0, h] = state[h]


def _gla(qvr, kT, laT, s0, nbatch, n, tw, chunk, need_y):
    s0_map = (lambda b: (0, 0, 0, 0)) if s0.shape[0] == 1 else (lambda b: (b, 0, 0, 0))
    kern = functools.partial(_gla_kernel, n=n, tw=tw, chunk=chunk, need_y=need_y)
    if not need_y:
        def kern(qvr_ref, kT_ref, laT_ref, s0_ref, sout_ref):
            _gla_kernel(qvr_ref, kT_ref, laT_ref, s0_ref, None, sout_ref,
                        n=n, tw=tw, chunk=chunk, need_y=False)
    state_shape = (GLA_HEADS, GLA_DK, GLA_DV)
    out_shape = [jax.ShapeDtypeStruct((nbatch,) + state_shape, F32)]
    out_specs = [pl.BlockSpec((1,) + state_shape, lambda b: (b, 0, 0, 0))]
    if need_y:
        out_shape = [jax.ShapeDtypeStruct((nbatch * n, GLA_V), BF16)] + out_shape
        out_specs = [pl.BlockSpec((n, GLA_V), lambda b: (b, 0))] + out_specs
    res = pl.pallas_call(
        kern, grid=(nbatch,),
        in_specs=[
            pl.BlockSpec((n, _C_FQ), lambda b: (b, 0)),
            pl.BlockSpec((1, GLA_QK, n), lambda b: (b, 0, 0)),
            pl.BlockSpec((1, GLA_QK, n), lambda b: (b, 0, 0)),
            pl.BlockSpec((1,) + state_shape, s0_map),
        ],
        out_specs=out_specs, out_shape=out_shape,
        compiler_params=pltpu.CompilerParams(
            dimension_semantics=("arbitrary",), vmem_limit_bytes=VMEM_LIMIT),
        name="gla",
    )(qvr, kT, laT, s0)
    return res if need_y else (None, res[0])


def _fold_lanes(x, op):
    w = x.shape[1]
    if w % LANES == 0:
        f = x[:, 0:LANES]
        for g in range(1, w // LANES):
            f = op(f, x[:, g * LANES:(g + 1) * LANES])
        return f
    assert op is jnp.add
    lane = lax.broadcasted_iota(jnp.int32, (x.shape[0], LANES), 1)
    return jnp.where(lane == 0, jnp.sum(x, axis=-1, keepdims=True), 0.0)


def _tile_lanes(x, w):
    return x[:, 0:w] if w < LANES else jnp.concatenate([x] * (w // LANES), axis=1)


def _fox_kernel(q_ref, kn_ref, vn_ref, kp_ref, vp_ref, crow_ref, cprow_ref, o_ref, ka_scr, va_scr, *scr,
                n, tq, tkp, p_pad, p_valid, pv_t, nhp):
    nq = n // tq
    rb = min(tq, 64)
    nh = 2 * nhp
    nbuf = len(scr) // (3 * nh)
    s_scr, p_scr, sh_scr = (scr[i * nh * nbuf:(i + 1) * nh * nbuf] for i in range(3))
    nparts = 3
    ones_rows = 16

    def c_block(c_row):
        parts = _split3(-c_row)
        r = lax.broadcasted_iota(jnp.int32, (FOX_DH, c_row.shape[1]), 0)
        blk = jnp.zeros((FOX_DH, c_row.shape[1]), F32)
        for i, part in enumerate(parts):
            blk = jnp.where(r == i, part.astype(F32), blk)
        return blk.astype(BF16)

    for h in range(nh):
        hp, hh = divmod(h, 2)
        hrows = slice(h * FOX_DH, (h + 1) * FOX_DH)
        for k_src, c_src, c0, w in ((kp_ref[0, h].astype(BF16), cprow_ref[0, hp, hh:hh + 1, :], 0, p_pad),
                                    (kn_ref[0, hrows, :], crow_ref[0, hp, hh:hh + 1, :], p_pad, n)):
            halves = (k_src, c_block(c_src)) if hh == 0 else (c_block(c_src), k_src)
            ka_scr[h, :, c0:c0 + w] = jnp.concatenate(halves, axis=0)
        va_scr[h, 0:FOX_DH, 0:p_pad] = vp_ref[0, h].astype(BF16)
        va_scr[h, 0:FOX_DH, p_pad:p_pad + n] = vn_ref[0, hrows, :]
        va_scr[h, FOX_DH:FOX_DH + ones_rows, :] = jnp.ones((ones_rows, p_pad + n), BF16)

    lane_q = lax.broadcasted_iota(jnp.int32, (tq, LANES), 1)
    eye = jnp.where(lax.broadcasted_iota(jnp.int32, (tq, tq), 0) == lax.broadcasted_iota(jnp.int32, (tq, tq), 1),
                    1.0, 0.0).astype(BF16)

    def block_chunks(qi):
        return [(c0, tkp) for c0 in range(0, p_pad, tkp)] + [(p_pad + j * tq, tq) for j in range(qi + 1)]

    def pass1_items(qi):
        rows = slice(qi * tq, (qi + 1) * tq)
        chunks = block_chunks(qi)
        state = {}

        def start():
            c_parts = [part for h in range(nh) for part in _split3(crow_ref[0, h // 2, h % 2:h % 2 + 1, rows])]
            pad_rows = -len(c_parts) % ones_rows
            c_rows = jnp.concatenate(c_parts + [jnp.zeros((pad_rows, tq), BF16)], axis=0)
            state["c_cols"] = _dot_nt(eye, c_rows)
            for h in range(nh):
                hp, hh = divmod(h, 2)
                qp = q_ref[rows, hp * LANES:(hp + 1) * LANES]
                c_lane0 = FOX_DH if hh == 0 else 0
                own = (lane_q < FOX_DH) if hh == 0 else (lane_q >= FOX_DH)
                ones_pat = jnp.where((lane_q >= c_lane0) & (lane_q < c_lane0 + nparts), 1.0, 0.0).astype(BF16)
                state["q", h] = jnp.where(own, qp, ones_pat)
                state["mx", h] = None

        def chunk(h, ci):
            c0, cw = chunks[ci]
            s = _dot(state["q", h], ka_scr[h, :, c0:c0 + cw])
            if c0 < p_pad and c0 + cw > p_valid:
                col = lax.broadcasted_iota(jnp.int32, s.shape, 1) + c0
                s = jnp.where(col < p_valid, s, NEG)
            if ci == len(chunks) - 1:
                r = lax.broadcasted_iota(jnp.int32, s.shape, 0)
                c = lax.broadcasted_iota(jnp.int32, s.shape, 1)
                s = jnp.where(c <= r, s, NEG)
            s_scr[buf(qi, h)][:, c0:c0 + cw] = s
            f = _fold_lanes(s, jnp.maximum) if cw % LANES == 0 else jnp.max(s, axis=-1, keepdims=True)
            mx = state["mx", h]
            if mx is None or mx.shape == f.shape:
                mx = f if mx is None else jnp.maximum(mx, f)
            else:
                mx = jnp.maximum(jnp.max(mx, axis=-1, keepdims=True), jnp.max(f, axis=-1, keepdims=True))
            state["mx", h] = mx

        def finish(h):
            cq = sum(state["c_cols"][:, h * nparts + i:h * nparts + i + 1] for i in range(nparts))
            m = jnp.max(state["mx", h], axis=-1, keepdims=True) + cq
            sh_scr[buf(qi, h)][...] = jnp.broadcast_to(m - cq, (tq, LANES))

        items = [start]
        for ci in range(len(chunks)):
            items += [functools.partial(chunk, h, ci) for h in range(nh)]
        return items + [functools.partial(finish, h) for h in range(nh)]

    def pass2_items(qi):
        def item(r0, h, c0, cw):
            rr = slice(r0, r0 + rb)
            sh = sh_scr[buf(qi, h)][rr, :]
            p = jnp.exp2(s_scr[buf(qi, h)][rr, c0:c0 + cw] - _tile_lanes(sh, cw))
            p_scr[buf(qi, h)][rr, c0:c0 + cw] = p.astype(BF16)
        return [functools.partial(item, r0, h, c0, cw)
                for r0 in range(0, tq, rb) for h in range(nh) for c0, cw in block_chunks(qi)]

    def pv_items(qi):
        w_tot = p_pad + (qi + 1) * tq
        acc = {}

        def matmul(h):
            p_all, v_aug = p_scr[buf(qi, h)][:, 0:w_tot], va_scr[h, :, 0:w_tot]
            acc[h] = _dot_nt(v_aug, p_all) if pv_t else _dot_nt(p_all, v_aug)

        def store():
            if pv_t:
                outs = [acc[h][0:FOX_DH] * (1.0 / acc[h][FOX_DH:FOX_DH + 1]) for h in range(nh)]
                o = jnp.concatenate(outs, axis=0).T
            else:
                outs = [acc[h][:, 0:FOX_DH] * (1.0 / acc[h][:, FOX_DH:FOX_DH + 1]) for h in range(nh)]
                o = jnp.concatenate(outs, axis=1)
            o_ref[qi * tq:(qi + 1) * tq, :] = o.astype(BF16)

        return [functools.partial(matmul, h) for h in range(nh)] + [store]

    def emit_merged(*lists):
        total = max(len(l) for l in lists)
        pos = [0] * len(lists)
        for t in range(total):
            for k, l in enumerate(lists):
                while pos[k] < len(l) and pos[k] * total <= t * len(l):
                    l[pos[k]]()
                    pos[k] += 1
        for k, l in enumerate(lists):
            for item in l[pos[k]:]:
                item()

    buf = lambda qi, h: (qi % nbuf) * nh + h
    emit_merged(pass1_items(0))
    for qi in range(nq):
        emit_merged(pass2_items(qi),
                    pass1_items(qi + 1) if qi + 1 < nq else [],
                    pv_items(qi - 1) if qi > 0 else [])
    emit_merged(pv_items(nq - 1))


def _fox(q, kn, vn, kp, vp, c_row, cp_row, nbatch, n, tq, tkp, p_valid, pv_t, nhp):
    nq = n // tq
    p_pad = kp.shape[3]
    pmap4 = (lambda b, hp: (0, hp, 0, 0)) if kp.shape[0] == 1 else (lambda b, hp: (b, hp, 0, 0))
    cmap4 = (lambda b, hp: (0, hp, 0, 0)) if cp_row.shape[0] == 1 else (lambda b, hp: (b, hp, 0, 0))
    nh = 2 * nhp
    tok = pl.BlockSpec((n, nhp * LANES), lambda b, hp: (b, hp))
    tlanes = pl.BlockSpec((1, nhp * LANES, n), lambda b, hp: (b, hp, 0))
    nbuf = min(nq, 2)
    return pl.pallas_call(
        functools.partial(_fox_kernel, n=n, tq=tq, tkp=tkp, p_pad=p_pad, p_valid=p_valid, pv_t=pv_t, nhp=nhp),
        grid=(nbatch, FOX_HEADS // nh),
        in_specs=[
            tok, tlanes, tlanes,
            pl.BlockSpec((1, nh, FOX_DH, p_pad), pmap4),
            pl.BlockSpec((1, nh, FOX_DH, p_pad), pmap4),
            pl.BlockSpec((1, nhp, 2, n), lambda b, hp: (b, hp, 0, 0)),
            pl.BlockSpec((1, nhp, 2, p_pad), cmap4),
        ],
        out_specs=tok,
        out_shape=jax.ShapeDtypeStruct((nbatch * n, FOX_W), BF16),
        scratch_shapes=([pltpu.VMEM((nh, LANES, p_pad + n), BF16), pltpu.VMEM((nh, FOX_DH + 16, p_pad + n), BF16)]
                        + [pltpu.VMEM((tq, p_pad + n), F32)] * (nh * nbuf)
                        + [pltpu.VMEM((tq, p_pad + n), BF16)] * (nh * nbuf)
                        + [pltpu.VMEM((tq, LANES), F32)] * (nh * nbuf)),
        compiler_params=pltpu.CompilerParams(
            dimension_semantics=("arbitrary", "arbitrary"), vmem_limit_bytes=VMEM_LIMIT),
        name="fox",
    )(q, kn, vn, kp, vp, c_row, cp_row)


def _merge_ffn_kernel(x_ref, ygla_ref, ofox_ref, sg_ref, wbg_ref, wbf_ref, wout_ref, gffn_ref,
                      wup_ref, wdown_ref, gfin_ref, y_ref, *, ff_chunk):
    ya = _dot(ygla_ref[...], wbg_ref[...])
    yb = _dot(ofox_ref[...], wbf_ref[...])
    m = sg_ref[:, 0:D_MODEL].astype(F32) * ya + sg_ref[:, D_MODEL:2 * D_MODEL].astype(F32) * yb
    h = x_ref[...] + _dot(m.astype(BF16), wout_ref[...])
    ms = jnp.mean(h * h, axis=-1, keepdims=True)
    hn = ((h * lax.rsqrt(ms + EPS)) * gffn_ref[...]).astype(BF16)
    for c in range(D_FF // ff_chunk):
        u = _dot(hn, wup_ref[:, c * ff_chunk:(c + 1) * ff_chunk])
        r = jnp.square(jnp.maximum(u, 0.0)).astype(BF16)
        h = h + _dot(r, wdown_ref[c * ff_chunk:(c + 1) * ff_chunk, :])
    ms = jnp.mean(h * h, axis=-1, keepdims=True)
    y_ref[...] = (h * lax.rsqrt(ms + EPS)) * gfin_ref[...]


def _merge_ffn(x2d, ygla, ofox, sg, w, tm):
    t_total = x2d.shape[0]
    tok = lambda width: pl.BlockSpec((tm, width), lambda i: (i, 0))
    return pl.pallas_call(
        functools.partial(_merge_ffn_kernel, ff_chunk=FFN_CHUNK),
        grid=(t_total // tm,),
        in_specs=[
            tok(D_MODEL), tok(GLA_V), tok(FOX_W), tok(2 * D_MODEL),
            _const_spec((GLA_V, D_MODEL)), _const_spec((FOX_W, D_MODEL)),
            _const_spec((D_MODEL, D_MODEL)), _const_spec((1, D_MODEL)),
            _const_spec((D_MODEL, D_FF)), _const_spec((D_FF, D_MODEL)), _const_spec((1, D_MODEL)),
        ],
        out_specs=tok(D_MODEL),
        out_shape=jax.ShapeDtypeStruct((t_total, D_MODEL), F32),
        compiler_params=pltpu.CompilerParams(
            dimension_semantics=("arbitrary",), vmem_limit_bytes=VMEM_LIMIT),
        name="merge_ffn",
    )(x2d, ygla, ofox, sg, w["w_bg"], w["w_bf"], w["w_out"], w["gn_ffn"], w["w_up"], w["w_down"],
      w["gn_final"])


def _prep_weights(norm_mix, w_in, w_gla_gate, b_gla_gate, g_gla_norm, b_fox_forget, w_branch_gla,
                  w_branch_fox, w_out, norm_ffn, w_up, w_down, norm_final):
    sizes = (GLA_QK, GLA_QK, GLA_V, GLA_V, GLA_RANK, FOX_W, FOX_W, FOX_W, FOX_HEADS, D_MODEL, D_MODEL)
    offs = [0]
    for s in sizes:
        offs.append(offs[-1] + s)
    col = lambda i: w_in[:, offs[i]:offs[i + 1]]
    gq, gk, gv, gr, glr, fq, fk, fv, ff, ga, gb = (col(i) for i in range(11))
    w_nn = jnp.concatenate([p.astype(BF16) for p in (
        gq * (GLA_DK ** -0.5), gv, gr, fq * (FOX_DH ** -0.5 * LOG2E), ga, gb)], axis=1)
    pad = jnp.zeros((D_MODEL, _R_FK - _R_FF - FOX_HEADS), F32)
    w_t = jnp.concatenate([gk, glr, ff, pad, fk, fv], axis=1).T.astype(BF16)
    return dict(
        gn_mix=norm_mix.reshape(1, D_MODEL), w_nn=w_nn, w_t=w_t,
        w_gate_t=w_gla_gate.T.astype(BF16), b_gate_col=b_gla_gate.reshape(GLA_QK, 1),
        b_f_col=b_fox_forget.reshape(FOX_HEADS, 1),
        w_bg=(jnp.tile(g_gla_norm, GLA_HEADS)[:, None] * w_branch_gla).astype(BF16), w_bf=w_branch_fox.astype(BF16), w_out=w_out.astype(BF16),
        gn_ffn=norm_ffn.reshape(1, D_MODEL), w_up=w_up.astype(BF16), w_down=w_down.astype(BF16),
        gn_final=norm_final.reshape(1, D_MODEL),
    )


def _pad_lanes(x):
    n = x.shape[-1]
    npad = -(-n // LANES) * LANES
    return x if npad == n else jnp.pad(x, [(0, 0)] * (x.ndim - 1) + [(0, npad - n)])


def kernel(x_prompt, x_sample, cache_fox_k, cache_fox_v, cache_fox_logf, state_gla, meta_tokens,
           norm_mix, w_in, w_gla_gate, b_gla_gate, g_gla_norm, b_fox_forget, w_branch_gla,
           w_branch_fox, w_out, norm_ffn, w_up, w_down, norm_final):
    bsz, seq, _ = x_prompt.shape
    dbsz, dseq, _ = x_sample.shape
    past = cache_fox_k.shape[3]
    w = _prep_weights(norm_mix[0], w_in[0], w_gla_gate[0], b_gla_gate[0], g_gla_norm[0],
                      b_fox_forget[0], w_branch_gla[0], w_branch_fox[0], w_out[0], norm_ffn[0],
                      w_up[0], w_down[0], norm_final)
    xp = x_prompt.reshape(bsz * seq, D_MODEL)
    xs = x_sample.reshape(dbsz * dseq, D_MODEL)

    (m_qvr, m_kT, m_laT, m_logf, _, _, _, m_kT32, m_vT32, _) = _inproj(
        meta_tokens.astype(F32), 1, N_META, 1, N_META, N_META, w)
    zero_state = jnp.zeros((1, GLA_HEADS, GLA_DK, GLA_DV), F32)
    _, s_meta = _gla(m_qvr, m_kT, m_laT, zero_state, 1, N_META, N_META, N_META, False)
    kp, vp = _pad_lanes(m_kT32), _pad_lanes(m_vT32)

    tm = ROW_TILE
    lead = (kp.reshape(FOX_W, LANES), vp.reshape(FOX_W, LANES))
    (p_qvr, p_kT, p_laT, p_logf, p_fq, p_fkT, p_fvT, p_kT32, p_vT32, p_sg) = _inproj(
        xp, bsz, seq, 1, tm, LANES, w, lead=lead)
    y_gla, s_p = _gla(p_qvr, p_kT, p_laT, s_meta, bsz, seq, LANES, CHUNK, True)
    lp = _pad_lanes(jnp.broadcast_to(m_logf, (bsz, FOX_HEADS, N_META)).reshape(bsz * FOX_HEADS, N_META))
    cp, cn = _decay(lp, p_logf.reshape(bsz * FOX_HEADS, seq), N_META)
    tq = FOX_TQ
    c_row = cn.reshape(bsz, FOX_HEADS // 2, 2, seq)
    cp_row = cp[:FOX_HEADS].reshape(1, FOX_HEADS // 2, 2, LANES)
    o_fox = _fox(p_fq, p_fkT, p_fvT, kp, vp, c_row, cp_row, bsz, seq, tq, LANES, N_META, True,
                 FOX_PAIRS_PROMPT)
    y_prompt = _merge_ffn(xp, y_gla, o_fox, p_sg, w, tm).reshape(bsz, seq, D_MODEL)

    (s_qvr, s_kT, s_laT, s_logf, s_fq, s_fkT, s_fvT, s_kT32, s_vT32, s_sg) = _inproj(
        xs, dbsz, dseq, dbsz, dseq, dseq, w)
    ys_gla, s_s = _gla(s_qvr, s_kT, s_laT, state_gla[0].astype(F32), dbsz, dseq, dseq, dseq, True)
    cps, cns = _decay(cache_fox_logf[0].astype(F32).reshape(dbsz * FOX_HEADS, past),
                      _pad_lanes(s_logf.reshape(dbsz * FOX_HEADS, dseq)), past)
    cs_row = cns[:, :dseq].reshape(dbsz, FOX_HEADS // 2, 2, dseq)
    cps_row = cps.reshape(dbsz, FOX_HEADS // 2, 2, past)
    os_fox = _fox(s_fq, s_fkT, s_fvT, jnp.swapaxes(cache_fox_k[0], 2, 3).astype(F32),
                  jnp.swapaxes(cache_fox_v[0], 2, 3).astype(F32), cs_row, cps_row,
                  dbsz, dseq, dseq, FOX_PREFIX_CHUNK, past, False, FOX_PAIRS_SAMPLE)
    y_sample = _merge_ffn(xs, ys_gla, os_fox, s_sg, w, dbsz * dseq).reshape(dbsz, dseq, D_MODEL)

    to_state = lambda a: jnp.swapaxes(a, 2, 3)[None]
    new_fox_logf_prompt = jnp.concatenate(
        [jnp.broadcast_to(m_logf, (bsz, FOX_HEADS, N_META)), p_logf], axis=2)[None]
    return (y_prompt, y_sample, to_state(p_kT32), to_state(p_vT32), new_fox_logf_prompt,
            s_p[None], to_state(s_kT32), to_state(s_vT32), s_logf[None], s_s[None])
```

```python
import functools

import jax
import jax.numpy as jnp
from jax import lax
from jax.experimental import pallas as pl
from jax.experimental.pallas import tpu as pltpu

F32 = jnp.float32
BF16 = jnp.bfloat16

D_MODEL = 1024
N_META = 16
EPS = 1e-6
GLA_HEADS = 4
GLA_DK = 64
GLA_DV = 128
GLA_RANK = 16
GLA_TAU = 16.0
GLA_QK = GLA_HEADS * GLA_DK
GLA_V = GLA_HEADS * GLA_DV
FOX_HEADS = 8
FOX_DH = 64
FOX_W = FOX_HEADS * FOX_DH
D_FF = 4 * D_MODEL
NEG = -1e30
CHUNK = 64
LOG2E = 1.4426950408889634

LANES = 128
VMEM_LIMIT = 56 * 1024 * 1024

ROW_TILE = 512
FFN_CHUNK = 1024
FOX_TQ = 256
FOX_PREFIX_CHUNK = 256
FOX_PAIRS_PROMPT = 2
FOX_PAIRS_SAMPLE = FOX_HEADS // 2
GLA_BLOCKS_PER_TRIP = 8

_C_GQ, _C_GV, _C_GR, _C_FQ, _C_GA, _C_GB, _C_END = 0, 256, 768, 1280, 1792, 2816, 3840
_R_K, _R_LR, _R_FF, _R_FK, _R_FV, _R_END = 0, 256, 272, 288, 800, 1312


def _dot(a, b):
    return jnp.dot(a, b, preferred_element_type=F32)


def _dot_nt(a, b):
    return lax.dot_general(a, b, (((1,), (1,)), ((), ())), preferred_element_type=F32)


def _split3(x):
    hi = x.astype(BF16)
    r = x - hi.astype(F32)
    mid = r.astype(BF16)
    lo = (r - mid.astype(F32)).astype(BF16)
    return hi, mid, lo


def _dot3(x, m):
    hi, mid, lo = _split3(x)
    return _dot(hi, m) + _dot(mid, m) + _dot(lo, m)


def _log_sigmoid(x):
    return jnp.minimum(x, 0.0) - jnp.log1p(jnp.exp(-jnp.abs(x)))


def _const_spec(shape):
    nd = len(shape)
    return pl.BlockSpec(shape, lambda *_: (0,) * nd, pipeline_mode=pl.Buffered(1))


def _inproj_kernel(x_ref, gn_ref, wnn_ref, wt_ref, wg_ref, bg_ref, bf_ref, *rest, nb, tmb, tw, nt, row_off):
    if row_off:
        lead_k_ref, lead_v_ref, x_first_ref = rest[:3]
        rest = rest[3:]
    (qvr_ref, kT_ref, laT_ref, logf_ref, fq_ref, fkT_ref, fvT_ref, k32_ref, v32_ref, sg_ref) = rest[:10]
    carries, hn_scr = rest[10:12], (rest[12] if row_off else None)
    tm = nb * tmb

    def normed(xr):
        x = xr[...]
        ms = jnp.mean(x * x, axis=-1, keepdims=True)
        return ((x * lax.rsqrt(ms + EPS)) * gn_ref[...]).astype(BF16)

    def project():
        if row_off:
            step = pl.program_id(0) * nt + pl.program_id(1)
            hn = hn_scr[lax.rem(step, 2)]
        else:
            hn = normed(x_ref)

        zT = _dot(wt_ref[...], hn.astype(F32).T.astype(BF16))
        if row_off:
            hn_scr[lax.rem(step + 1, 2)] = normed(x_ref)
        kT = zT[_R_K:_R_LR].astype(BF16)
        glrT = zT[_R_LR:_R_FF].astype(BF16)
        laT = _log_sigmoid(_dot(wg_ref[...], glrT) + bg_ref[...]) * (1.0 / GLA_TAU)
        logf = _log_sigmoid(zT[_R_FF:_R_FF + FOX_HEADS] + bf_ref[...])
        zk, zv = zT[_R_FK:_R_FV], zT[_R_FV:_R_END]
        for bb in range(nb):
            r0 = bb * tmb
            logf_ref[bb] = logf[:, r0:r0 + tmb]
            fkT_ref[bb] = zk[:, r0:r0 + tmb].astype(BF16)
            fvT_ref[bb] = zv[:, r0:r0 + tmb].astype(BF16)
            kT_ref[bb] = kT[:, r0:r0 + tmb]
            laT_ref[bb] = laT[:, r0:r0 + tmb]
            if not row_off:
                for h in range(FOX_HEADS):
                    k32_ref[bb, h] = zk[h * FOX_DH:(h + 1) * FOX_DH, r0:r0 + tmb]
                    v32_ref[bb, h] = zv[h * FOX_DH:(h + 1) * FOX_DH, r0:r0 + tmb]
        if row_off:
            lane = lax.broadcasted_iota(jnp.int32, (FOX_W, LANES), 1)
            for z, carry, o_ref in ((zk, carries[0], k32_ref), (zv, carries[1], v32_ref)):
                rolled = pltpu.roll(z, row_off, axis=1)
                first = jnp.where(lane < row_off, carry[...], rolled[:, 0:LANES])
                carry[...] = rolled[:, 0:LANES]
                for h in range(FOX_HEADS):
                    hr = slice(h * FOX_DH, (h + 1) * FOX_DH)
                    o_ref[0, h, :, 0:LANES] = first[hr]
                    o_ref[0, h, :, LANES:tm] = rolled[hr, LANES:tm]

        sg_ref[...] = jax.nn.sigmoid(_dot(hn, wnn_ref[:, _C_GA:_C_END])).astype(BF16)
        zq = _dot(hn, wnn_ref[:, _C_GQ:_C_FQ])
        qvr_ref[:, _C_GQ:_C_GR] = zq[:, _C_GQ:_C_GR].astype(BF16)
        qvr_ref[:, _C_GR:_C_FQ] = jax.nn.silu(zq[:, _C_GR:_C_FQ]).astype(BF16)
        fq_ref[...] = _dot(hn, wnn_ref[:, _C_FQ:_C_GA]).astype(BF16)

    if not row_off:
        project()
        return

    i = pl.program_id(1)

    @pl.when(i == 0)
    def _():
        carries[0][...] = lead_k_ref[...]
        carries[1][...] = lead_v_ref[...]

    @pl.when((pl.program_id(0) == 0) & (i == 0))
    def _():
        hn_scr[0] = normed(x_first_ref)

    pl.when(i < nt)(project)

    @pl.when(i == nt)
    def _():
        for carry, o_ref in ((carries[0], k32_ref), (carries[1], v32_ref)):
            for h in range(FOX_HEADS):
                o_ref[0, h, :, 0:LANES] = carry[h * FOX_DH:(h + 1) * FOX_DH, :]


def _inproj(x2d, nbatch, n, nb, tmb, tw, w, lead=None):
    tm = nb * tmb
    ngrp, nt = nbatch // nb, n // tmb
    t_total = nbatch * n
    row_off = N_META if lead is not None else 0
    steps = nt + 1 if row_off else nt
    blk = (lambda i: jnp.minimum(i, nt - 1)) if row_off else (lambda i: i)
    tok = lambda width: pl.BlockSpec((tm, width), lambda g, i: (g * nt + blk(i), 0))
    out_shape = (
        jax.ShapeDtypeStruct((t_total, _C_FQ), BF16),
        jax.ShapeDtypeStruct((nbatch, GLA_QK, n), BF16),
        jax.ShapeDtypeStruct((nbatch, GLA_QK, n), F32),
        jax.ShapeDtypeStruct((nbatch, FOX_HEADS, n), F32),
        jax.ShapeDtypeStruct((t_total, FOX_W), BF16),
        jax.ShapeDtypeStruct((nbatch, FOX_W, n), BF16),
        jax.ShapeDtypeStruct((nbatch, FOX_W, n), BF16),
        jax.ShapeDtypeStruct((nbatch, FOX_HEADS, FOX_DH, row_off + n), F32),
        jax.ShapeDtypeStruct((nbatch, FOX_HEADS, FOX_DH, row_off + n), F32),
        jax.ShapeDtypeStruct((t_total, 2 * D_MODEL), BF16),
    )
    tblk = pl.BlockSpec((nb, GLA_QK, tmb), lambda g, i: (g, 0, blk(i)))
    lblk = pl.BlockSpec((nb, FOX_W, tmb), lambda g, i: (g, 0, blk(i)))
    sblk = pl.BlockSpec((nb, FOX_HEADS, FOX_DH, tmb), lambda g, i: (g, 0, 0, i))
    out_specs = (
        tok(_C_FQ), tblk, tblk,
        pl.BlockSpec((nb, FOX_HEADS, tmb), lambda g, i: (g, 0, blk(i))),
        tok(FOX_W), lblk, lblk, sblk, sblk, tok(2 * D_MODEL),
    )
    if row_off:
        x_spec = pl.BlockSpec((tm, D_MODEL), lambda g, i: (jnp.minimum(g * nt + blk(i) + 1, ngrp * nt - 1), 0))
    else:
        x_spec = tok(D_MODEL)
    in_specs = [
        x_spec,
        _const_spec((1, D_MODEL)),
        _const_spec((D_MODEL, _C_END)),
        _const_spec((_R_END, D_MODEL)),
        _const_spec((GLA_QK, GLA_RANK)),
        _const_spec((GLA_QK, 1)),
        _const_spec((FOX_HEADS, 1)),
    ]
    args = [x2d, w["gn_mix"], w["w_nn"], w["w_t"], w["w_gate_t"], w["b_gate_col"], w["b_f_col"]]
    scratch = []
    if row_off:
        assert nb == 1 and tmb % LANES == 0
        in_specs += [_const_spec((FOX_W, LANES))] * 2 + [_const_spec((tm, D_MODEL))]
        args += list(lead) + [x2d]
        scratch = [pltpu.VMEM((FOX_W, LANES), F32)] * 2 + [pltpu.VMEM((2, tm, D_MODEL), BF16)]
    return pl.pallas_call(
        functools.partial(_inproj_kernel, nb=nb, tmb=tmb, tw=tw, nt=nt, row_off=row_off),
        grid=(ngrp, steps), in_specs=in_specs, out_specs=out_specs, out_shape=out_shape,
        scratch_shapes=scratch,
        compiler_params=pltpu.CompilerParams(
            dimension_semantics=("arbitrary", "arbitrary"), vmem_limit_bytes=VMEM_LIMIT),
        name="inproj",
    )(*args)


def _cumsum_lanes(x, carry):
    n = x.shape[1]
    r = lax.broadcasted_iota(jnp.int32, (LANES, LANES), 0)
    c = lax.broadcasted_iota(jnp.int32, (LANES, LANES), 1)
    upper = jnp.where(r <= c, 1.0, 0.0).astype(BF16)
    outs = []
    for j in range(n // LANES):
        cj = _dot3(x[:, j * LANES:(j + 1) * LANES], upper) + carry
        outs.append(cj)
        carry = cj[:, LANES - 1:LANES]
    return jnp.concatenate(outs, axis=1), carry


def _decay_kernel(lp_ref, ln_ref, cp_ref, cn_ref, *, p_valid):
    lp = lp_ref[...]
    lane = lax.broadcasted_iota(jnp.int32, lp.shape, 1)
    lp = jnp.where(lane < p_valid, lp, 0.0)
    zero = jnp.zeros((lp.shape[0], 1), F32)
    cp, tot = _cumsum_lanes(lp, zero)
    cp_ref[...] = cp * LOG2E
    cn, _ = _cumsum_lanes(ln_ref[...], tot)
    cn_ref[...] = cn * LOG2E


def _decay(logf_p, logf_n, p_valid):
    r, pp = logf_p.shape
    npad = logf_n.shape[1]
    return pl.pallas_call(
        functools.partial(_decay_kernel, p_valid=p_valid),
        out_shape=(jax.ShapeDtypeStruct((r, pp), F32), jax.ShapeDtypeStruct((r, npad), F32)),
        compiler_params=pltpu.CompilerParams(vmem_limit_bytes=VMEM_LIMIT),
        name="decay",
    )(logf_p, logf_n)


def _gla_kernel(qvr_ref, kT_ref, laT_ref, s0_ref, y_ref, sout_ref, *, n, tw, chunk, need_y):
    nblk = n // tw
    ncb = tw // chunk

    sr = lax.broadcasted_iota(jnp.int32, (tw, LANES), 0)
    tr = lax.broadcasted_iota(jnp.int32, (tw, tw), 0)
    tc = lax.broadcasted_iota(jnp.int32, (tw, tw), 1)
    m_cum = ((tr // chunk) == (tc // chunk)) & (tr <= tc)
    m_all = jnp.concatenate(
        [jnp.where((sr // chunk) == ci, 1.0, 0.0) for ci in range(ncb)] + [jnp.where(m_cum, 1.0, 0.0)],
        axis=1).astype(BF16)
    stack_k = tw % LANES == 0
    m_rhs = jnp.concatenate([m_all] * 2, axis=0) if stack_k else m_all
    lane_chunk = lax.broadcasted_iota(jnp.int32, (GLA_QK, tw), 1) // chunk
    zero_blk = jnp.zeros((GLA_DK, GLA_DV), BF16)

    def group(jg, state):
        chunks = []
        for t in range(nbg):
            jb = jg * nbg + t
            cols = pl.ds(pl.multiple_of(jb * tw, tw), tw) if nblk > 1 else slice(0, tw)
            la = laT_ref[0, :, cols]
            hi = la.astype(BF16)
            lo = (la - hi.astype(F32)).astype(BF16)
            if stack_k:
                g = _dot(jnp.concatenate([hi, lo], axis=1), m_rhs)
            else:
                g = _dot(hi, m_rhs) + _dot(lo, m_rhs)
            tot = [g[:, ci * LANES:(ci + 1) * LANES] for ci in range(ncb)]
            g_cum = g[:, ncb * LANES:ncb * LANES + tw]
            g_tot = tot[0][:, 0:tw] if tw <= LANES else jnp.concatenate([tot[0]] * (tw // LANES), axis=1)
            for ci in range(1, ncb):
                g_tot = jnp.where(lane_chunk == ci, tot[ci][:, 0:tw], g_tot)
            kdec = (kT_ref[0, :, cols].astype(F32) * jnp.exp(g_tot - g_cum)).astype(BF16)
            for ci in range(ncb):
                rows = pl.ds(pl.multiple_of(jb * tw, tw) + ci * chunk, chunk)
                v = qvr_ref[rows, _C_GV:_C_GR]
                us = [_dot(kdec[h * GLA_DK:(h + 1) * GLA_DK, ci * chunk:(ci + 1) * chunk],
                           v[:, h * GLA_DV:(h + 1) * GLA_DV]) for h in range(GLA_HEADS)]
                chunks.append((rows, jnp.exp(tot[ci]), us))
        s_bds = []
        for rows, a, us in chunks:
            state = tuple(a[h * GLA_DK:(h + 1) * GLA_DK] * state[h] + us[h] for h in range(GLA_HEADS))
            if need_y:
                s_bds.append(jnp.concatenate(
                    [jnp.concatenate([state[h].astype(BF16) if hc == h else zero_blk
                                      for hc in range(GLA_HEADS)], axis=1) for h in range(GLA_HEADS)], axis=0))
        if need_y:
            for (rows, _, _), s_bd in zip(chunks, s_bds):
                o = _dot(qvr_ref[rows, _C_GQ:_C_GV], s_bd)
                gate = qvr_ref[rows, _C_GR:_C_FQ].astype(F32)
                parts = []
                for h in range(GLA_HEADS):
                    oh = o[:, h * GLA_DV:(h + 1) * GLA_DV]
                    msq = jnp.mean(oh * oh, axis=-1, keepdims=True)
                    parts.append(oh * lax.rsqrt(msq + EPS))
                y_ref[rows, :] = (jnp.concatenate(parts, axis=1) * gate).astype(BF16)
        return state

    nbg = GLA_BLOCKS_PER_TRIP if nblk % GLA_BLOCKS_PER_TRIP == 0 else 1
    state = tuple(s0_ref[0, h] for h in range(GLA_HEADS))
    state = lax.fori_loop(0, nblk // nbg, group, state) if nblk > 1 else group(0, state)
    for h in range(GLA_HEADS):
        sout_ref[0, h] = state[h]


def _gla(qvr, kT, laT, s0, nbatch, n, tw, chunk, need_y):
    s0_map = (lambda b: (0, 0, 0, 0)) if s0.shape[0] == 1 else (lambda b: (b, 0, 0, 0))
    kern = functools.partial(_gla_kernel, n=n, tw=tw, chunk=chunk, need_y=need_y)
    if not need_y:
        def kern(qvr_ref, kT_ref, laT_ref, s0_ref, sout_ref):
            _gla_kernel(qvr_ref, kT_ref, laT_ref, s0_ref, None, sout_ref,
                        n=n, tw=tw, chunk=chunk, need_y=False)
    state_shape = (GLA_HEADS, GLA_DK, GLA_DV)
    out_shape = [jax.ShapeDtypeStruct((nbatch,) + state_shape, F32)]
    out_specs = [pl.BlockSpec((1,) + state_shape, lambda b: (b, 0, 0, 0))]
    if need_y:
        out_shape = [jax.ShapeDtypeStruct((nbatch * n, GLA_V), BF16)] + out_shape
        out_specs = [pl.BlockSpec((n, GLA_V), lambda b: (b, 0))] + out_specs
    res = pl.pallas_call(
        kern, grid=(nbatch,),
        in_specs=[
            pl.BlockSpec((n, _C_FQ), lambda b: (b, 0)),
            pl.BlockSpec((1, GLA_QK, n), lambda b: (b, 0, 0)),
            pl.BlockSpec((1, GLA_QK, n), lambda b: (b, 0, 0)),
            pl.BlockSpec((1,) + state_shape, s0_map),
        ],
        out_specs=out_specs, out_shape=out_shape,
        compiler_params=pltpu.CompilerParams(
            dimension_semantics=("arbitrary",), vmem_limit_bytes=VMEM_LIMIT),
        name="gla",
    )(qvr, kT, laT, s0)
    return res if need_y else (None, res[0])


def _fold_lanes(x, op):
    w = x.shape[1]
    if w % LANES == 0:
        f = x[:, 0:LANES]
        for g in range(1, w // LANES):
            f = op(f, x[:, g * LANES:(g + 1) * LANES])
        return f
    assert op is jnp.add
    lane = lax.broadcasted_iota(jnp.int32, (x.shape[0], LANES), 1)
    return jnp.where(lane == 0, jnp.sum(x, axis=-1, keepdims=True), 0.0)


def _tile_lanes(x, w):
    return x[:, 0:w] if w < LANES else jnp.concatenate([x] * (w // LANES), axis=1)


def _fox_kernel(q_ref, kn_ref, vn_ref, kp_ref, vp_ref, crow_ref, cprow_ref, o_ref, ka_scr, va_scr, *scr,
                n, tq, tkp, p_pad, p_valid, pv_t, nhp):
    nq = n // tq
    rb = min(tq, 64)
    nh = 2 * nhp
    nbuf = len(scr) // (3 * nh)
    s_scr, p_scr, sh_scr = (scr[i * nh * nbuf:(i + 1) * nh * nbuf] for i in range(3))
    nparts = 3
    ones_rows = 16

    def c_block(c_row):
        parts = _split3(-c_row)
        r = lax.broadcasted_iota(jnp.int32, (FOX_DH, c_row.shape[1]), 0)
        blk = jnp.zeros((FOX_DH, c_row.shape[1]), F32)
        for i, part in enumerate(parts):
            blk = jnp.where(r == i, part.astype(F32), blk)
        return blk.astype(BF16)

    for h in range(nh):
        hp, hh = divmod(h, 2)
        hrows = slice(h * FOX_DH, (h + 1) * FOX_DH)
        for k_src, c_src, c0, w in ((kp_ref[0, h].astype(BF16), cprow_ref[0, hp, hh:hh + 1, :], 0, p_pad),
                                    (kn_ref[0, hrows, :], crow_ref[0, hp, hh:hh + 1, :], p_pad, n)):
            halves = (k_src, c_block(c_src)) if hh == 0 else (c_block(c_src), k_src)
            ka_scr[h, :, c0:c0 + w] = jnp.concatenate(halves, axis=0)
        va_scr[h, 0:FOX_DH, 0:p_pad] = vp_ref[0, h].astype(BF16)
        va_scr[h, 0:FOX_DH, p_pad:p_pad + n] = vn_ref[0, hrows, :]
        va_scr[h, FOX_DH:FOX_DH + ones_rows, :] = jnp.ones((ones_rows, p_pad + n), BF16)

    lane_q = lax.broadcasted_iota(jnp.int32, (tq, LANES), 1)
    eye = jnp.where(lax.broadcasted_iota(jnp.int32, (tq, tq), 0) == lax.broadcasted_iota(jnp.int32, (tq, tq), 1),
                    1.0, 0.0).astype(BF16)

    def block_chunks(qi):
        return [(c0, tkp) for c0 in range(0, p_pad, tkp)] + [(p_pad + j * tq, tq) for j in range(qi + 1)]

    def pass1_items(qi):
        rows = slice(qi * tq, (qi + 1) * tq)
        chunks = block_chunks(qi)
        state = {}

        def start():
            c_parts = [part for h in range(nh) for part in _split3(crow_ref[0, h // 2, h % 2:h % 2 + 1, rows])]
            pad_rows = -len(c_parts) % ones_rows
            c_rows = jnp.concatenate(c_parts + [jnp.zeros((pad_rows, tq), BF16)], axis=0)
            state["c_cols"] = _dot_nt(eye, c_rows)
            for h in range(nh):
                hp, hh = divmod(h, 2)
                qp = q_ref[rows, hp * LANES:(hp + 1) * LANES]
                c_lane0 = FOX_DH if hh == 0 else 0
                own = (lane_q < FOX_DH) if hh == 0 else (lane_q >= FOX_DH)
                ones_pat = jnp.where((lane_q >= c_lane0) & (lane_q < c_lane0 + nparts), 1.0, 0.0).astype(BF16)
                state["q", h] = jnp.where(own, qp, ones_pat)
                state["mx", h] = None

        def chunk(h, ci):
            c0, cw = chunks[ci]
            s = _dot(state["q", h], ka_scr[h, :, c0:c0 + cw])
            if c0 < p_pad and c0 + cw > p_valid:
                col = lax.broadcasted_iota(jnp.int32, s.shape, 1) + c0
                s = jnp.where(col < p_valid, s, NEG)
            if ci == len(chunks) - 1:
                r = lax.broadcasted_iota(jnp.int32, s.shape, 0)
                c = lax.broadcasted_iota(jnp.int32, s.shape, 1)
                s = jnp.where(c <= r, s, NEG)
            s_scr[buf(qi, h)][:, c0:c0 + cw] = s
            f = _fold_lanes(s, jnp.maximum) if cw % LANES == 0 else jnp.max(s, axis=-1, keepdims=True)
            mx = state["mx", h]
            if mx is None or mx.shape == f.shape:
                mx = f if mx is None else jnp.maximum(mx, f)
            else:
                mx = jnp.maximum(jnp.max(mx, axis=-1, keepdims=True), jnp.max(f, axis=-1, keepdims=True))
            state["mx", h] = mx

        def finish(h):
            cq = sum(state["c_cols"][:, h * nparts + i:h * nparts + i + 1] for i in range(nparts))
            m = jnp.max(state["mx", h], axis=-1, keepdims=True) + cq
            sh_scr[buf(qi, h)][...] = jnp.broadcast_to(m - cq, (tq, LANES))

        items = [start]
        for ci in range(len(chunks)):
            items += [functools.partial(chunk, h, ci) for h in range(nh)]
        return items + [functools.partial(finish, h) for h in range(nh)]

    def pass2_items(qi):
        def item(r0, h, c0, cw):
            rr = slice(r0, r0 + rb)
            sh = sh_scr[buf(qi, h)][rr, :]
            p = jnp.exp2(s_scr[buf(qi, h)][rr, c0:c0 + cw] - _tile_lanes(sh, cw))
            p_scr[buf(qi, h)][rr, c0:c0 + cw] = p.astype(BF16)
        return [functools.partial(item, r0, h, c0, cw)
                for r0 in range(0, tq, rb) for h in range(nh) for c0, cw in block_chunks(qi)]

    def pv_items(qi):
        w_tot = p_pad + (qi + 1) * tq
        acc = {}

        def matmul(h):
            p_all, v_aug = p_scr[buf(qi, h)][:, 0:w_tot], va_scr[h, :, 0:w_tot]
            acc[h] = _dot_nt(v_aug, p_all) if pv_t else _dot_nt(p_all, v_aug)

        def store():
            if pv_t:
                outs = [acc[h][0:FOX_DH] * (1.0 / acc[h][FOX_DH:FOX_DH + 1]) for h in range(nh)]
                o = jnp.concatenate(outs, axis=0).T
            else:
                outs = [acc[h][:, 0:FOX_DH] * (1.0 / acc[h][:, FOX_DH:FOX_DH + 1]) for h in range(nh)]
                o = jnp.concatenate(outs, axis=1)
            o_ref[qi * tq:(qi + 1) * tq, :] = o.astype(BF16)

        return [functools.partial(matmul, h) for h in range(nh)] + [store]

    def emit_merged(*lists):
        total = max(len(l) for l in lists)
        pos = [0] * len(lists)
        for t in range(total):
            for k, l in enumerate(lists):
                while pos[k] < len(l) and pos[k] * total <= t * len(l):
                    l[pos[k]]()
                    pos[k] += 1
        for k, l in enumerate(lists):
            for item in l[pos[k]:]:
                item()

    buf = lambda qi, h: (qi % nbuf) * nh + h
    emit_merged(pass1_items(0))
    for qi in range(nq):
        emit_merged(pass2_items(qi),
                    pass1_items(qi + 1) if qi + 1 < nq else [],
                    pv_items(qi - 1) if qi > 0 else [])
    emit_merged(pv_items(nq - 1))


def _fox(q, kn, vn, kp, vp, c_row, cp_row, nbatch, n, tq, tkp, p_valid, pv_t, nhp):
    nq = n // tq
    p_pad = kp.shape[3]
    pmap4 = (lambda b, hp: (0, hp, 0, 0)) if kp.shape[0] == 1 else (lambda b, hp: (b, hp, 0, 0))
    cmap4 = (lambda b, hp: (0, hp, 0, 0)) if cp_row.shape[0] == 1 else (lambda b, hp: (b, hp, 0, 0))
    nh = 2 * nhp
    tok = pl.BlockSpec((n, nhp * LANES), lambda b, hp: (b, hp))
    tlanes = pl.BlockSpec((1, nhp * LANES, n), lambda b, hp: (b, hp, 0))
    nbuf = min(nq, 2)
    return pl.pallas_call(
        functools.partial(_fox_kernel, n=n, tq=tq, tkp=tkp, p_pad=p_pad, p_valid=p_valid, pv_t=pv_t, nhp=nhp),
        grid=(nbatch, FOX_HEADS // nh),
        in_specs=[
            tok, tlanes, tlanes,
            pl.BlockSpec((1, nh, FOX_DH, p_pad), pmap4),
            pl.BlockSpec((1, nh, FOX_DH, p_pad), pmap4),
            pl.BlockSpec((1, nhp, 2, n), lambda b, hp: (b, hp, 0, 0)),
            pl.BlockSpec((1, nhp, 2, p_pad), cmap4),
        ],
        out_specs=tok,
        out_shape=jax.ShapeDtypeStruct((nbatch * n, FOX_W), BF16),
        scratch_shapes=([pltpu.VMEM((nh, LANES, p_pad + n), BF16), pltpu.VMEM((nh, FOX_DH + 16, p_pad + n), BF16)]
                        + [pltpu.VMEM((tq, p_pad + n), F32)] * (nh * nbuf)
                        + [pltpu.VMEM((tq, p_pad + n), BF16)] * (nh * nbuf)
                        + [pltpu.VMEM((tq, LANES), F32)] * (nh * nbuf)),
        compiler_params=pltpu.CompilerParams(
            dimension_semantics=("arbitrary", "arbitrary"), vmem_limit_bytes=VMEM_LIMIT),
        name="fox",
    )(q, kn, vn, kp, vp, c_row, cp_row)


def _merge_ffn_kernel(x_ref, ygla_ref, ofox_ref, sg_ref, wbg_ref, wbf_ref, wout_ref, gffn_ref,
                      wup_ref, wdown_ref, gfin_ref, y_ref, *, ff_chunk):
    ya = _dot(ygla_ref[...], wbg_ref[...])
    yb = _dot(ofox_ref[...], wbf_ref[...])
    m = sg_ref[:, 0:D_MODEL].astype(F32) * ya + sg_ref[:, D_MODEL:2 * D_MODEL].astype(F32) * yb
    h = x_ref[...] + _dot(m.astype(BF16), wout_ref[...])
    ms = jnp.mean(h * h, axis=-1, keepdims=True)
    hn = ((h * lax.rsqrt(ms + EPS)) * gffn_ref[...]).astype(BF16)
    for c in range(D_FF // ff_chunk):
        u = _dot(hn, wup_ref[:, c * ff_chunk:(c + 1) * ff_chunk])
        r = jnp.square(jnp.maximum(u, 0.0)).astype(BF16)
        h = h + _dot(r, wdown_ref[c * ff_chunk:(c + 1) * ff_chunk, :])
    ms = jnp.mean(h * h, axis=-1, keepdims=True)
    y_ref[...] = (h * lax.rsqrt(ms + EPS)) * gfin_ref[...]


def _merge_ffn(x2d, ygla, ofox, sg, w, tm):
    t_total = x2d.shape[0]
    tok = lambda width: pl.BlockSpec((tm, width), lambda i: (i, 0))
    return pl.pallas_call(
        functools.partial(_merge_ffn_kernel, ff_chunk=FFN_CHUNK),
        grid=(t_total // tm,),
        in_specs=[
            tok(D_MODEL), tok(GLA_V), tok(FOX_W), tok(2 * D_MODEL),
            _const_spec((GLA_V, D_MODEL)), _const_spec((FOX_W, D_MODEL)),
            _const_spec((D_MODEL, D_MODEL)), _const_spec((1, D_MODEL)),
            _const_spec((D_MODEL, D_FF)), _const_spec((D_FF, D_MODEL)), _const_spec((1, D_MODEL)),
        ],
        out_specs=tok(D_MODEL),
        out_shape=jax.ShapeDtypeStruct((t_total, D_MODEL), F32),
        compiler_params=pltpu.CompilerParams(
            dimension_semantics=("arbitrary",), vmem_limit_bytes=VMEM_LIMIT),
        name="merge_ffn",
    )(x2d, ygla, ofox, sg, w["w_bg"], w["w_bf"], w["w_out"], w["gn_ffn"], w["w_up"], w["w_down"],
      w["gn_final"])


def _prep_weights(norm_mix, w_in, w_gla_gate, b_gla_gate, g_gla_norm, b_fox_forget, w_branch_gla,
                  w_branch_fox, w_out, norm_ffn, w_up, w_down, norm_final):
    sizes = (GLA_QK, GLA_QK, GLA_V, GLA_V, GLA_RANK, FOX_W, FOX_W, FOX_W, FOX_HEADS, D_MODEL, D_MODEL)
    offs = [0]
    for s in sizes:
        offs.append(offs[-1] + s)
    col = lambda i: w_in[:, offs[i]:offs[i + 1]]
    gq, gk, gv, gr, glr, fq, fk, fv, ff, ga, gb = (col(i) for i in range(11))
    w_nn = jnp.concatenate([p.astype(BF16) for p in (
        gq * (GLA_DK ** -0.5), gv, gr, fq * (FOX_DH ** -0.5 * LOG2E), ga, gb)], axis=1)
    pad = jnp.zeros((D_MODEL, _R_FK - _R_FF - FOX_HEADS), F32)
    w_t = jnp.concatenate([gk, glr, ff, pad, fk, fv], axis=1).T.astype(BF16)
    return dict(
        gn_mix=norm_mix.reshape(1, D_MODEL), w_nn=w_nn, w_t=w_t,
        w_gate_t=w_gla_gate.T.astype(BF16), b_gate_col=b_gla_gate.reshape(GLA_QK, 1),
        b_f_col=b_fox_forget.reshape(FOX_HEADS, 1),
        w_bg=(jnp.tile(g_gla_norm, GLA_HEADS)[:, None] * w_branch_gla).astype(BF16), w_bf=w_branch_fox.astype(BF16), w_out=w_out.astype(BF16),
        gn_ffn=norm_ffn.reshape(1, D_MODEL), w_up=w_up.astype(BF16), w_down=w_down.astype(BF16),
        gn_final=norm_final.reshape(1, D_MODEL),
    )


def _pad_lanes(x):
    n = x.shape[-1]
    npad = -(-n // LANES) * LANES
    return x if npad == n else jnp.pad(x, [(0, 0)] * (x.ndim - 1) + [(0, npad - n)])


def kernel(x_prompt, x_sample, cache_fox_k, cache_fox_v, cache_fox_logf, state_gla, meta_tokens,
           norm_mix, w_in, w_gla_gate, b_gla_gate, g_gla_norm, b_fox_forget, w_branch_gla,
           w_branch_fox, w_out, norm_ffn, w_up, w_down, norm_final):
    bsz, seq, _ = x_prompt.shape
    dbsz, dseq, _ = x_sample.shape
    past = cache_fox_k.shape[3]
    w = _prep_weights(norm_mix[0], w_in[0], w_gla_gate[0], b_gla_gate[0], g_gla_norm[0],
                      b_fox_forget[0], w_branch_gla[0], w_branch_fox[0], w_out[0], norm_ffn[0],
                      w_up[0], w_down[0], norm_final)
    xp = x_prompt.reshape(bsz * seq, D_MODEL)
    xs = x_sample.reshape(dbsz * dseq, D_MODEL)

    (m_qvr, m_kT, m_laT, m_logf, _, _, _, m_kT32, m_vT32, _) = _inproj(
        meta_tokens.astype(F32), 1, N_META, 1, N_META, N_META, w)
    zero_state = jnp.zeros((1, GLA_HEADS, GLA_DK, GLA_DV), F32)
    _, s_meta = _gla(m_qvr, m_kT, m_laT, zero_state, 1, N_META, N_META, N_META, False)
    kp, vp = _pad_lanes(m_kT32), _pad_lanes(m_vT32)

    tm = ROW_TILE
    lead = (kp.reshape(FOX_W, LANES), vp.reshape(FOX_W, LANES))
    (p_qvr, p_kT, p_laT, p_logf, p_fq, p_fkT, p_fvT, p_kT32, p_vT32, p_sg) = _inproj(
        xp, bsz, seq, 1, tm, LANES, w, lead=lead)
    y_gla, s_p = _gla(p_qvr, p_kT, p_laT, s_meta, bsz, seq, LANES, CHUNK, True)
    lp = _pad_lanes(jnp.broadcast_to(m_logf, (bsz, FOX_HEADS, N_META)).reshape(bsz * FOX_HEADS, N_META))
    cp, cn = _decay(lp, p_logf.reshape(bsz * FOX_HEADS, seq), N_META)
    tq = FOX_TQ
    c_row = cn.reshape(bsz, FOX_HEADS // 2, 2, seq)
    cp_row = cp[:FOX_HEADS].reshape(1, FOX_HEADS // 2, 2, LANES)
    o_fox = _fox(p_fq, p_fkT, p_fvT, kp, vp, c_row, cp_row, bsz, seq, tq, LANES, N_META, True,
                 FOX_PAIRS_PROMPT)
    y_prompt = _merge_ffn(xp, y_gla, o_fox, p_sg, w, tm).reshape(bsz, seq, D_MODEL)

    (s_qvr, s_kT, s_laT, s_logf, s_fq, s_fkT, s_fvT, s_kT32, s_vT32, s_sg) = _inproj(
        xs, dbsz, dseq, dbsz, dseq, dseq, w)
    ys_gla, s_s = _gla(s_qvr, s_kT, s_laT, state_gla[0].astype(F32), dbsz, dseq, dseq, dseq, True)
    cps, cns = _decay(cache_fox_logf[0].astype(F32).reshape(dbsz * FOX_HEADS, past),
                      _pad_lanes(s_logf.reshape(dbsz * FOX_HEADS, dseq)), past)
    cs_row = cns[:, :dseq].reshape(dbsz, FOX_HEADS // 2, 2, dseq)
    cps_row = cps.reshape(dbsz, FOX_HEADS // 2, 2, past)
    os_fox = _fox(s_fq, s_fkT, s_fvT, jnp.swapaxes(cache_fox_k[0], 2, 3).astype(F32),
                  jnp.swapaxes(cache_fox_v[0], 2, 3).astype(F32), cs_row, cps_row,
                  dbsz, dseq, dseq, FOX_PREFIX_CHUNK, past, False, FOX_PAIRS_SAMPLE)
    y_sample = _merge_ffn(xs, ys_gla, os_fox, s_sg, w, dbsz * dseq).reshape(dbsz, dseq, D_MODEL)

    to_state = lambda a: jnp.swapaxes(a, 2, 3)[None]
    new_fox_logf_prompt = jnp.concatenate(
        [jnp.broadcast_to(m_logf, (bsz, FOX_HEADS, N_META)), p_logf], axis=2)[None]
    return (y_prompt, y_sample, to_state(p_kT32), to_state(p_vT32), new_fox_logf_prompt,
            s_p[None], to_state(s_kT32), to_state(s_vT32), s_logf[None], s_s[None])
```

```python
import functools

import jax
import jax.numpy as jnp
from jax import lax
from jax.experimental import pallas as pl
from jax.experimental.pallas import tpu as pltpu

F32 = jnp.float32
BF16 = jnp.bfloat16

D_MODEL = 1024
N_META = 16
EPS = 1e-6
GLA_HEADS = 4
GLA_DK = 64
GLA_DV = 128
GLA_RANK = 16
GLA_TAU = 16.0
GLA_QK = GLA_HEADS * GLA_DK
GLA_V = GLA_HEADS * GLA_DV
FOX_HEADS = 8
FOX_DH = 64
FOX_W = FOX_HEADS * FOX_DH
D_FF = 4 * D_MODEL
NEG = -1e30
CHUNK = 64
LOG2E = 1.4426950408889634

LANES = 128
VMEM_LIMIT = 56 * 1024 * 1024

ROW_TILE = 512
FFN_CHUNK = 1024
FOX_TQ = 256
FOX_PREFIX_CHUNK = 256
FOX_PAIRS_PROMPT = 2
FOX_PAIRS_SAMPLE = FOX_HEADS // 2
GLA_BLOCKS_PER_TRIP = 16

_C_GQ, _C_GV, _C_GR, _C_FQ, _C_GA, _C_GB, _C_END = 0, 256, 768, 1280, 1792, 2816, 3840
_R_K, _R_LR, _R_FF, _R_FK, _R_FV, _R_END = 0, 256, 272, 288, 800, 1312


def _dot(a, b):
    return jnp.dot(a, b, preferred_element_type=F32)


def _dot_nt(a, b):
    return lax.dot_general(a, b, (((1,), (1,)), ((), ())), preferred_element_type=F32)


def _split3(x):
    hi = x.astype(BF16)
    r = x - hi.astype(F32)
    mid = r.astype(BF16)
    lo = (r - mid.astype(F32)).astype(BF16)
    return hi, mid, lo


def _dot3(x, m):
    hi, mid, lo = _split3(x)
    return _dot(hi, m) + _dot(mid, m) + _dot(lo, m)


def _log_sigmoid(x):
    return jnp.minimum(x, 0.0) - jnp.log1p(jnp.exp(-jnp.abs(x)))


def _const_spec(shape):
    nd = len(shape)
    return pl.BlockSpec(shape, lambda *_: (0,) * nd, pipeline_mode=pl.Buffered(1))


def _inproj_kernel(x_ref, gn_ref, wnn_ref, wt_ref, wg_ref, bg_ref, bf_ref, *rest, nb, tmb, tw, nt, row_off):
    if row_off:
        lead_k_ref, lead_v_ref, x_first_ref = rest[:3]
        rest = rest[3:]
    (qvr_ref, kT_ref, laT_ref, logf_ref, fq_ref, fkT_ref, fvT_ref, k32_ref, v32_ref, sg_ref) = rest[:10]
    carries, hn_scr = rest[10:12], (rest[12] if row_off else None)
    tm = nb * tmb

    def normed(xr):
        x = xr[...]
        ms = jnp.mean(x * x, axis=-1, keepdims=True)
        return ((x * lax.rsqrt(ms + EPS)) * gn_ref[...]).astype(BF16)

    def project():
        if row_off:
            step = pl.program_id(0) * nt + pl.program_id(1)
            hn = hn_scr[lax.rem(step, 2)]
        else:
            hn = normed(x_ref)

        zT = _dot_nt(wt_ref[...], hn)
        if row_off:
            hn_scr[lax.rem(step + 1, 2)] = normed(x_ref)
        kT = zT[_R_K:_R_LR].astype(BF16)
        glrT = zT[_R_LR:_R_FF].astype(BF16)
        laT = _log_sigmoid(_dot(wg_ref[...], glrT) + bg_ref[...]) * (1.0 / GLA_TAU)
        logf = _log_sigmoid(zT[_R_FF:_R_FF + FOX_HEADS] + bf_ref[...])
        zk, zv = zT[_R_FK:_R_FV], zT[_R_FV:_R_END]
        for bb in range(nb):
            r0 = bb * tmb
            logf_ref[bb] = logf[:, r0:r0 + tmb]
            fkT_ref[bb] = zk[:, r0:r0 + tmb].astype(BF16)
            fvT_ref[bb] = zv[:, r0:r0 + tmb].astype(BF16)
            for j in range(tmb // tw):
                kT_ref[bb, j] = kT[:, r0 + j * tw:r0 + (j + 1) * tw]
                laT_ref[bb, j] = laT[:, r0 + j * tw:r0 + (j + 1) * tw]
            if not row_off:
                for h in range(FOX_HEADS):
                    k32_ref[bb, h] = zk[h * FOX_DH:(h + 1) * FOX_DH, r0:r0 + tmb]
                    v32_ref[bb, h] = zv[h * FOX_DH:(h + 1) * FOX_DH, r0:r0 + tmb]
        if row_off:
            lane = lax.broadcasted_iota(jnp.int32, (FOX_W, LANES), 1)
            for z, carry, o_ref in ((zk, carries[0], k32_ref), (zv, carries[1], v32_ref)):
                rolled = pltpu.roll(z, row_off, axis=1)
                first = jnp.where(lane < row_off, carry[...], rolled[:, 0:LANES])
                carry[...] = rolled[:, 0:LANES]
                for h in range(FOX_HEADS):
                    hr = slice(h * FOX_DH, (h + 1) * FOX_DH)
                    o_ref[0, h, :, 0:LANES] = first[hr]
                    o_ref[0, h, :, LANES:tm] = rolled[hr, LANES:tm]

        sg_ref[...] = jax.nn.sigmoid(_dot(hn, wnn_ref[:, _C_GA:_C_END])).astype(BF16)
        zq = _dot(hn, wnn_ref[:, _C_GQ:_C_FQ])
        qvr_ref[:, _C_GQ:_C_GR] = zq[:, _C_GQ:_C_GR].astype(BF16)
        qvr_ref[:, _C_GR:_C_FQ] = jax.nn.silu(zq[:, _C_GR:_C_FQ]).astype(BF16)
        fq_ref[...] = _dot(hn, wnn_ref[:, _C_FQ:_C_GA]).astype(BF16)

    if not row_off:
        project()
        return

    i = pl.program_id(1)

    @pl.when(i == 0)
    def _():
        carries[0][...] = lead_k_ref[...]
        carries[1][...] = lead_v_ref[...]

    @pl.when((pl.program_id(0) == 0) & (i == 0))
    def _():
        hn_scr[0] = normed(x_first_ref)

    pl.when(i < nt)(project)

    @pl.when(i == nt)
    def _():
        for carry, o_ref in ((carries[0], k32_ref), (carries[1], v32_ref)):
            for h in range(FOX_HEADS):
                o_ref[0, h, :, 0:LANES] = carry[h * FOX_DH:(h + 1) * FOX_DH, :]


def _inproj(x2d, nbatch, n, nb, tmb, tw, w, lead=None):
    tm = nb * tmb
    ngrp, nt = nbatch // nb, n // tmb
    t_total = nbatch * n
    row_off = N_META if lead is not None else 0
    steps = nt + 1 if row_off else nt
    blk = (lambda i: jnp.minimum(i, nt - 1)) if row_off else (lambda i: i)
    tok = lambda width: pl.BlockSpec((tm, width), lambda g, i: (g * nt + blk(i), 0))
    out_shape = (
        jax.ShapeDtypeStruct((t_total, _C_FQ), BF16),
        jax.ShapeDtypeStruct((nbatch, n // tw, GLA_QK, tw), BF16),
        jax.ShapeDtypeStruct((nbatch, n // tw, GLA_QK, tw), F32),
        jax.ShapeDtypeStruct((nbatch, FOX_HEADS, n), F32),
        jax.ShapeDtypeStruct((t_total, FOX_W), BF16),
        jax.ShapeDtypeStruct((nbatch, FOX_W, n), BF16),
        jax.ShapeDtypeStruct((nbatch, FOX_W, n), BF16),
        jax.ShapeDtypeStruct((nbatch, FOX_HEADS, FOX_DH, row_off + n), F32),
        jax.ShapeDtypeStruct((nbatch, FOX_HEADS, FOX_DH, row_off + n), F32),
        jax.ShapeDtypeStruct((t_total, 2 * D_MODEL), BF16),
    )
    tblk = pl.BlockSpec((nb, tmb // tw, GLA_QK, tw), lambda g, i: (g, blk(i), 0, 0))
    lblk = pl.BlockSpec((nb, FOX_W, tmb), lambda g, i: (g, 0, blk(i)))
    sblk = pl.BlockSpec((nb, FOX_HEADS, FOX_DH, tmb), lambda g, i: (g, 0, 0, i))
    out_specs = (
        tok(_C_FQ), tblk, tblk,
        pl.BlockSpec((nb, FOX_HEADS, tmb), lambda g, i: (g, 0, blk(i))),
        tok(FOX_W), lblk, lblk, sblk, sblk, tok(2 * D_MODEL),
    )
    if row_off:
        x_spec = pl.BlockSpec((tm, D_MODEL), lambda g, i: (jnp.minimum(g * nt + blk(i) + 1, ngrp * nt - 1), 0))
    else:
        x_spec = tok(D_MODEL)
    in_specs = [
        x_spec,
        _const_spec((1, D_MODEL)),
        _const_spec((D_MODEL, _C_END)),
        _const_spec((_R_END, D_MODEL)),
        _const_spec((GLA_QK, GLA_RANK)),
        _const_spec((GLA_QK, 1)),
        _const_spec((FOX_HEADS, 1)),
    ]
    args = [x2d, w["gn_mix"], w["w_nn"], w["w_t"], w["w_gate_t"], w["b_gate_col"], w["b_f_col"]]
    scratch = []
    if row_off:
        assert nb == 1 and tmb % LANES == 0
        in_specs += [_const_spec((FOX_W, LANES))] * 2 + [_const_spec((tm, D_MODEL))]
        args += list(lead) + [x2d]
        scratch = [pltpu.VMEM((FOX_W, LANES), F32)] * 2 + [pltpu.VMEM((2, tm, D_MODEL), BF16)]
    return pl.pallas_call(
        functools.partial(_inproj_kernel, nb=nb, tmb=tmb, tw=tw, nt=nt, row_off=row_off),
        grid=(ngrp, steps), in_specs=in_specs, out_specs=out_specs, out_shape=out_shape,
        scratch_shapes=scratch,
        compiler_params=pltpu.CompilerParams(
            dimension_semantics=("arbitrary", "arbitrary"), vmem_limit_bytes=VMEM_LIMIT),
        name="inproj",
    )(*args)


def _cumsum_lanes(x, carry):
    n = x.shape[1]
    r = lax.broadcasted_iota(jnp.int32, (LANES, LANES), 0)
    c = lax.broadcasted_iota(jnp.int32, (LANES, LANES), 1)
    upper = jnp.where(r <= c, 1.0, 0.0).astype(BF16)
    outs = []
    for j in range(n // LANES):
        cj = _dot3(x[:, j * LANES:(j + 1) * LANES], upper) + carry
        outs.append(cj)
        carry = cj[:, LANES - 1:LANES]
    return jnp.concatenate(outs, axis=1), carry


def _decay_kernel(lp_ref, ln_ref, cp_ref, cn_ref, *, p_valid):
    lp = lp_ref[...]
    lane = lax.broadcasted_iota(jnp.int32, lp.shape, 1)
    lp = jnp.where(lane < p_valid, lp, 0.0)
    zero = jnp.zeros((lp.shape[0], 1), F32)
    cp, tot = _cumsum_lanes(lp, zero)
    cp_ref[...] = cp * LOG2E
    cn, _ = _cumsum_lanes(ln_ref[...], tot)
    cn_ref[...] = cn * LOG2E


def _decay(logf_p, logf_n, p_valid):
    r, pp = logf_p.shape
    npad = logf_n.shape[1]
    return pl.pallas_call(
        functools.partial(_decay_kernel, p_valid=p_valid),
        out_shape=(jax.ShapeDtypeStruct((r, pp), F32), jax.ShapeDtypeStruct((r, npad), F32)),
        compiler_params=pltpu.CompilerParams(vmem_limit_bytes=VMEM_LIMIT),
        name="decay",
    )(logf_p, logf_n)


def _gla_kernel(qvr_ref, kT_ref, laT_ref, s0_ref, y_ref, sout_ref, *, n, tw, chunk, need_y):
    nblk = n // tw
    ncb = tw // chunk

    sr = lax.broadcasted_iota(jnp.int32, (tw, LANES), 0)
    tr = lax.broadcasted_iota(jnp.int32, (tw, tw), 0)
    tc = lax.broadcasted_iota(jnp.int32, (tw, tw), 1)
    m_cum = ((tr // chunk) == (tc // chunk)) & (tr <= tc)
    m_all = jnp.concatenate(
        [jnp.where((sr // chunk) == ci, 1.0, 0.0) for ci in range(ncb)] + [jnp.where(m_cum, 1.0, 0.0)],
        axis=1).astype(BF16)
    stack_k = tw % LANES == 0
    m_rhs = jnp.concatenate([m_all] * 2, axis=0) if stack_k else m_all
    lane_chunk = lax.broadcasted_iota(jnp.int32, (GLA_QK, tw), 1) // chunk
    zero_blk = jnp.zeros((GLA_DK, GLA_DV), BF16)

    def group(jg, state):
        chunks = []
        for t in range(nbg):
            jb = jg * nbg + t
            la = laT_ref[0, jb]
            hi = la.astype(BF16)
            lo = (la - hi.astype(F32)).astype(BF16)
            if stack_k:
                g = _dot(jnp.concatenate([hi, lo], axis=1), m_rhs)
            else:
                g = _dot(hi, m_rhs) + _dot(lo, m_rhs)
            tot = [g[:, ci * LANES:(ci + 1) * LANES] for ci in range(ncb)]
            g_cum = g[:, ncb * LANES:ncb * LANES + tw]
            g_tot = tot[0][:, 0:tw] if tw <= LANES else jnp.concatenate([tot[0]] * (tw // LANES), axis=1)
            for ci in range(1, ncb):
                g_tot = jnp.where(lane_chunk == ci, tot[ci][:, 0:tw], g_tot)
            kdec = (kT_ref[0, jb].astype(F32) * jnp.exp(g_tot - g_cum)).astype(BF16)
            for ci in range(ncb):
                rows = pl.ds(pl.multiple_of(jb * tw, tw) + ci * chunk, chunk)
                v = qvr_ref[rows, _C_GV:_C_GR]
                us = [_dot(kdec[h * GLA_DK:(h + 1) * GLA_DK, ci * chunk:(ci + 1) * chunk],
                           v[:, h * GLA_DV:(h + 1) * GLA_DV]) for h in range(GLA_HEADS)]
                chunks.append((rows, jnp.exp(tot[ci]), us))
        s_bds = []
        for rows, a, us in chunks:
            state = tuple(a[h * GLA_DK:(h + 1) * GLA_DK] * state[h] + us[h] for h in range(GLA_HEADS))
            if need_y:
                s_bds.append(jnp.concatenate(
                    [jnp.concatenate([state[h].astype(BF16) if hc == h else zero_blk
                                      for hc in range(GLA_HEADS)], axis=1) for h in range(GLA_HEADS)], axis=0))
        if need_y:
            for (rows, _, _), s_bd in zip(chunks, s_bds):
                o = _dot(qvr_ref[rows, _C_GQ:_C_GV], s_bd)
                gate = qvr_ref[rows, _C_GR:_C_FQ].astype(F32)
                parts = []
                for h in range(GLA_HEADS):
                    oh = o[:, h * GLA_DV:(h + 1) * GLA_DV]
                    msq = jnp.mean(oh * oh, axis=-1, keepdims=True)
                    parts.append(oh * lax.rsqrt(msq + EPS))
                y_ref[rows, :] = (jnp.concatenate(parts, axis=1) * gate).astype(BF16)
        return state

    nbg = GLA_BLOCKS_PER_TRIP if nblk % GLA_BLOCKS_PER_TRIP == 0 else 1
    state = tuple(s0_ref[0, h] for h in range(GLA_HEADS))
    state = lax.fori_loop(0, nblk // nbg, group, state)
    for h in range(GLA_HEADS):
        sout_ref[0, h] = state[h]


def _gla(qvr, kT, laT, s0, nbatch, n, tw, chunk, need_y):
    s0_map = (lambda b: (0, 0, 0, 0)) if s0.shape[0] == 1 else (lambda b: (b, 0, 0, 0))
    kern = functools.partial(_gla_kernel, n=n, tw=tw, chunk=chunk, need_y=need_y)
    if not need_y:
        def kern(qvr_ref, kT_ref, laT_ref, s0_ref, sout_ref):
            _gla_kernel(qvr_ref, kT_ref, laT_ref, s0_ref, None, sout_ref,
                        n=n, tw=tw, chunk=chunk, need_y=False)
    state_shape = (GLA_HEADS, GLA_DK, GLA_DV)
    out_shape = [jax.ShapeDtypeStruct((nbatch,) + state_shape, F32)]
    out_specs = [pl.BlockSpec((1,) + state_shape, lambda b: (b, 0, 0, 0))]
    if need_y:
        out_shape = [jax.ShapeDtypeStruct((nbatch * n, GLA_V), BF16)] + out_shape
        out_specs = [pl.BlockSpec((n, GLA_V), lambda b: (b, 0))] + out_specs
    res = pl.pallas_call(
        kern, grid=(nbatch,),
        in_specs=[
            pl.BlockSpec((n, _C_FQ), lambda b: (b, 0)),
            pl.BlockSpec((1, n // tw, GLA_QK, tw), lambda b: (b, 0, 0, 0)),
            pl.BlockSpec((1, n // tw, GLA_QK, tw), lambda b: (b, 0, 0, 0)),
            pl.BlockSpec((1,) + state_shape, s0_map),
        ],
        out_specs=out_specs, out_shape=out_shape,
        compiler_params=pltpu.CompilerParams(
            dimension_semantics=("arbitrary",), vmem_limit_bytes=VMEM_LIMIT),
        name="gla",
    )(qvr, kT, laT, s0)
    return res if need_y else (None, res[0])


def _fold_lanes(x, op):
    w = x.shape[1]
    if w % LANES == 0:
        f = x[:, 0:LANES]
        for g in range(1, w // LANES):
            f = op(f, x[:, g * LANES:(g + 1) * LANES])
        return f
    assert op is jnp.add
    lane = lax.broadcasted_iota(jnp.int32, (x.shape[0], LANES), 1)
    return jnp.where(lane == 0, jnp.sum(x, axis=-1, keepdims=True), 0.0)


def _tile_lanes(x, w):
    return x[:, 0:w] if w < LANES else jnp.concatenate([x] * (w // LANES), axis=1)


def _fox_kernel(q_ref, kn_ref, vn_ref, kp_ref, vp_ref, crow_ref, cprow_ref, o_ref, ka_scr, va_scr, *scr,
                n, tq, tkp, p_pad, p_valid, pv_t, nhp):
    nq = n // tq
    rb = min(tq, 128)
    nh = 2 * nhp
    nbuf = len(scr) // (3 * nh)
    s_scr, p_scr, sh_scr = (scr[i * nh * nbuf:(i + 1) * nh * nbuf] for i in range(3))
    nparts = 3
    ones_rows = 16

    def c_block(c_row):
        parts = _split3(-c_row)
        r = lax.broadcasted_iota(jnp.int32, (FOX_DH, c_row.shape[1]), 0)
        blk = jnp.zeros((FOX_DH, c_row.shape[1]), F32)
        for i, part in enumerate(parts):
            blk = jnp.where(r == i, part.astype(F32), blk)
        return blk.astype(BF16)

    for h in range(nh):
        hp, hh = divmod(h, 2)
        hrows = slice(h * FOX_DH, (h + 1) * FOX_DH)
        for k_src, c_src, c0, w in ((kp_ref[0, h].astype(BF16), cprow_ref[0, hp, hh:hh + 1, :], 0, p_pad),
                                    (kn_ref[0, hrows, :], crow_ref[0, hp, hh:hh + 1, :], p_pad, n)):
            halves = (k_src, c_block(c_src)) if hh == 0 else (c_block(c_src), k_src)
            ka_scr[h, :, c0:c0 + w] = jnp.concatenate(halves, axis=0)
        va_scr[h, 0:FOX_DH, 0:p_pad] = vp_ref[0, h].astype(BF16)
        va_scr[h, 0:FOX_DH, p_pad:p_pad + n] = vn_ref[0, hrows, :]
        va_scr[h, FOX_DH:FOX_DH + ones_rows, :] = jnp.ones((ones_rows, p_pad + n), BF16)

    lane_q = lax.broadcasted_iota(jnp.int32, (tq, LANES), 1)
    eye = jnp.where(lax.broadcasted_iota(jnp.int32, (tq, tq), 0) == lax.broadcasted_iota(jnp.int32, (tq, tq), 1),
                    1.0, 0.0).astype(BF16)

    def block_chunks(qi):
        return [(c0, tkp) for c0 in range(0, p_pad, tkp)] + [(p_pad + j * tq, tq) for j in range(qi + 1)]

    def pass1_items(qi):
        rows = slice(qi * tq, (qi + 1) * tq)
        chunks = block_chunks(qi)
        state = {}

        def start():
            c_parts = [part for h in range(nh) for part in _split3(crow_ref[0, h // 2, h % 2:h % 2 + 1, rows])]
            pad_rows = -len(c_parts) % ones_rows
            c_rows = jnp.concatenate(c_parts + [jnp.zeros((pad_rows, tq), BF16)], axis=0)
            state["c_cols"] = _dot_nt(eye, c_rows)
            for h in range(nh):
                hp, hh = divmod(h, 2)
                qp = q_ref[rows, hp * LANES:(hp + 1) * LANES]
                c_lane0 = FOX_DH if hh == 0 else 0
                own = (lane_q < FOX_DH) if hh == 0 else (lane_q >= FOX_DH)
                ones_pat = jnp.where((lane_q >= c_lane0) & (lane_q < c_lane0 + nparts), 1.0, 0.0).astype(BF16)
                state["q", h] = jnp.where(own, qp, ones_pat)
                state["mx", h] = None

        def chunk(h, ci):
            c0, cw = chunks[ci]
            s = _dot(state["q", h], ka_scr[h, :, c0:c0 + cw])
            if c0 < p_pad and c0 + cw > p_valid:
                col = lax.broadcasted_iota(jnp.int32, s.shape, 1) + c0
                s = jnp.where(col < p_valid, s, NEG)
            if ci == len(chunks) - 1:
                r = lax.broadcasted_iota(jnp.int32, s.shape, 0)
                c = lax.broadcasted_iota(jnp.int32, s.shape, 1)
                s = jnp.where(c <= r, s, NEG)
            s_scr[buf(qi, h)][:, c0:c0 + cw] = s
            f = _fold_lanes(s, jnp.maximum) if cw % LANES == 0 else jnp.max(s, axis=-1, keepdims=True)
            mx = state["mx", h]
            if mx is None or mx.shape == f.shape:
                mx = f if mx is None else jnp.maximum(mx, f)
            else:
                mx = jnp.maximum(jnp.max(mx, axis=-1, keepdims=True), jnp.max(f, axis=-1, keepdims=True))
            state["mx", h] = mx

        def finish(h):
            cq = sum(state["c_cols"][:, h * nparts + i:h * nparts + i + 1] for i in range(nparts))
            m = jnp.max(state["mx", h], axis=-1, keepdims=True) + cq
            sh_scr[buf(qi, h)][...] = jnp.broadcast_to(m - cq, (tq, LANES))

        items = [start]
        for ci in range(len(chunks)):
            items += [functools.partial(chunk, h, ci) for h in range(nh)]
        return items + [functools.partial(finish, h) for h in range(nh)]

    def pass2_items(qi):
        def item(r0, h, c0, cw):
            rr = slice(r0, r0 + rb)
            sh = sh_scr[buf(qi, h)][rr, :]
            p = jnp.exp2(s_scr[buf(qi, h)][rr, c0:c0 + cw] - _tile_lanes(sh, cw))
            p_scr[buf(qi, h)][rr, c0:c0 + cw] = p.astype(BF16)
        return [functools.partial(item, r0, h, c0, cw)
                for r0 in range(0, tq, rb) for h in range(nh) for c0, cw in block_chunks(qi)]

    def pv_items(qi):
        w_tot = p_pad + (qi + 1) * tq
        acc = {}

        def matmul(h):
            p_all, v_aug = p_scr[buf(qi, h)][:, 0:w_tot], va_scr[h, :, 0:w_tot]
            acc[h] = _dot_nt(v_aug, p_all) if pv_t else _dot_nt(p_all, v_aug)

        def store():
            if pv_t:
                outs = [acc[h][0:FOX_DH] * (1.0 / acc[h][FOX_DH:FOX_DH + 1]) for h in range(nh)]
                o = jnp.concatenate(outs, axis=0).T
            else:
                outs = [acc[h][:, 0:FOX_DH] * (1.0 / acc[h][:, FOX_DH:FOX_DH + 1]) for h in range(nh)]
                o = jnp.concatenate(outs, axis=1)
            o_ref[qi * tq:(qi + 1) * tq, :] = o.astype(BF16)

        return [functools.partial(matmul, h) for h in range(nh)] + [store]

    def emit_merged(*lists):
        total = max(len(l) for l in lists)
        pos = [0] * len(lists)
        for t in range(total):
            for k, l in enumerate(lists):
                while pos[k] < len(l) and pos[k] * total <= t * len(l):
                    l[pos[k]]()
                    pos[k] += 1
        for k, l in enumerate(lists):
            for item in l[pos[k]:]:
                item()

    buf = lambda qi, h: (qi % nbuf) * nh + h
    emit_merged(pass1_items(0))
    for qi in range(nq):
        emit_merged(pass2_items(qi),
                    pass1_items(qi + 1) if qi + 1 < nq else [],
                    pv_items(qi - 1) if qi > 0 else [])
    emit_merged(pv_items(nq - 1))


def _fox(q, kn, vn, kp, vp, c_row, cp_row, nbatch, n, tq, tkp, p_valid, pv_t, nhp):
    nq = n // tq
    p_pad = kp.shape[3]
    pmap4 = (lambda b, hp: (0, hp, 0, 0)) if kp.shape[0] == 1 else (lambda b, hp: (b, hp, 0, 0))
    cmap4 = (lambda b, hp: (0, hp, 0, 0)) if cp_row.shape[0] == 1 else (lambda b, hp: (b, hp, 0, 0))
    nh = 2 * nhp
    tok = pl.BlockSpec((n, nhp * LANES), lambda b, hp: (b, hp))
    tlanes = pl.BlockSpec((1, nhp * LANES, n), lambda b, hp: (b, hp, 0))
    nbuf = min(nq, 2)
    return pl.pallas_call(
        functools.partial(_fox_kernel, n=n, tq=tq, tkp=tkp, p_pad=p_pad, p_valid=p_valid, pv_t=pv_t, nhp=nhp),
        grid=(nbatch, FOX_HEADS // nh),
        in_specs=[
            tok, tlanes, tlanes,
            pl.BlockSpec((1, nh, FOX_DH, p_pad), pmap4),
            pl.BlockSpec((1, nh, FOX_DH, p_pad), pmap4),
            pl.BlockSpec((1, nhp, 2, n), lambda b, hp: (b, hp, 0, 0)),
            pl.BlockSpec((1, nhp, 2, p_pad), cmap4),
        ],
        out_specs=tok,
        out_shape=jax.ShapeDtypeStruct((nbatch * n, FOX_W), BF16),
        scratch_shapes=([pltpu.VMEM((nh, LANES, p_pad + n), BF16), pltpu.VMEM((nh, FOX_DH + 16, p_pad + n), BF16)]
                        + [pltpu.VMEM((tq, p_pad + n), F32)] * (nh * nbuf)
                        + [pltpu.VMEM((tq, p_pad + n), BF16)] * (nh * nbuf)
                        + [pltpu.VMEM((tq, LANES), F32)] * (nh * nbuf)),
        compiler_params=pltpu.CompilerParams(
            dimension_semantics=("arbitrary", "arbitrary"), vmem_limit_bytes=VMEM_LIMIT),
        name="fox",
    )(q, kn, vn, kp, vp, c_row, cp_row)


def _merge_ffn_kernel(x_ref, ygla_ref, ofox_ref, sg_ref, wbg_ref, wbf_ref, wout_ref, gffn_ref,
                      wup_ref, wdown_ref, gfin_ref, y_ref, *, ff_chunk):
    ya = _dot(ygla_ref[...], wbg_ref[...])
    yb = _dot(ofox_ref[...], wbf_ref[...])
    m = sg_ref[:, 0:D_MODEL].astype(F32) * ya + sg_ref[:, D_MODEL:2 * D_MODEL].astype(F32) * yb
    h = x_ref[...] + _dot(m.astype(BF16), wout_ref[...])
    ms = jnp.mean(h * h, axis=-1, keepdims=True)
    hn = ((h * lax.rsqrt(ms + EPS)) * gffn_ref[...]).astype(BF16)
    for c in range(D_FF // ff_chunk):
        u = _dot(hn, wup_ref[:, c * ff_chunk:(c + 1) * ff_chunk])
        r = jnp.square(jnp.maximum(u, 0.0)).astype(BF16)
        h = h + _dot(r, wdown_ref[c * ff_chunk:(c + 1) * ff_chunk, :])
    ms = jnp.mean(h * h, axis=-1, keepdims=True)
    y_ref[...] = (h * lax.rsqrt(ms + EPS)) * gfin_ref[...]


def _merge_ffn(x2d, ygla, ofox, sg, w, tm):
    t_total = x2d.shape[0]
    tok = lambda width: pl.BlockSpec((tm, width), lambda i: (i, 0))
    return pl.pallas_call(
        functools.partial(_merge_ffn_kernel, ff_chunk=FFN_CHUNK),
        grid=(t_total // tm,),
        in_specs=[
            tok(D_MODEL), tok(GLA_V), tok(FOX_W), tok(2 * D_MODEL),
            _const_spec((GLA_V, D_MODEL)), _const_spec((FOX_W, D_MODEL)),
            _const_spec((D_MODEL, D_MODEL)), _const_spec((1, D_MODEL)),
            _const_spec((D_MODEL, D_FF)), _const_spec((D_FF, D_MODEL)), _const_spec((1, D_MODEL)),
        ],
        out_specs=tok(D_MODEL),
        out_shape=jax.ShapeDtypeStruct((t_total, D_MODEL), F32),
        compiler_params=pltpu.CompilerParams(
            dimension_semantics=("arbitrary",), vmem_limit_bytes=VMEM_LIMIT),
        name="merge_ffn",
    )(x2d, ygla, ofox, sg, w["w_bg"], w["w_bf"], w["w_out"], w["gn_ffn"], w["w_up"], w["w_down"],
      w["gn_final"])


def _prep_weights(norm_mix, w_in, w_gla_gate, b_gla_gate, g_gla_norm, b_fox_forget, w_branch_gla,
                  w_branch_fox, w_out, norm_ffn, w_up, w_down, norm_final):
    sizes = (GLA_QK, GLA_QK, GLA_V, GLA_V, GLA_RANK, FOX_W, FOX_W, FOX_W, FOX_HEADS, D_MODEL, D_MODEL)
    offs = [0]
    for s in sizes:
        offs.append(offs[-1] + s)
    col = lambda i: w_in[:, offs[i]:offs[i + 1]]
    gq, gk, gv, gr, glr, fq, fk, fv, ff, ga, gb = (col(i) for i in range(11))
    w_nn = jnp.concatenate([p.astype(BF16) for p in (
        gq * (GLA_DK ** -0.5), gv, gr, fq * (FOX_DH ** -0.5 * LOG2E), ga, gb)], axis=1)
    pad = jnp.zeros((D_MODEL, _R_FK - _R_FF - FOX_HEADS), F32)
    w_t = jnp.concatenate([gk, glr, ff, pad, fk, fv], axis=1).T.astype(BF16)
    return dict(
        gn_mix=norm_mix.reshape(1, D_MODEL), w_nn=w_nn, w_t=w_t,
        w_gate_t=w_gla_gate.T.astype(BF16), b_gate_col=b_gla_gate.reshape(GLA_QK, 1),
        b_f_col=b_fox_forget.reshape(FOX_HEADS, 1),
        w_bg=(jnp.tile(g_gla_norm, GLA_HEADS)[:, None] * w_branch_gla).astype(BF16), w_bf=w_branch_fox.astype(BF16), w_out=w_out.astype(BF16),
        gn_ffn=norm_ffn.reshape(1, D_MODEL), w_up=w_up.astype(BF16), w_down=w_down.astype(BF16),
        gn_final=norm_final.reshape(1, D_MODEL),
    )


def _pad_lanes(x):
    n = x.shape[-1]
    npad = -(-n // LANES) * LANES
    return x if npad == n else jnp.pad(x, [(0, 0)] * (x.ndim - 1) + [(0, npad - n)])


def kernel(x_prompt, x_sample, cache_fox_k, cache_fox_v, cache_fox_logf, state_gla, meta_tokens,
           norm_mix, w_in, w_gla_gate, b_gla_gate, g_gla_norm, b_fox_forget, w_branch_gla,
           w_branch_fox, w_out, norm_ffn, w_up, w_down, norm_final):
    bsz, seq, _ = x_prompt.shape
    dbsz, dseq, _ = x_sample.shape
    past = cache_fox_k.shape[3]
    w = _prep_weights(norm_mix[0], w_in[0], w_gla_gate[0], b_gla_gate[0], g_gla_norm[0],
                      b_fox_forget[0], w_branch_gla[0], w_branch_fox[0], w_out[0], norm_ffn[0],
                      w_up[0], w_down[0], norm_final)
    xp = x_prompt.reshape(bsz * seq, D_MODEL)
    xs = x_sample.reshape(dbsz * dseq, D_MODEL)

    (m_qvr, m_kT, m_laT, m_logf, _, _, _, m_kT32, m_vT32, _) = _inproj(
        meta_tokens.astype(F32), 1, N_META, 1, N_META, N_META, w)
    zero_state = jnp.zeros((1, GLA_HEADS, GLA_DK, GLA_DV), F32)
    _, s_meta = _gla(m_qvr, m_kT, m_laT, zero_state, 1, N_META, N_META, N_META, False)
    kp, vp = _pad_lanes(m_kT32), _pad_lanes(m_vT32)

    tm = ROW_TILE
    lead = (kp.reshape(FOX_W, LANES), vp.reshape(FOX_W, LANES))
    (p_qvr, p_kT, p_laT, p_logf, p_fq, p_fkT, p_fvT, p_kT32, p_vT32, p_sg) = _inproj(
        xp, bsz, seq, 1, tm, LANES, w, lead=lead)
    y_gla, s_p = _gla(p_qvr, p_kT, p_laT, s_meta, bsz, seq, LANES, CHUNK, True)
    lp = _pad_lanes(jnp.broadcast_to(m_logf, (bsz, FOX_HEADS, N_META)).reshape(bsz * FOX_HEADS, N_META))
    cp, cn = _decay(lp, p_logf.reshape(bsz * FOX_HEADS, seq), N_META)
    tq = FOX_TQ
    c_row = cn.reshape(bsz, FOX_HEADS // 2, 2, seq)
    cp_row = cp[:FOX_HEADS].reshape(1, FOX_HEADS // 2, 2, LANES)
    o_fox = _fox(p_fq, p_fkT, p_fvT, kp, vp, c_row, cp_row, bsz, seq, tq, LANES, N_META, True,
                 FOX_PAIRS_PROMPT)
    y_prompt = _merge_ffn(xp, y_gla, o_fox, p_sg, w, tm).reshape(bsz, seq, D_MODEL)

    (s_qvr, s_kT, s_laT, s_logf, s_fq, s_fkT, s_fvT, s_kT32, s_vT32, s_sg) = _inproj(
        xs, dbsz, dseq, dbsz, dseq, dseq, w)
    ys_gla, s_s = _gla(s_qvr, s_kT, s_laT, state_gla[0].astype(F32), dbsz, dseq, dseq, dseq, True)
    cps, cns = _decay(cache_fox_logf[0].astype(F32).reshape(dbsz * FOX_HEADS, past),
                      _pad_lanes(s_logf.reshape(dbsz * FOX_HEADS, dseq)), past)
    cs_row = cns[:, :dseq].reshape(dbsz, FOX_HEADS // 2, 2, dseq)
    cps_row = cps.reshape(dbsz, FOX_HEADS // 2, 2, past)
    os_fox = _fox(s_fq, s_fkT, s_fvT, jnp.swapaxes(cache_fox_k[0], 2, 3).astype(F32),
                  jnp.swapaxes(cache_fox_v[0], 2, 3).astype(F32), cs_row, cps_row,
                  dbsz, dseq, dseq, FOX_PREFIX_CHUNK, past, False, FOX_PAIRS_SAMPLE)
    y_sample = _merge_ffn(xs, ys_gla, os_fox, s_sg, w, dbsz * dseq).reshape(dbsz, dseq, D_MODEL)

    to_state = lambda a: jnp.swapaxes(a, 2, 3)[None]
    new_fox_logf_prompt = jnp.concatenate(
        [jnp.broadcast_to(m_logf, (bsz, FOX_HEADS, N_META)), p_logf], axis=2)[None]
    return (y_prompt, y_sample, to_state(p_kT32), to_state(p_vT32), new_fox_logf_prompt,
            s_p[None], to_state(s_kT32), to_state(s_vT32), s_logf[None], s_s[None])
```
